```python
import math
import jax
import jax.numpy as jnp
from jax import lax
import numpy as np

D_MODEL = 4096
BATCH = 4
SEQ = 4096
DEPTH = 2

HEAD_DIM = 128
N_HEADS = D_MODEL // HEAD_DIM
H_A = N_HEADS // 4
H_C = N_HEADS // 4
H_B = N_HEADS - H_A - H_C
KV_B = max(1, H_B // 8)
DIFF_DIM = HEAD_DIM // 2
MOBA_BLOCK = 256
MOBA_TOPK = 3
MOBA_QCHUNK = 16
WINDOW = 128
DIFF_QBLOCK = 128
N_BUCKETS = 32
MAX_DISTANCE = 128
D_FF = -(-8 * D_MODEL // (3 * 256)) * 256
ALPHA = (2.0 * DEPTH) ** 0.25
BETA = (8.0 * DEPTH) ** -0.25
LN_EPS = 1e-5
NEG = -1e30

QA_W = H_A * HEAD_DIM
QB_W = H_B * HEAD_DIM
KB_W = KV_B * HEAD_DIM
QC_W = H_C * 2 * DIFF_DIM
VC_W = H_C * HEAD_DIM
PROJ_SIZES = (QA_W, QA_W, QA_W, QB_W, KB_W, KB_W, QC_W, QC_W, VC_W)
V_SEGMENTS = (2, 5, 8)
SPLIT_POINTS = tuple(sum(PROJ_SIZES[:i + 1]) for i in range(len(PROJ_SIZES) - 1))
D_PROJ = sum(PROJ_SIZES)
D_CAT = QA_W + QB_W + VC_W

kernel_name = "hybrid_moba_swa_diff_adaln_deepnorm"


def rel_bucket(dist):
    n = jnp.maximum(dist, 0)
    max_exact = N_BUCKETS // 2
    nf = jnp.maximum(n, 1).astype(jnp.float32)
    large = max_exact + (jnp.log(nf / max_exact) / math.log(MAX_DISTANCE / max_exact)
                         * (N_BUCKETS - max_exact)).astype(jnp.int32)
    large = jnp.minimum(large, N_BUCKETS - 1)
    return jnp.where(n < max_exact, n, large)


def layer_norm(x, g, b):
    xf = x.astype(jnp.float32)
    mu = jnp.mean(xf, axis=-1, keepdims=True)
    var = jnp.mean(jnp.square(xf - mu), axis=-1, keepdims=True)
    return ((xf - mu) * lax.rsqrt(var + LN_EPS) * g + b).astype(x.dtype)


def rms_norm(x, g):
    xf = x.astype(jnp.float32)
    return (xf * lax.rsqrt(jnp.mean(jnp.square(xf), axis=-1, keepdims=True) + LN_EPS) * g).astype(x.dtype)


def moba_attention(q, k, v, tab):
    B, H, S, d = q.shape
    nb = -(-S // MOBA_BLOCK)
    pad = nb * MOBA_BLOCK - S
    kp = jnp.pad(k, ((0, 0), (0, 0), (0, pad), (0, 0))).reshape(B, H, nb, MOBA_BLOCK, d)
    vp = jnp.pad(v, ((0, 0), (0, 0), (0, pad), (0, 0))).reshape(B, H, nb, MOBA_BLOCK, d)
    k_mean = jnp.mean(kp.astype(jnp.float32), axis=3)
    qblk = jnp.arange(S) // MOBA_BLOCK
    gate = jnp.einsum('bhsd,bhnd->bhsn', q.astype(jnp.float32), k_mean)
    fully_past = jnp.arange(nb)[None, :] < qblk[:, None]
    gate = jnp.where(fully_past, gate, -jnp.inf)
    n_sel = min(MOBA_TOPK, nb)
    _, sel = lax.top_k(gate, n_sel)
    sel_ok = jnp.arange(n_sel)[None, :] < qblk[:, None]
    scale = d ** -0.5
    bi = jnp.arange(B)[:, None, None, None]
    hi = jnp.arange(H)[None, :, None, None]
    hi5 = jnp.arange(H)[None, :, None, None, None]
    offs = jnp.arange(MOBA_BLOCK)

    def chunk(ci):
        t0 = ci * MOBA_QCHUNK
        qc = lax.dynamic_slice_in_dim(q, t0, MOBA_QCHUNK, axis=2)
        sc = lax.dynamic_slice_in_dim(sel, t0, MOBA_QCHUNK, axis=2)
        ok = lax.dynamic_slice_in_dim(sel_ok, t0, MOBA_QCHUNK, axis=0)
        tq = t0 + jnp.arange(MOBA_QCHUNK)
        kg = kp[bi, hi, sc]
        vg = vp[bi, hi, sc]
        s_sel = jnp.einsum('bhqd,bhqnkd->bhqnk', qc, kg).astype(jnp.float32) * scale
        dist_sel = tq[None, None, :, None, None] - (sc[..., None] * MOBA_BLOCK + offs)
        s_sel = s_sel + tab[hi5, rel_bucket(dist_sel)]
        s_sel = jnp.where(ok[None, None, :, :, None], s_sel, NEG)
        ob = t0 // MOBA_BLOCK
        ko = lax.dynamic_slice_in_dim(kp, ob, 1, axis=2)[:, :, 0]
        vo = lax.dynamic_slice_in_dim(vp, ob, 1, axis=2)[:, :, 0]
        s_own = jnp.einsum('bhqd,bhkd->bhqk', qc, ko).astype(jnp.float32) * scale
        dist_own = tq[:, None] - (ob * MOBA_BLOCK + offs)[None, :]
        s_own = s_own + tab[:, rel_bucket(dist_own)][None]
        s_own = jnp.where((dist_own >= 0)[None, None], s_own, NEG)
        logits = jnp.concatenate([s_sel.reshape(B, H, MOBA_QCHUNK, n_sel * MOBA_BLOCK), s_own], axis=-1)
        p = jax.nn.softmax(logits, axis=-1).astype(v.dtype)
        p_sel = p[..., :n_sel * MOBA_BLOCK].reshape(B, H, MOBA_QCHUNK, n_sel, MOBA_BLOCK)
        p_own = p[..., n_sel * MOBA_BLOCK:]
        return (jnp.einsum('bhqnk,bhqnkd->bhqd', p_sel, vg)
                + jnp.einsum('bhqk,bhkd->bhqd', p_own, vo))

    outs = lax.map(chunk, jnp.arange(S // MOBA_QCHUNK))
    return outs.transpose(1, 2, 0, 3, 4).reshape(B, H, S, d)


def swa_sink_attention(q, k, v, sinks, tab):
    B, S, H, d = q.shape
    KV = k.shape[2]
    G = H // KV
    nq = S // WINDOW
    qb = q.reshape(B, nq, WINDOW, KV, G, d)
    kb = k.reshape(B, nq, WINDOW, KV, d)
    vb = v.reshape(B, nq, WINDOW, KV, d)
    kband = jnp.concatenate([jnp.pad(kb, ((0, 0), (1, 0), (0, 0), (0, 0), (0, 0)))[:, :-1], kb], axis=2)
    vband = jnp.concatenate([jnp.pad(vb, ((0, 0), (1, 0), (0, 0), (0, 0), (0, 0)))[:, :-1], vb], axis=2)
    s = jnp.einsum('bnqkgd,bnjkd->bkgnqj', qb, kband).astype(jnp.float32) * (d ** -0.5)
    i = jnp.arange(WINDOW)
    j = jnp.arange(2 * WINDOW)
    dist = i[:, None] + WINDOW - j[None, :]
    band = (dist >= 0) & (dist < WINDOW)
    valid = band[None] & ((jnp.arange(nq)[:, None, None] > 0) | (j[None, None, :] >= WINDOW))
    bias = tab[:, rel_bucket(dist)].reshape(KV, G, WINDOW, 2 * WINDOW)
    s = s + bias[None, :, :, None]
    s = jnp.where(valid[None, None, None], s, NEG)
    sink = jnp.broadcast_to(sinks.astype(jnp.float32).reshape(1, KV, G, 1, 1, 1), s.shape[:-1] + (1,))
    p = jax.nn.softmax(jnp.concatenate([s, sink], axis=-1), axis=-1)[..., :-1].astype(v.dtype)
    out = jnp.einsum('bkgnqj,bnjkd->bnqkgd', p, vband)
    return out.reshape(B, S, H, d)


def diff_attention(q, k, v, lam, subln_g, lam_init, tab):
    B, H, _, S, dd = q.shape
    scale = dd ** -0.5
    kpos = jnp.arange(S)

    def block(bi):
        t0 = bi * DIFF_QBLOCK
        qc = lax.dynamic_slice_in_dim(q, t0, DIFF_QBLOCK, axis=3)
        s = jnp.einsum('bhmqd,bhmkd->bhmqk', qc, k).astype(jnp.float32) * scale
        dist = (t0 + jnp.arange(DIFF_QBLOCK))[:, None] - kpos[None, :]
        s = s + tab[:, rel_bucket(dist)][None, :, None]
        s = jnp.where((dist >= 0)[None, None, None], s, NEG)
        p = jax.nn.softmax(s, axis=-1)
        a = (p[:, :, 0] - lam * p[:, :, 1]).astype(v.dtype)
        return jnp.einsum('bhqk,bhkd->bhqd', a, v)

    o = lax.map(block, jnp.arange(S // DIFF_QBLOCK))
    o = o.transpose(1, 2, 0, 3, 4).reshape(B, H, S, 2 * dd)
    return rms_norm(o, subln_g) * (1.0 - lam_init)


def hybrid_layer(x, c, tab_a, tab_b, tab_c, w_ada, b_ada, w_in, w_o, sinks, lam_p, subln_g,
                 ln_g, ln_b, w_gate, w_up, w_down, layer_idx):
    B, S, D = x.shape
    mod = jax.nn.silu(c) @ w_ada + b_ada
    sh1, sc1, g1, sh2, sc2, g2 = jnp.split(mod, 6, axis=-1)
    h = x * (1.0 + sc1[:, None]) + sh1[:, None]
    proj = h @ w_in
    qa, ka, va, qb, kb, vb, qc, kc, vc = jnp.split(proj, SPLIT_POINTS, axis=-1)
    to_heads = lambda t, n: t.reshape(B, S, n, HEAD_DIM).transpose(0, 2, 1, 3)
    o_a = moba_attention(to_heads(qa, H_A), to_heads(ka, H_A), to_heads(va, H_A), tab_a)
    o_a = o_a.transpose(0, 2, 1, 3).reshape(B, S, QA_W)
    o_b = swa_sink_attention(qb.reshape(B, S, H_B, HEAD_DIM), kb.reshape(B, S, KV_B, HEAD_DIM),
                             vb.reshape(B, S, KV_B, HEAD_DIM), sinks, tab_b).reshape(B, S, QB_W)
    lam_init = 0.8 - 0.6 * math.exp(-0.3 * layer_idx)
    lp = lam_p.astype(jnp.float32)
    lam = jnp.exp(jnp.sum(lp[0] * lp[1])) - jnp.exp(jnp.sum(lp[2] * lp[3])) + lam_init
    to_diff = lambda t: t.reshape(B, S, H_C, 2, DIFF_DIM).transpose(0, 2, 3, 1, 4)
    o_c = diff_attention(to_diff(qc), to_diff(kc), to_heads(vc, H_C), lam, subln_g, lam_init, tab_c)
    o_c = o_c.transpose(0, 2, 1, 3).reshape(B, S, VC_W)
    mix = jnp.concatenate([o_a, o_b, o_c], axis=-1) @ w_o
    x = layer_norm(ALPHA * x + g1[:, None] * mix, ln_g[0], ln_b[0])
    h = x * (1.0 + sc2[:, None]) + sh2[:, None]
    ff = (jax.nn.silu(h @ w_gate) * (h @ w_up)) @ w_down
    return layer_norm(ALPHA * x + g2[:, None] * ff, ln_g[1], ln_b[1])


def setup_inputs(seed: int = 0) -> dict:
    key = jax.random.key(seed)
    ks = jax.random.split(key, 16)
    f32 = jnp.float32
    nrm = lambda k, shape: jax.random.normal(k, shape, f32)
    col_scale = jnp.concatenate([jnp.full((n,), BETA if i in V_SEGMENTS else 1.0, f32)
                                 for i, n in enumerate(PROJ_SIZES)])
    return {
        "x": nrm(ks[0], (BATCH, SEQ, D_MODEL)),
        "c": nrm(ks[1], (BATCH, D_MODEL)),
        "rel_bias": 0.3 * nrm(ks[2], (N_BUCKETS, N_HEADS)),
        "w_ada": nrm(ks[3], (DEPTH, D_MODEL, 6 * D_MODEL)) * (0.5 * D_MODEL ** -0.5),
        "b_ada": 0.01 * nrm(ks[4], (DEPTH, 6 * D_MODEL)),
        "w_in": nrm(ks[5], (DEPTH, D_MODEL, D_PROJ)) * (D_MODEL ** -0.5) * col_scale,
        "w_o": nrm(ks[6], (DEPTH, D_CAT, D_MODEL)) * (BETA * D_CAT ** -0.5),
        "attn_sinks": nrm(ks[7], (DEPTH, H_B)),
        "diff_lambda": 0.1 * nrm(ks[8], (DEPTH, 4, DIFF_DIM)),
        "diff_subln_g": 1.0 + 0.02 * nrm(ks[9], (DEPTH, HEAD_DIM)),
        "ln_g": 1.0 + 0.02 * nrm(ks[10], (DEPTH, 2, D_MODEL)),
        "ln_b": 0.02 * nrm(ks[11], (DEPTH, 2, D_MODEL)),
        "w_gate": nrm(ks[12], (DEPTH, D_MODEL, D_FF)) * (D_MODEL ** -0.5),
        "w_up": nrm(ks[13], (DEPTH, D_MODEL, D_FF)) * (D_MODEL ** -0.5),
        "w_down": nrm(ks[14], (DEPTH, D_FF, D_MODEL)) * (BETA * D_FF ** -0.5),
    }


def reference(x, c, rel_bias, w_ada, b_ada, w_in, w_o, attn_sinks, diff_lambda, diff_subln_g,
              ln_g, ln_b, w_gate, w_up, w_down):
    tab_a = rel_bias[:, :H_A].T
    tab_b = rel_bias[:, H_A:H_A + H_B].T
    tab_c = rel_bias[:, H_A + H_B:].T
    for l in range(DEPTH):
        x = hybrid_layer(x, c, tab_a, tab_b, tab_c, w_ada[l], b_ada[l], w_in[l], w_o[l],
                         attn_sinks[l], diff_lambda[l], diff_subln_g[l], ln_g[l], ln_b[l],
                         w_gate[l], w_up[l], w_down[l], l)
    return x
```

```python
import functools
import math

import numpy as np
import jax
import jax.numpy as jnp
from jax import lax
from jax.experimental import pallas as pl
from jax.experimental.pallas import tpu as pltpu

D_MODEL = 4096
DEPTH = 2
HEAD_DIM = 128
N_HEADS = D_MODEL // HEAD_DIM
H_A = N_HEADS // 4
H_C = N_HEADS // 4
H_B = N_HEADS - H_A - H_C
KV_B = max(1, H_B // 8)
G_B = H_B // KV_B
DIFF_DIM = HEAD_DIM // 2
MOBA_BLOCK = 256
MOBA_TOPK = 3
WINDOW = 128
N_BUCKETS = 32
MAX_DISTANCE = 128
D_FF = -(-8 * D_MODEL // (3 * 256)) * 256
ALPHA = (2.0 * DEPTH) ** 0.25
LN_EPS = 1e-5
NEG = -1e30

QA_W = H_A * HEAD_DIM
QB_W = H_B * HEAD_DIM
KB_W = KV_B * HEAD_DIM
QC_W = H_C * 2 * DIFF_DIM
VC_W = H_C * HEAD_DIM
PROJ_SIZES = (QA_W, QA_W, QA_W, QB_W, KB_W, KB_W, QC_W, QC_W, VC_W)
PROJ_OFFS = tuple(int(sum(PROJ_SIZES[:i])) for i in range(len(PROJ_SIZES)))
D_PROJ = sum(PROJ_SIZES)

LANE = 128
D_FF_PAD = -(-D_FF // 1024) * 1024
VMEM_LIMIT = 56 * 1024 * 1024

F32 = jnp.float32
BF16 = jnp.bfloat16


def _bucket_upper_bounds():
    n = np.arange(0, 4 * MAX_DISTANCE, dtype=np.int64)
    max_exact = N_BUCKETS // 2
    nf = np.maximum(n, 1).astype(np.float32)
    large = max_exact + (np.log(nf / np.float32(max_exact)) / np.float32(math.log(MAX_DISTANCE / max_exact))
                         * np.float32(N_BUCKETS - max_exact)).astype(np.int32)
    large = np.minimum(large, N_BUCKETS - 1)
    bucket = np.where(n < max_exact, n, large)
    assert bucket[-1] == N_BUCKETS - 1 and np.all(np.diff(bucket) >= 0)
    return tuple(int(np.argmax(bucket > k)) for k in range(N_BUCKETS - 1))


BUCKET_UPPER = _bucket_upper_bounds()
FAR_DIST = BUCKET_UPPER[-1]


def _bias_from_dist(dist, tab):
    b = jnp.full(dist.shape, tab(N_BUCKETS - 1), F32)
    for k in range(N_BUCKETS - 2, -1, -1):
        b = jnp.where(dist < BUCKET_UPPER[k], tab(k), b)
    return b


def _dot(a, b):
    return jnp.dot(a, b, preferred_element_type=F32)


def _dot_nt(a, b):
    return lax.dot_general(a, b, (((1,), (1,)), ((), ())), preferred_element_type=F32)


def _params(*sem):
    return pltpu.CompilerParams(dimension_semantics=sem, vmem_limit_bytes=VMEM_LIMIT)


def _mod_kernel(c_ref, w_ref, b_ref, o_ref):
    c = c_ref[...]
    s = (c * jax.nn.sigmoid(c)).astype(BF16)
    o_ref[...] = _dot(s, w_ref[...].astype(BF16)) + b_ref[...]


def _ada_mod(c8, w_ada, b_ada3):
    depth, d, n = w_ada.shape
    tn = 512
    return pl.pallas_call(
        _mod_kernel,
        grid=(depth, n // tn),
        in_specs=[pl.BlockSpec((8, d), lambda l, j: (0, 0)),
                  pl.BlockSpec((None, d, tn), lambda l, j: (l, 0, j)),
                  pl.BlockSpec((None, 1, tn), lambda l, j: (l, 0, j))],
        out_specs=pl.BlockSpec((None, 8, tn), lambda l, j: (l, 0, j)),
        out_shape=jax.ShapeDtypeStruct((depth, 8, n), F32),
        compiler_params=_params("parallel", "parallel"),
        name="ada_mod",
    )(c8, w_ada, b_ada3)


def _modulate_kernel(x_ref, sc_ref, sh_ref, h_ref):
    h_ref[...] = (x_ref[...] * (1.0 + sc_ref[...]) + sh_ref[...]).astype(BF16)


def _modulate(x, sc, sh):
    b, s, d = x.shape
    tm = 512
    vec = pl.BlockSpec((None, 1, d), lambda i, j: (i, 0, 0))
    return pl.pallas_call(
        _modulate_kernel,
        grid=(b, s // tm),
        in_specs=[pl.BlockSpec((None, tm, d), lambda i, j: (i, j, 0)), vec, vec],
        out_specs=pl.BlockSpec((None, tm, d), lambda i, j: (i, j, 0)),
        out_shape=jax.ShapeDtypeStruct((b, s, d), BF16),
        compiler_params=_params("parallel", "parallel"),
        name="modulate",
    )(x, sc, sh)


def _mm_kernel(x_ref, w_ref, o_ref):
    o_ref[...] = _dot(x_ref[...], w_ref[...]).astype(o_ref.dtype)


def _matmul(x, w, out_dtype, tm, tn, name):
    m, k = x.shape
    _, n = w.shape
    return pl.pallas_call(
        _mm_kernel,
        grid=(m // tm, n // tn),
        in_specs=[pl.BlockSpec((tm, k), lambda i, j: (i, 0)),
                  pl.BlockSpec((k, tn), lambda i, j: (0, j))],
        out_specs=pl.BlockSpec((tm, tn), lambda i, j: (i, j)),
        out_shape=jax.ShapeDtypeStruct((m, n), out_dtype),
        compiler_params=_params("parallel", "arbitrary"),
        name=name,
    )(x, w)


def _mm_acc_kernel(x_ref, w_ref, o_ref):
    part = _dot(x_ref[...], w_ref[...])

    @pl.when(pl.program_id(2) == 0)
    def _():
        o_ref[...] = part

    @pl.when(pl.program_id(2) > 0)
    def _():
        o_ref[...] += part


def _matmul_ksplit(x, w, tm, tn, tk, name):
    m, k = x.shape
    _, n = w.shape
    return pl.pallas_call(
        _mm_acc_kernel,
        grid=(m // tm, n // tn, k // tk),
        in_specs=[pl.BlockSpec((tm, tk), lambda i, j, kk: (i, kk)),
                  pl.BlockSpec((tk, tn), lambda i, j, kk: (kk, j))],
        out_specs=pl.BlockSpec((tm, tn), lambda i, j, kk: (i, j)),
        out_shape=jax.ShapeDtypeStruct((m, n), F32),
        compiler_params=_params("parallel", "parallel", "arbitrary"),
        name=name,
    )(x, w)


def _gate_up_kernel(h_ref, wg_ref, wu_ref, o_ref):
    h = h_ref[...]
    g = _dot(h, wg_ref[...])
    u = _dot(h, wu_ref[...])
    o_ref[...] = (g * jax.nn.sigmoid(g) * u).astype(BF16)


def _gate_up(h, wg, wu, tm, tf):
    m, k = h.shape
    _, f = wg.shape
    wspec = pl.BlockSpec((k, tf), lambda i, j: (0, j))
    return pl.pallas_call(
        _gate_up_kernel,
        grid=(m // tm, f // tf),
        in_specs=[pl.BlockSpec((tm, k), lambda i, j: (i, 0)), wspec, wspec],
        out_specs=pl.BlockSpec((tm, tf), lambda i, j: (i, j)),
        out_shape=jax.ShapeDtypeStruct((m, f), BF16),
        compiler_params=_params("parallel", "arbitrary"),
        name="ffn_gate_up",
    )(h, wg, wu)


def _norm_kernel(with_h, x_ref, y_ref, gate_ref, lg_ref, lb_ref, *rest):
    z = ALPHA * x_ref[...] + gate_ref[...] * y_ref[...]
    mu = jnp.mean(z, axis=-1, keepdims=True)
    zc = z - mu
    var = jnp.mean(zc * zc, axis=-1, keepdims=True)
    xn = zc * lax.rsqrt(var + LN_EPS) * lg_ref[...] + lb_ref[...]
    if with_h:
        sc_ref, sh_ref, xo_ref, h_ref = rest
        xo_ref[...] = xn
        h_ref[...] = (xn * (1.0 + sc_ref[...]) + sh_ref[...]).astype(BF16)
    else:
        (xo_ref,) = rest
        xo_ref[...] = xn


def _residual_norm(x, y, gate, lg, lb, sc=None, sh=None):
    b, s, d = x.shape
    tm = 256
    with_h = sc is not None
    tile = pl.BlockSpec((None, tm, d), lambda i, j: (i, j, 0))
    vec = pl.BlockSpec((None, 1, d), lambda i, j: (i, 0, 0))
    par = pl.BlockSpec((1, d), lambda i, j: (0, 0))
    in_specs = [tile, tile, vec, par, par]
    args = [x, y, gate, lg, lb]
    out_specs = [tile]
    out_shape = [jax.ShapeDtypeStruct((b, s, d), F32)]
    if with_h:
        in_specs += [vec, vec]
        args += [sc, sh]
        out_specs.append(tile)
        out_shape.append(jax.ShapeDtypeStruct((b, s, d), BF16))
    return pl.pallas_call(
        functools.partial(_norm_kernel, with_h),
        grid=(b, s // tm),
        in_specs=in_specs,
        out_specs=out_specs,
        out_shape=out_shape,
        compiler_params=_params("parallel", "parallel"),
        name="residual_norm",
    )(*args)


def _softmax_init(s, v, m_ref, l_ref, acc_ref):
    m = jnp.max(s, axis=1, keepdims=True)
    p = jnp.exp(s - m)
    m_ref[...] = m
    l_ref[...] = jnp.sum(p, axis=1, keepdims=True)
    acc_ref[...] = _dot(p.astype(BF16), v)


def _softmax_update(s, v, m_ref, l_ref, acc_ref):
    m_old = m_ref[...]
    m = jnp.maximum(m_old, jnp.max(s, axis=1, keepdims=True))
    corr = jnp.exp(m_old - m)
    p = jnp.exp(s - m)
    m_ref[...] = m
    l_ref[...] = corr * l_ref[...] + jnp.sum(p, axis=1, keepdims=True)
    acc_ref[...] = corr * acc_ref[...] + _dot(p.astype(BF16), v)


def _near_bias_tiles(tab, t, d0_ref, d1_ref):
    i = lax.broadcasted_iota(jnp.int32, (t, t), 0)
    j = lax.broadcasted_iota(jnp.int32, (t, t), 1)
    d0_ref[...] = _bias_from_dist(i - j, tab)
    d1_ref[...] = _bias_from_dist(i - j + t, tab)


def _moba_kernel(tab_ref, q_ref, k_ref, v_ref, o_ref, kmean_ref, d0_ref, d1_ref, m_ref, l_ref, acc_ref):
    h = pl.program_id(1)
    qi = pl.program_id(2)
    t = MOBA_BLOCK
    nb = k_ref.shape[0] // t
    scale = HEAD_DIM ** -0.5
    tab = lambda k: tab_ref[h, k]

    @pl.when(qi == 0)
    def _():
        for jb in range(nb):
            kb = k_ref[jb * t:(jb + 1) * t, :].astype(F32)
            kmean_ref[jb:jb + 1, :] = jnp.sum(kb, axis=0, keepdims=True) * (1.0 / t)
        _near_bias_tiles(tab, t, d0_ref, d1_ref)

    q = q_ref[...]
    km = kmean_ref[...]
    km_hi = km.astype(BF16)
    km_lo = (km - km_hi.astype(F32)).astype(BF16)
    gate = _dot_nt(q, km_hi) + _dot_nt(q, km_lo)
    col = lax.broadcasted_iota(jnp.int32, gate.shape, 1)
    g = jnp.where(col < qi, gate, -jnp.inf)
    picks = []
    for r in range(MOBA_TOPK):
        mx = jnp.max(g, axis=1, keepdims=True)
        first = jnp.min(jnp.where(g == mx, col, nb), axis=1, keepdims=True)
        picks.append(jnp.where(r < qi, first, -1))
        g = jnp.where(col == first, -jnp.inf, g)

    def selected(jb):
        return (picks[0] == jb) | (picks[1] == jb) | (picks[2] == jb)

    start = pl.multiple_of(qi * t, t)
    row = lax.broadcasted_iota(jnp.int32, (t, t), 0)
    colk = lax.broadcasted_iota(jnp.int32, (t, t), 1)
    s = _dot_nt(q, k_ref[pl.ds(start, t), :]) * scale + d0_ref[...]
    s = jnp.where(row >= colk, s, NEG)
    _softmax_init(s, v_ref[pl.ds(start, t), :], m_ref, l_ref, acc_ref)

    @pl.when(qi > 0)
    def _():
        st = pl.multiple_of((qi - 1) * t, t)
        s1 = _dot_nt(q, k_ref[pl.ds(st, t), :]) * scale + d1_ref[...]
        s1 = jnp.where(selected(qi - 1), s1, NEG)
        _softmax_update(s1, v_ref[pl.ds(st, t), :], m_ref, l_ref, acc_ref)

    far = tab(N_BUCKETS - 1)

    def body(jb, carry):
        st = pl.multiple_of(jb * t, t)
        s2 = _dot_nt(q, k_ref[pl.ds(st, t), :]) * scale + far
        s2 = jnp.where(selected(jb), s2, NEG)
        _softmax_update(s2, v_ref[pl.ds(st, t), :], m_ref, l_ref, acc_ref)
        return carry

    lax.fori_loop(0, jnp.maximum(qi - 1, 0), body, 0)
    o_ref[...] = (acc_ref[...] / l_ref[...]).astype(o_ref.dtype)


def _moba(proj, tab_a):
    b, s, _ = proj.shape
    t = MOBA_BLOCK
    assert s % t == 0 and t >= FAR_DIST
    qo, ko, vo = (PROJ_OFFS[i] // HEAD_DIM for i in (0, 1, 2))
    return pl.pallas_call(
        _moba_kernel,
        grid=(b, H_A, s // t),
        in_specs=[pl.BlockSpec(memory_space=pltpu.SMEM),
                  pl.BlockSpec((None, t, HEAD_DIM), lambda bi, h, qi: (bi, qi, qo + h)),
                  pl.BlockSpec((None, s, HEAD_DIM), lambda bi, h, qi: (bi, 0, ko + h)),
                  pl.BlockSpec((None, s, HEAD_DIM), lambda bi, h, qi: (bi, 0, vo + h))],
        out_specs=pl.BlockSpec((None, t, HEAD_DIM), lambda bi, h, qi: (bi, qi, h)),
        out_shape=jax.ShapeDtypeStruct((b, s, QA_W), BF16),
        scratch_shapes=[pltpu.VMEM((s // t, HEAD_DIM), F32),
                        pltpu.VMEM((t, t), F32), pltpu.VMEM((t, t), F32),
                        pltpu.VMEM((t, 1), F32), pltpu.VMEM((t, 1), F32),
                        pltpu.VMEM((t, HEAD_DIM), F32)],
        compiler_params=_params("parallel", "parallel", "arbitrary"),
        name="moba_attention",
    )(tab_a, proj, proj, proj)


def _swa_kernel(tab_ref, sink_ref, q_ref, k_ref, v_ref, o_ref, bias_ref, sinkcol_ref):
    g = pl.program_id(1)
    ti = pl.program_id(2)
    w = WINDOW
    rows = G_B * w
    nsub = q_ref.shape[0] // w
    scale = HEAD_DIM ** -0.5
    i = lax.broadcasted_iota(jnp.int32, (w, 2 * w), 0)
    j = lax.broadcasted_iota(jnp.int32, (w, 2 * w), 1)
    dist = i + w - j

    @pl.when(ti == 0)
    def _():
        for hh in range(G_B):
            head = g * G_B + hh
            bias_ref[hh * w:(hh + 1) * w, :] = _bias_from_dist(dist, lambda k: tab_ref[head, k])
            sinkcol_ref[hh * w:(hh + 1) * w, :] = jnp.full((w, 1), sink_ref[head], F32)

    band = jnp.concatenate([(dist >= 0) & (dist < w)] * G_B, axis=0)
    own = jnp.concatenate([j >= w] * G_B, axis=0)
    sink = sinkcol_ref[...]

    def body(n, carry):
        blk = ti * nsub + n
        r0 = pl.multiple_of(n * w, w)
        k0 = pl.multiple_of(jnp.maximum(blk - 1, 0) * w, w)
        k1 = pl.multiple_of(blk * w, w)
        qs = jnp.concatenate([q_ref[pl.ds(r0, w), hh * HEAD_DIM:(hh + 1) * HEAD_DIM] for hh in range(G_B)], axis=0)
        kband = jnp.concatenate([k_ref[pl.ds(k0, w), :], k_ref[pl.ds(k1, w), :]], axis=0)
        vband = jnp.concatenate([v_ref[pl.ds(k0, w), :], v_ref[pl.ds(k1, w), :]], axis=0)
        s = _dot_nt(qs, kband) * scale + bias_ref[...]
        valid = band & ((blk > 0) | own)
        s = jnp.where(valid, s, NEG)
        m = jnp.maximum(jnp.max(s, axis=1, keepdims=True), sink)
        p = jnp.exp(s - m)
        l = jnp.sum(p, axis=1, keepdims=True) + jnp.exp(sink - m)
        o = _dot(p.astype(BF16), vband) / l
        for hh in range(G_B):
            o_ref[pl.ds(r0, w), hh * HEAD_DIM:(hh + 1) * HEAD_DIM] = o[hh * w:(hh + 1) * w, :].astype(o_ref.dtype)
        return carry

    lax.fori_loop(0, nsub, body, 0)


def _swa(proj, tab_b, sinks):
    b, s, _ = proj.shape
    tq = 1024
    gw = G_B * HEAD_DIM
    qo = PROJ_OFFS[3] // gw
    ko, vo = (PROJ_OFFS[i] // HEAD_DIM for i in (4, 5))
    assert PROJ_OFFS[3] % gw == 0 and s % tq == 0
    smem = pl.BlockSpec(memory_space=pltpu.SMEM)
    return pl.pallas_call(
        _swa_kernel,
        grid=(b, KV_B, s // tq),
        in_specs=[smem, smem,
                  pl.BlockSpec((None, tq, gw), lambda bi, g, ti: (bi, ti, qo + g)),
                  pl.BlockSpec((None, s, HEAD_DIM), lambda bi, g, ti: (bi, 0, ko + g)),
                  pl.BlockSpec((None, s, HEAD_DIM), lambda bi, g, ti: (bi, 0, vo + g))],
        out_specs=pl.BlockSpec((None, tq, gw), lambda bi, g, ti: (bi, ti, g)),
        out_shape=jax.ShapeDtypeStruct((b, s, QB_W), BF16),
        scratch_shapes=[pltpu.VMEM((G_B * WINDOW, 2 * WINDOW), F32),
                        pltpu.VMEM((G_B * WINDOW, 1), F32)],
        compiler_params=_params("parallel", "parallel", "arbitrary"),
        name="swa_attention",
    )(tab_b, sinks, proj, proj, proj)


def _diff_kernel(lam_init, tab_ref, q_ref, k_ref, v_ref, lamp_ref, subg_ref, o_ref,
                 d0_ref, d1_ref, m_ref, l_ref, acc_ref):
    h = pl.program_id(1)
    qi = pl.program_id(2)
    t = q_ref.shape[0]
    scale = DIFF_DIM ** -0.5
    tab = lambda k: tab_ref[h, k]

    @pl.when(qi == 0)
    def _():
        _near_bias_tiles(tab, t, d0_ref, d1_ref)

    q = q_ref[...]
    lane = lax.broadcasted_iota(jnp.int32, q.shape, 1)
    zero = jnp.zeros_like(q)
    q2 = jnp.concatenate([jnp.where(lane < DIFF_DIM, q, zero), jnp.where(lane >= DIFF_DIM, q, zero)], axis=0)

    start = pl.multiple_of(qi * t, t)
    row = lax.broadcasted_iota(jnp.int32, (t, t), 0)
    colk = lax.broadcasted_iota(jnp.int32, (t, t), 1)
    causal = jnp.concatenate([row >= colk] * 2, axis=0)
    d0 = jnp.concatenate([d0_ref[...]] * 2, axis=0)
    s = _dot_nt(q2, k_ref[pl.ds(start, t), :]) * scale + d0
    s = jnp.where(causal, s, NEG)
    _softmax_init(s, v_ref[pl.ds(start, t), :], m_ref, l_ref, acc_ref)

    @pl.when(qi > 0)
    def _():
        st = pl.multiple_of((qi - 1) * t, t)
        d1 = jnp.concatenate([d1_ref[...]] * 2, axis=0)
        s1 = _dot_nt(q2, k_ref[pl.ds(st, t), :]) * scale + d1
        _softmax_update(s1, v_ref[pl.ds(st, t), :], m_ref, l_ref, acc_ref)

    far = tab(N_BUCKETS - 1)

    def body(jb, carry):
        st = pl.multiple_of(jb * t, t)
        s2 = _dot_nt(q2, k_ref[pl.ds(st, t), :]) * scale + far
        _softmax_update(s2, v_ref[pl.ds(st, t), :], m_ref, l_ref, acc_ref)
        return carry

    lax.fori_loop(0, jnp.maximum(qi - 1, 0), body, 0)

    lp = lamp_ref[...]
    lam = (jnp.exp(jnp.sum(lp[0:1] * lp[1:2], axis=1, keepdims=True))
           - jnp.exp(jnp.sum(lp[2:3] * lp[3:4], axis=1, keepdims=True)) + lam_init)
    a = acc_ref[...] / l_ref[...]
    o = a[:t] - lam * a[t:]
    o = o * lax.rsqrt(jnp.mean(o * o, axis=-1, keepdims=True) + LN_EPS) * subg_ref[...]
    o_ref[...] = (o * (1.0 - lam_init)).astype(o_ref.dtype)


def _diff(proj, tab_c, lam_p, subln_g, lam_init):
    b, s, _ = proj.shape
    t = 256
    assert s % t == 0 and t >= FAR_DIST
    qo, ko, vo = (PROJ_OFFS[i] // HEAD_DIM for i in (6, 7, 8))
    return pl.pallas_call(
        functools.partial(_diff_kernel, lam_init),
        grid=(b, H_C, s // t),
        in_specs=[pl.BlockSpec(memory_space=pltpu.SMEM),
                  pl.BlockSpec((None, t, HEAD_DIM), lambda bi, h, qi: (bi, qi, qo + h)),
                  pl.BlockSpec((None, s, HEAD_DIM), lambda bi, h, qi: (bi, 0, ko + h)),
                  pl.BlockSpec((None, s, HEAD_DIM), lambda bi, h, qi: (bi, 0, vo + h)),
                  pl.BlockSpec((4, DIFF_DIM), lambda bi, h, qi: (0, 0)),
                  pl.BlockSpec((1, HEAD_DIM), lambda bi, h, qi: (0, 0))],
        out_specs=pl.BlockSpec((None, t, HEAD_DIM), lambda bi, h, qi: (bi, qi, h)),
        out_shape=jax.ShapeDtypeStruct((b, s, VC_W), BF16),
        scratch_shapes=[pltpu.VMEM((t, t), F32), pltpu.VMEM((t, t), F32),
                        pltpu.VMEM((2 * t, 1), F32), pltpu.VMEM((2 * t, 1), F32),
                        pltpu.VMEM((2 * t, HEAD_DIM), F32)],
        compiler_params=_params("parallel", "parallel", "arbitrary"),
        name="diff_attention",
    )(tab_c, proj, proj, proj, lam_p, subln_g)


def kernel(x, c, rel_bias, w_ada, b_ada, w_in, w_o, attn_sinks, diff_lambda, diff_subln_g,
           ln_g, ln_b, w_gate, w_up, w_down):
    b, s, d = x.shape
    m = b * s
    tab = rel_bias.T
    tab_a, tab_b, tab_c = tab[:H_A], tab[H_A:H_A + H_B], tab[H_A + H_B:]

    c8 = jnp.pad(c, ((0, 8 - b), (0, 0)))
    mod = _ada_mod(c8, w_ada, b_ada[:, None, :])[:, :b]
    mod = mod.reshape(DEPTH, b, 6, 1, d)

    fpad = D_FF_PAD - D_FF
    h = None
    for l in range(DEPTH):
        sh1, sc1, g1, sh2, sc2, g2 = (mod[l, :, i] for i in range(6))
        w_in_l = w_in[l].astype(BF16)
        w_o_l = w_o[l].astype(BF16)
        wg_l = jnp.pad(w_gate[l].astype(BF16), ((0, 0), (0, fpad)))
        wu_l = jnp.pad(w_up[l].astype(BF16), ((0, 0), (0, fpad)))
        wd_l = jnp.pad(w_down[l].astype(BF16), ((0, fpad), (0, 0)))

        if h is None:
            h = _modulate(x, sc1, sh1)
        proj = _matmul(h.reshape(m, d), w_in_l, BF16, 1024, 512, "proj_in").reshape(b, s, D_PROJ)
        lam_init = 0.8 - 0.6 * math.exp(-0.3 * l)
        o_a = _moba(proj, tab_a)
        o_b = _swa(proj, tab_b, attn_sinks[l])
        o_c = _diff(proj, tab_c, diff_lambda[l], diff_subln_g[l][None, :], lam_init)
        cat = jnp.concatenate([o_a, o_b, o_c], axis=-1).reshape(m, d)
        mix = _matmul(cat, w_o_l, F32, 1024, 1024, "proj_out").reshape(b, s, d)
        x, h = _residual_norm(x, mix, g1, ln_g[l, 0][None, :], ln_b[l, 0][None, :], sc2, sh2)

        act = _gate_up(h.reshape(m, d), wg_l, wu_l, 1024, 512)
        ff = _matmul_ksplit(act, wd_l, 1024, 1024, D_FF_PAD // 4, "ffn_down").reshape(b, s, d)
        if l + 1 < DEPTH:
            nsh, nsc = mod[l + 1, :, 0], mod[l + 1, :, 1]
            x, h = _residual_norm(x, ff, g2, ln_g[l, 1][None, :], ln_b[l, 1][None, :], nsc, nsh)
        else:
            (x,) = _residual_norm(x, ff, g2, ln_g[l, 1][None, :], ln_b[l, 1][None, :])
    return x
```

```python
import functools
import math

import numpy as np
import jax
import jax.numpy as jnp
from jax import lax
from jax.experimental import pallas as pl
from jax.experimental.pallas import tpu as pltpu

D_MODEL = 4096
DEPTH = 2
HEAD_DIM = 128
N_HEADS = D_MODEL // HEAD_DIM
H_A = N_HEADS // 4
H_C = N_HEADS // 4
H_B = N_HEADS - H_A - H_C
KV_B = max(1, H_B // 8)
G_B = H_B // KV_B
DIFF_DIM = HEAD_DIM // 2
MOBA_BLOCK = 256
MOBA_TOPK = 3
WINDOW = 128
N_BUCKETS = 32
MAX_DISTANCE = 128
D_FF = -(-8 * D_MODEL // (3 * 256)) * 256
ALPHA = (2.0 * DEPTH) ** 0.25
LN_EPS = 1e-5
NEG = -1e30
LOG2E = math.log2(math.e)

QA_W = H_A * HEAD_DIM
QB_W = H_B * HEAD_DIM
KB_W = KV_B * HEAD_DIM
QC_W = H_C * 2 * DIFF_DIM
VC_W = H_C * HEAD_DIM
PROJ_SIZES = (QA_W, QA_W, QA_W, QB_W, KB_W, KB_W, QC_W, QC_W, VC_W)
PROJ_OFFS = tuple(int(sum(PROJ_SIZES[:i])) for i in range(len(PROJ_SIZES)))
D_PROJ = sum(PROJ_SIZES)

LANE = 128
D_FF_PAD = -(-D_FF // 1024) * 1024
VMEM_LIMIT = 56 * 1024 * 1024

F32 = jnp.float32
BF16 = jnp.bfloat16


def _bucket_upper_bounds():
    n = np.arange(0, 4 * MAX_DISTANCE, dtype=np.int64)
    max_exact = N_BUCKETS // 2
    nf = np.maximum(n, 1).astype(np.float32)
    large = max_exact + (np.log(nf / np.float32(max_exact)) / np.float32(math.log(MAX_DISTANCE / max_exact))
                         * np.float32(N_BUCKETS - max_exact)).astype(np.int32)
    large = np.minimum(large, N_BUCKETS - 1)
    bucket = np.where(n < max_exact, n, large)
    assert bucket[-1] == N_BUCKETS - 1 and np.all(np.diff(bucket) >= 0)
    return tuple(int(np.argmax(bucket > k)) for k in range(N_BUCKETS - 1))


BUCKET_UPPER = _bucket_upper_bounds()
FAR_DIST = BUCKET_UPPER[-1]


def _bias_from_dist(dist, tab):
    b = jnp.full(dist.shape, tab(N_BUCKETS - 1), F32)
    for k in range(N_BUCKETS - 2, -1, -1):
        b = jnp.where(dist < BUCKET_UPPER[k], tab(k), b)
    return b


def _dot(a, b):
    return jnp.dot(a, b, preferred_element_type=F32)


def _dot_nt(a, b):
    return lax.dot_general(a, b, (((1,), (1,)), ((), ())), preferred_element_type=F32)


def _params(*sem):
    return pltpu.CompilerParams(dimension_semantics=sem, vmem_limit_bytes=VMEM_LIMIT)


def _mod_kernel(c_ref, w_ref, b_ref, o_ref):
    c = c_ref[...]
    s = (c * jax.nn.sigmoid(c)).astype(BF16)
    o_ref[...] = _dot(s, w_ref[...].astype(BF16)) + b_ref[...]


def _ada_mod(c8, w_ada, b_ada3):
    depth, d, n = w_ada.shape
    tn = 512
    return pl.pallas_call(
        _mod_kernel,
        grid=(depth, n // tn),
        in_specs=[pl.BlockSpec((8, d), lambda l, j: (0, 0)),
                  pl.BlockSpec((None, d, tn), lambda l, j: (l, 0, j)),
                  pl.BlockSpec((None, 1, tn), lambda l, j: (l, 0, j))],
        out_specs=pl.BlockSpec((None, 8, tn), lambda l, j: (l, 0, j)),
        out_shape=jax.ShapeDtypeStruct((depth, 8, n), F32),
        compiler_params=_params("parallel", "parallel"),
        name="ada_mod",
    )(c8, w_ada, b_ada3)


def _modulate_kernel(x_ref, sc_ref, sh_ref, h_ref):
    h_ref[...] = (x_ref[...] * (1.0 + sc_ref[...]) + sh_ref[...]).astype(BF16)


def _modulate(x, sc, sh):
    b, s, d = x.shape
    tm = 512
    vec = pl.BlockSpec((None, 1, d), lambda i, j: (i, 0, 0))
    return pl.pallas_call(
        _modulate_kernel,
        grid=(b, s // tm),
        in_specs=[pl.BlockSpec((None, tm, d), lambda i, j: (i, j, 0)), vec, vec],
        out_specs=pl.BlockSpec((None, tm, d), lambda i, j: (i, j, 0)),
        out_shape=jax.ShapeDtypeStruct((b, s, d), BF16),
        compiler_params=_params("parallel", "parallel"),
        name="modulate",
    )(x, sc, sh)


def _mm_kernel(x_ref, w_ref, o_ref):
    o_ref[...] = _dot(x_ref[...], w_ref[...]).astype(o_ref.dtype)


def _matmul(x, w, out_dtype, tm, tn, name):
    m, k = x.shape
    _, n = w.shape
    return pl.pallas_call(
        _mm_kernel,
        grid=(m // tm, n // tn),
        in_specs=[pl.BlockSpec((tm, k), lambda i, j: (i, 0)),
                  pl.BlockSpec((k, tn), lambda i, j: (0, j))],
        out_specs=pl.BlockSpec((tm, tn), lambda i, j: (i, j)),
        out_shape=jax.ShapeDtypeStruct((m, n), out_dtype),
        compiler_params=_params("parallel", "arbitrary"),
        name=name,
    )(x, w)


def _mm_acc_kernel(x_ref, w_ref, o_ref):
    part = _dot(x_ref[...], w_ref[...])

    @pl.when(pl.program_id(2) == 0)
    def _():
        o_ref[...] = part

    @pl.when(pl.program_id(2) > 0)
    def _():
        o_ref[...] += part


def _matmul_ksplit(x, w, tm, tn, tk, name):
    m, k = x.shape
    _, n = w.shape
    return pl.pallas_call(
        _mm_acc_kernel,
        grid=(m // tm, n // tn, k // tk),
        in_specs=[pl.BlockSpec((tm, tk), lambda i, j, kk: (i, kk)),
                  pl.BlockSpec((tk, tn), lambda i, j, kk: (kk, j))],
        out_specs=pl.BlockSpec((tm, tn), lambda i, j, kk: (i, j)),
        out_shape=jax.ShapeDtypeStruct((m, n), F32),
        compiler_params=_params("parallel", "parallel", "arbitrary"),
        name=name,
    )(x, w)


def _gate_up_kernel(h_ref, wg_ref, wu_ref, o_ref):
    h = h_ref[...]
    g = _dot(h, wg_ref[...])
    u = _dot(h, wu_ref[...])
    o_ref[...] = (g * jax.nn.sigmoid(g) * u).astype(BF16)


def _gate_up(h, wg, wu, tm, tf):
    m, k = h.shape
    _, f = wg.shape
    wspec = pl.BlockSpec((k, tf), lambda i, j: (0, j))
    return pl.pallas_call(
        _gate_up_kernel,
        grid=(m // tm, f // tf),
        in_specs=[pl.BlockSpec((tm, k), lambda i, j: (i, 0)), wspec, wspec],
        out_specs=pl.BlockSpec((tm, tf), lambda i, j: (i, j)),
        out_shape=jax.ShapeDtypeStruct((m, f), BF16),
        compiler_params=_params("parallel", "arbitrary"),
        name="ffn_gate_up",
    )(h, wg, wu)


def _norm_kernel(with_h, x_ref, y_ref, gate_ref, lg_ref, lb_ref, *rest):
    z = ALPHA * x_ref[...] + gate_ref[...] * y_ref[...]
    mu = jnp.mean(z, axis=-1, keepdims=True)
    zc = z - mu
    var = jnp.mean(zc * zc, axis=-1, keepdims=True)
    xn = zc * lax.rsqrt(var + LN_EPS) * lg_ref[...] + lb_ref[...]
    if with_h:
        sc_ref, sh_ref, xo_ref, h_ref = rest
        xo_ref[...] = xn
        h_ref[...] = (xn * (1.0 + sc_ref[...]) + sh_ref[...]).astype(BF16)
    else:
        (xo_ref,) = rest
        xo_ref[...] = xn


def _residual_norm(x, y, gate, lg, lb, sc=None, sh=None):
    b, s, d = x.shape
    tm = 256
    with_h = sc is not None
    tile = pl.BlockSpec((None, tm, d), lambda i, j: (i, j, 0))
    vec = pl.BlockSpec((None, 1, d), lambda i, j: (i, 0, 0))
    par = pl.BlockSpec((1, d), lambda i, j: (0, 0))
    in_specs = [tile, tile, vec, par, par]
    args = [x, y, gate, lg, lb]
    out_specs = [tile]
    out_shape = [jax.ShapeDtypeStruct((b, s, d), F32)]
    if with_h:
        in_specs += [vec, vec]
        args += [sc, sh]
        out_specs.append(tile)
        out_shape.append(jax.ShapeDtypeStruct((b, s, d), BF16))
    return pl.pallas_call(
        functools.partial(_norm_kernel, with_h),
        grid=(b, s // tm),
        in_specs=in_specs,
        out_specs=out_specs,
        out_shape=out_shape,
        compiler_params=_params("parallel", "parallel"),
        name="residual_norm",
    )(*args)


def _lane_tiles(a):
    return [a[:, i * LANE:(i + 1) * LANE] for i in range(a.shape[1] // LANE)]


def _row_max_rep(s):
    tiles = _lane_tiles(s)
    t = tiles[0]
    for u in tiles[1:]:
        t = jnp.maximum(t, u)
    return jnp.broadcast_to(jnp.max(t, axis=1, keepdims=True), t.shape)


def _probs(s, m_rep):
    return jnp.concatenate([jnp.exp2(u - m_rep) for u in _lane_tiles(s)], axis=1).astype(BF16)


def _softmax_reset(m_ref, acc_ref):
    m_ref[...] = jnp.full(m_ref.shape, NEG, F32)
    acc_ref[...] = jnp.zeros(acc_ref.shape, F32)


def _softmax_update(s, v_aug, m_ref, acc_ref):
    m_old = m_ref[...]
    m = jnp.maximum(m_old, _row_max_rep(s))
    corr = jnp.exp2(m_old - m)
    m_ref[...] = m
    acc_ref[...] = jnp.concatenate([corr, corr], axis=1) * acc_ref[...] + _dot(_probs(s, m), v_aug)


def _augment_values(v_ref, vaug_ref):
    vaug_ref[:, :HEAD_DIM] = v_ref[...]
    vaug_ref[:, HEAD_DIM:] = jnp.ones((v_ref.shape[0], HEAD_DIM), BF16)


def _near_bias_tiles(tab, t, d0_ref, d1_ref):
    i = lax.broadcasted_iota(jnp.int32, (t, t), 0)
    j = lax.broadcasted_iota(jnp.int32, (t, t), 1)
    far = tab(N_BUCKETS - 1)
    d0 = (_bias_from_dist(i - j, tab) - far) * LOG2E
    d0_ref[...] = jnp.where(i >= j, d0, NEG)
    d1_ref[...] = (_bias_from_dist(i - j + t, tab) - far) * LOG2E


def _moba_kernel(tab_ref, q_ref, k_ref, v_ref, o_ref, kmean_ref, d0_ref, d1_ref, vaug_ref, m_ref, acc_ref):
    h = pl.program_id(1)
    qi = pl.program_id(2)
    t = MOBA_BLOCK
    nb = k_ref.shape[0] // t
    cs = HEAD_DIM ** -0.5 * LOG2E
    tab = lambda k: tab_ref[h, k]

    @pl.when(qi == 0)
    def _():
        for jb in range(nb):
            kb = k_ref[jb * t:(jb + 1) * t, :].astype(F32)
            kmean_ref[jb:jb + 1, :] = jnp.sum(kb, axis=0, keepdims=True) * (1.0 / t)
        _near_bias_tiles(tab, t, d0_ref, d1_ref)
        _augment_values(v_ref, vaug_ref)

    q = q_ref[...]
    km = kmean_ref[...]
    km_hi = km.astype(BF16)
    km_lo = (km - km_hi.astype(F32)).astype(BF16)
    gate = _dot_nt(q, km_hi) + _dot_nt(q, km_lo)
    col = lax.broadcasted_iota(jnp.int32, gate.shape, 1)
    g = jnp.where(col < qi, gate, -jnp.inf)
    bits = jnp.zeros((t, 1), jnp.int32)
    for r in range(MOBA_TOPK):
        mx = jnp.max(g, axis=1, keepdims=True)
        first = jnp.min(jnp.where(g == mx, col, nb), axis=1, keepdims=True)
        bits = bits | jnp.where(r < qi, jnp.left_shift(1, first), 0)
        g = jnp.where(col == first, -jnp.inf, g)
    bits = jnp.broadcast_to(bits, (t, LANE))

    def masked(s, jb):
        sel = (jnp.right_shift(bits, jb) & 1) == 1
        return jnp.concatenate([jnp.where(sel, u, NEG) for u in _lane_tiles(s)], axis=1)

    _softmax_reset(m_ref, acc_ref)
    start = pl.multiple_of(qi * t, t)
    s = _dot_nt(q, k_ref[pl.ds(start, t), :]) * cs + d0_ref[...]
    _softmax_update(s, vaug_ref[pl.ds(start, t), :], m_ref, acc_ref)

    @pl.when(qi > 0)
    def _():
        st = pl.multiple_of((qi - 1) * t, t)
        s1 = _dot_nt(q, k_ref[pl.ds(st, t), :]) * cs + d1_ref[...]
        _softmax_update(masked(s1, qi - 1), vaug_ref[pl.ds(st, t), :], m_ref, acc_ref)

    def body(jb, carry):
        st = pl.multiple_of(jb * t, t)
        s2 = _dot_nt(q, k_ref[pl.ds(st, t), :]) * cs
        _softmax_update(masked(s2, jb), vaug_ref[pl.ds(st, t), :], m_ref, acc_ref)
        return carry

    lax.fori_loop(0, jnp.maximum(qi - 1, 0), body, 0)
    acc = acc_ref[...]
    o_ref[...] = (acc[:, :HEAD_DIM] / acc[:, HEAD_DIM:]).astype(o_ref.dtype)


def _moba(proj, tab_a):
    b, s, _ = proj.shape
    t = MOBA_BLOCK
    assert s % t == 0 and t >= FAR_DIST
    qo, ko, vo = (PROJ_OFFS[i] // HEAD_DIM for i in (0, 1, 2))
    return pl.pallas_call(
        _moba_kernel,
        grid=(b, H_A, s // t),
        in_specs=[pl.BlockSpec(memory_space=pltpu.SMEM),
                  pl.BlockSpec((None, t, HEAD_DIM), lambda bi, h, qi: (bi, qi, qo + h)),
                  pl.BlockSpec((None, s, HEAD_DIM), lambda bi, h, qi: (bi, 0, ko + h)),
                  pl.BlockSpec((None, s, HEAD_DIM), lambda bi, h, qi: (bi, 0, vo + h))],
        out_specs=pl.BlockSpec((None, t, HEAD_DIM), lambda bi, h, qi: (bi, qi, h)),
        out_shape=jax.ShapeDtypeStruct((b, s, QA_W), BF16),
        scratch_shapes=[pltpu.VMEM((s // t, HEAD_DIM), F32),
                        pltpu.VMEM((t, t), F32), pltpu.VMEM((t, t), F32),
                        pltpu.VMEM((s, 2 * HEAD_DIM), BF16),
                        pltpu.VMEM((t, LANE), F32),
                        pltpu.VMEM((t, 2 * HEAD_DIM), F32)],
        compiler_params=_params("parallel", "parallel", "arbitrary"),
        name="moba_attention",
    )(tab_a, proj, proj, proj)


def _swa_kernel(tab_ref, sink_ref, q_ref, k_ref, v_ref, o_ref, bias_ref, sink_rep_ref, vaug_ref):
    g = pl.program_id(1)
    ti = pl.program_id(2)
    w = WINDOW
    nsub = q_ref.shape[0] // w
    cs = HEAD_DIM ** -0.5 * LOG2E

    @pl.when(ti == 0)
    def _():
        i = lax.broadcasted_iota(jnp.int32, (w, 2 * w), 0)
        j = lax.broadcasted_iota(jnp.int32, (w, 2 * w), 1)
        dist = i + w - j
        band = (dist >= 0) & (dist < w)
        for hh in range(G_B):
            head = g * G_B + hh
            bias = _bias_from_dist(dist, lambda k: tab_ref[head, k]) * LOG2E
            bias_ref[hh * w:(hh + 1) * w, :] = jnp.where(band, bias, NEG)
            sink_rep_ref[hh * w:(hh + 1) * w, :] = jnp.full((w, LANE), sink_ref[head] * LOG2E, F32)
        _augment_values(v_ref, vaug_ref)

    def step(n, first_block):
        blk = ti * nsub + n
        r0 = pl.multiple_of(n * w, w)
        k0 = pl.multiple_of(jnp.maximum(blk - 1, 0) * w, w)
        k1 = pl.multiple_of(blk * w, w)
        qs = jnp.concatenate([q_ref[pl.ds(r0, w), hh * HEAD_DIM:(hh + 1) * HEAD_DIM] for hh in range(G_B)], axis=0)
        kband = jnp.concatenate([k_ref[pl.ds(k0, w), :], k_ref[pl.ds(k1, w), :]], axis=0)
        vband = jnp.concatenate([vaug_ref[pl.ds(k0, w), :], vaug_ref[pl.ds(k1, w), :]], axis=0)
        bias = bias_ref[...]
        if first_block:
            jj = lax.broadcasted_iota(jnp.int32, bias.shape, 1)
            bias = jnp.where(jj >= w, bias, NEG)
        s = _dot_nt(qs, kband) * cs + bias
        sink = sink_rep_ref[...]
        m = jnp.maximum(_row_max_rep(s), sink)
        acc = _dot(_probs(s, m), vband)
        o = acc[:, :HEAD_DIM] / (acc[:, HEAD_DIM:] + jnp.exp2(sink - m))
        for hh in range(G_B):
            o_ref[pl.ds(r0, w), hh * HEAD_DIM:(hh + 1) * HEAD_DIM] = o[hh * w:(hh + 1) * w, :].astype(o_ref.dtype)

    @pl.when(ti == 0)
    def _():
        step(0, True)

    def body(n, carry):
        step(n, False)
        return carry

    lax.fori_loop(jnp.where(ti == 0, 1, 0), nsub, body, 0)


def _swa(proj, tab_b, sinks):
    b, s, _ = proj.shape
    tq = 1024
    gw = G_B * HEAD_DIM
    qo = PROJ_OFFS[3] // gw
    ko, vo = (PROJ_OFFS[i] // HEAD_DIM for i in (4, 5))
    assert PROJ_OFFS[3] % gw == 0 and s % tq == 0
    smem = pl.BlockSpec(memory_space=pltpu.SMEM)
    return pl.pallas_call(
        _swa_kernel,
        grid=(b, KV_B, s // tq),
        in_specs=[smem, smem,
                  pl.BlockSpec((None, tq, gw), lambda bi, g, ti: (bi, ti, qo + g)),
                  pl.BlockSpec((None, s, HEAD_DIM), lambda bi, g, ti: (bi, 0, ko + g)),
                  pl.BlockSpec((None, s, HEAD_DIM), lambda bi, g, ti: (bi, 0, vo + g))],
        out_specs=pl.BlockSpec((None, tq, gw), lambda bi, g, ti: (bi, ti, g)),
        out_shape=jax.ShapeDtypeStruct((b, s, QB_W), BF16),
        scratch_shapes=[pltpu.VMEM((G_B * WINDOW, 2 * WINDOW), F32),
                        pltpu.VMEM((G_B * WINDOW, LANE), F32),
                        pltpu.VMEM((s, 2 * HEAD_DIM), BF16)],
        compiler_params=_params("parallel", "parallel", "arbitrary"),
        name="swa_attention",
    )(tab_b, sinks, proj, proj, proj)


def _diff_kernel(lam_init, tab_ref, q_ref, k_ref, v_ref, lamp_ref, subg_ref, o_ref,
                 d0_ref, d1_ref, vaug_ref, m_ref, acc_ref):
    h = pl.program_id(1)
    qi = pl.program_id(2)
    t = q_ref.shape[0]
    cs = DIFF_DIM ** -0.5 * LOG2E
    tab = lambda k: tab_ref[h, k]

    @pl.when(qi == 0)
    def _():
        _near_bias_tiles(tab, t, d0_ref, d1_ref)
        _augment_values(v_ref, vaug_ref)

    q = q_ref[...]
    lane = lax.broadcasted_iota(jnp.int32, q.shape, 1)
    zero = jnp.zeros_like(q)
    q2 = jnp.concatenate([jnp.where(lane < DIFF_DIM, q, zero), jnp.where(lane >= DIFF_DIM, q, zero)], axis=0)

    def logits(st, d_ref):
        s = _dot_nt(q2, k_ref[pl.ds(st, t), :]) * cs
        if d_ref is None:
            return s
        d = d_ref[...]
        return jnp.concatenate([s[:t] + d, s[t:] + d], axis=0)

    _softmax_reset(m_ref, acc_ref)
    start = pl.multiple_of(qi * t, t)
    _softmax_update(logits(start, d0_ref), vaug_ref[pl.ds(start, t), :], m_ref, acc_ref)

    @pl.when(qi > 0)
    def _():
        st = pl.multiple_of((qi - 1) * t, t)
        _softmax_update(logits(st, d1_ref), vaug_ref[pl.ds(st, t), :], m_ref, acc_ref)

    def body(jb, carry):
        st = pl.multiple_of(jb * t, t)
        _softmax_update(logits(st, None), vaug_ref[pl.ds(st, t), :], m_ref, acc_ref)
        return carry

    lax.fori_loop(0, jnp.maximum(qi - 1, 0), body, 0)

    lp = lamp_ref[...]
    lam = (jnp.exp(jnp.sum(lp[0:1] * lp[1:2], axis=1, keepdims=True))
           - jnp.exp(jnp.sum(lp[2:3] * lp[3:4], axis=1, keepdims=True)) + lam_init)
    acc = acc_ref[...]
    a = acc[:, :HEAD_DIM] / acc[:, HEAD_DIM:]
    o = a[:t] - lam * a[t:]
    o = o * lax.rsqrt(jnp.mean(o * o, axis=-1, keepdims=True) + LN_EPS) * subg_ref[...]
    o_ref[...] = (o * (1.0 - lam_init)).astype(o_ref.dtype)


def _diff(proj, tab_c, lam_p, subln_g, lam_init):
    b, s, _ = proj.shape
    t = 512
    assert s % t == 0 and t >= FAR_DIST
    qo, ko, vo = (PROJ_OFFS[i] // HEAD_DIM for i in (6, 7, 8))
    return pl.pallas_call(
        functools.partial(_diff_kernel, lam_init),
        grid=(b, H_C, s // t),
        in_specs=[pl.BlockSpec(memory_space=pltpu.SMEM),
                  pl.BlockSpec((None, t, HEAD_DIM), lambda bi, h, qi: (bi, qi, qo + h)),
                  pl.BlockSpec((None, s, HEAD_DIM), lambda bi, h, qi: (bi, 0, ko + h)),
                  pl.BlockSpec((None, s, HEAD_DIM), lambda bi, h, qi: (bi, 0, vo + h)),
                  pl.BlockSpec((4, DIFF_DIM), lambda bi, h, qi: (0, 0)),
                  pl.BlockSpec((1, HEAD_DIM), lambda bi, h, qi: (0, 0))],
        out_specs=pl.BlockSpec((None, t, HEAD_DIM), lambda bi, h, qi: (bi, qi, h)),
        out_shape=jax.ShapeDtypeStruct((b, s, VC_W), BF16),
        scratch_shapes=[pltpu.VMEM((t, t), F32), pltpu.VMEM((t, t), F32),
                        pltpu.VMEM((s, 2 * HEAD_DIM), BF16),
                        pltpu.VMEM((2 * t, LANE), F32),
                        pltpu.VMEM((2 * t, 2 * HEAD_DIM), F32)],
        compiler_params=_params("parallel", "parallel", "arbitrary"),
        name="diff_attention",
    )(tab_c, proj, proj, proj, lam_p, subln_g)


def kernel(x, c, rel_bias, w_ada, b_ada, w_in, w_o, attn_sinks, diff_lambda, diff_subln_g,
           ln_g, ln_b, w_gate, w_up, w_down):
    b, s, d = x.shape
    m = b * s
    tab = rel_bias.T
    tab_a, tab_b, tab_c = tab[:H_A], tab[H_A:H_A + H_B], tab[H_A + H_B:]

    c8 = jnp.pad(c, ((0, 8 - b), (0, 0)))
    mod = _ada_mod(c8, w_ada, b_ada[:, None, :])[:, :b]
    mod = mod.reshape(DEPTH, b, 6, 1, d)

    fpad = D_FF_PAD - D_FF
    h = None
    for l in range(DEPTH):
        sh1, sc1, g1, sh2, sc2, g2 = (mod[l, :, i] for i in range(6))
        w_in_l = w_in[l].astype(BF16)
        w_o_l = w_o[l].astype(BF16)
        wg_l = jnp.pad(w_gate[l].astype(BF16), ((0, 0), (0, fpad)))
        wu_l = jnp.pad(w_up[l].astype(BF16), ((0, 0), (0, fpad)))
        wd_l = jnp.pad(w_down[l].astype(BF16), ((0, fpad), (0, 0)))

        if h is None:
            h = _modulate(x, sc1, sh1)
        proj = _matmul(h.reshape(m, d), w_in_l, BF16, 1024, 512, "proj_in").reshape(b, s, D_PROJ)
        lam_init = 0.8 - 0.6 * math.exp(-0.3 * l)
        o_a = _moba(proj, tab_a)
        o_b = _swa(proj, tab_b, attn_sinks[l])
        o_c = _diff(proj, tab_c, diff_lambda[l], diff_subln_g[l][None, :], lam_init)
        cat = jnp.concatenate([o_a, o_b, o_c], axis=-1).reshape(m, d)
        mix = _matmul(cat, w_o_l, F32, 1024, 1024, "proj_out").reshape(b, s, d)
        x, h = _residual_norm(x, mix, g1, ln_g[l, 0][None, :], ln_b[l, 0][None, :], sc2, sh2)

        act = _gate_up(h.reshape(m, d), wg_l, wu_l, 1024, 512)
        ff = _matmul_ksplit(act, wd_l, 1024, 1024, D_FF_PAD // 4, "ffn_down").reshape(b, s, d)
        if l + 1 < DEPTH:
            nsh, nsc = mod[l + 1, :, 0], mod[l + 1, :, 1]
            x, h = _residual_norm(x, ff, g2, ln_g[l, 1][None, :], ln_b[l, 1][None, :], nsc, nsh)
        else:
            (x,) = _residual_norm(x, ff, g2, ln_g[l, 1][None, :], ln_b[l, 1][None, :])
    return x
```

```python
import functools
import math

import numpy as np
import jax
import jax.numpy as jnp
from jax import lax
from jax.experimental import pallas as pl
from jax.experimental.pallas import tpu as pltpu

D_MODEL = 4096
DEPTH = 2
HEAD_DIM = 128
N_HEADS = D_MODEL // HEAD_DIM
H_A = N_HEADS // 4
H_C = N_HEADS // 4
H_B = N_HEADS - H_A - H_C
KV_B = max(1, H_B // 8)
G_B = H_B // KV_B
DIFF_DIM = HEAD_DIM // 2
MOBA_BLOCK = 256
MOBA_TOPK = 3
WINDOW = 128
N_BUCKETS = 32
MAX_DISTANCE = 128
D_FF = -(-8 * D_MODEL // (3 * 256)) * 256
ALPHA = (2.0 * DEPTH) ** 0.25
LN_EPS = 1e-5
NEG = -1e30
LOG2E = math.log2(math.e)

QA_W = H_A * HEAD_DIM
QB_W = H_B * HEAD_DIM
KB_W = KV_B * HEAD_DIM
QC_W = H_C * 2 * DIFF_DIM
VC_W = H_C * HEAD_DIM
PROJ_SIZES = (QA_W, QA_W, QA_W, QB_W, KB_W, KB_W, QC_W, QC_W, VC_W)
PROJ_OFFS = tuple(int(sum(PROJ_SIZES[:i])) for i in range(len(PROJ_SIZES)))
D_PROJ = sum(PROJ_SIZES)

LANE = 128
ATTN_TILE = 512
D_FF_PAD = -(-D_FF // 1024) * 1024
VMEM_LIMIT = 56 * 1024 * 1024

F32 = jnp.float32
BF16 = jnp.bfloat16


def _bucket_upper_bounds():
    n = np.arange(0, 4 * MAX_DISTANCE, dtype=np.int64)
    max_exact = N_BUCKETS // 2
    nf = np.maximum(n, 1).astype(np.float32)
    large = max_exact + (np.log(nf / np.float32(max_exact)) / np.float32(math.log(MAX_DISTANCE / max_exact))
                         * np.float32(N_BUCKETS - max_exact)).astype(np.int32)
    large = np.minimum(large, N_BUCKETS - 1)
    bucket = np.where(n < max_exact, n, large)
    assert bucket[-1] == N_BUCKETS - 1 and np.all(np.diff(bucket) >= 0)
    return tuple(int(np.argmax(bucket > k)) for k in range(N_BUCKETS - 1))


BUCKET_UPPER = _bucket_upper_bounds()
FAR_DIST = BUCKET_UPPER[-1]


def _bias_from_dist(dist, tab):
    b = jnp.full(dist.shape, tab(N_BUCKETS - 1), F32)
    for k in range(N_BUCKETS - 2, -1, -1):
        b = jnp.where(dist < BUCKET_UPPER[k], tab(k), b)
    return b


def _dot(a, b):
    return jnp.dot(a, b, preferred_element_type=F32)


def _dot_nt(a, b):
    return lax.dot_general(a, b, (((1,), (1,)), ((), ())), preferred_element_type=F32)


def _params(*sem):
    return pltpu.CompilerParams(dimension_semantics=sem, vmem_limit_bytes=VMEM_LIMIT)


def _mod_kernel(c_ref, w_ref, b_ref, o_ref):
    c = c_ref[...]
    s = (c * jax.nn.sigmoid(c)).astype(BF16)
    o_ref[...] = _dot(s, w_ref[...].astype(BF16)) + b_ref[...]


def _ada_mod(c8, w_ada, b_ada3):
    depth, d, n = w_ada.shape
    tn = 512
    return pl.pallas_call(
        _mod_kernel,
        grid=(depth, n // tn),
        in_specs=[pl.BlockSpec((8, d), lambda l, j: (0, 0)),
                  pl.BlockSpec((None, d, tn), lambda l, j: (l, 0, j)),
                  pl.BlockSpec((None, 1, tn), lambda l, j: (l, 0, j))],
        out_specs=pl.BlockSpec((None, 8, tn), lambda l, j: (l, 0, j)),
        out_shape=jax.ShapeDtypeStruct((depth, 8, n), F32),
        compiler_params=_params("parallel", "parallel"),
        name="ada_mod",
    )(c8, w_ada, b_ada3)


def _modulate_kernel(x_ref, sc_ref, sh_ref, h_ref):
    h_ref[...] = (x_ref[...] * (1.0 + sc_ref[...]) + sh_ref[...]).astype(BF16)


def _modulate(x, sc, sh):
    b, s, d = x.shape
    tm = 512
    vec = pl.BlockSpec((None, 1, d), lambda i, j: (i, 0, 0))
    return pl.pallas_call(
        _modulate_kernel,
        grid=(b, s // tm),
        in_specs=[pl.BlockSpec((None, tm, d), lambda i, j: (i, j, 0)), vec, vec],
        out_specs=pl.BlockSpec((None, tm, d), lambda i, j: (i, j, 0)),
        out_shape=jax.ShapeDtypeStruct((b, s, d), BF16),
        compiler_params=_params("parallel", "parallel"),
        name="modulate",
    )(x, sc, sh)


def _mm_kernel(x_ref, w_ref, o_ref):
    o_ref[...] = _dot(x_ref[...], w_ref[...]).astype(o_ref.dtype)


def _matmul(x, w, out_dtype, tm, tn, name):
    m, k = x.shape
    _, n = w.shape
    return pl.pallas_call(
        _mm_kernel,
        grid=(m // tm, n // tn),
        in_specs=[pl.BlockSpec((tm, k), lambda i, j: (i, 0)),
                  pl.BlockSpec((k, tn), lambda i, j: (0, j))],
        out_specs=pl.BlockSpec((tm, tn), lambda i, j: (i, j)),
        out_shape=jax.ShapeDtypeStruct((m, n), out_dtype),
        compiler_params=_params("parallel", "arbitrary"),
        name=name,
    )(x, w)


def _mm_acc_kernel(x_ref, w_ref, o_ref):
    part = _dot(x_ref[...], w_ref[...])

    @pl.when(pl.program_id(2) == 0)
    def _():
        o_ref[...] = part

    @pl.when(pl.program_id(2) > 0)
    def _():
        o_ref[...] += part


def _matmul_ksplit(x, w, tm, tn, tk, name):
    m, k = x.shape
    _, n = w.shape
    return pl.pallas_call(
        _mm_acc_kernel,
        grid=(m // tm, n // tn, k // tk),
        in_specs=[pl.BlockSpec((tm, tk), lambda i, j, kk: (i, kk)),
                  pl.BlockSpec((tk, tn), lambda i, j, kk: (kk, j))],
        out_specs=pl.BlockSpec((tm, tn), lambda i, j, kk: (i, j)),
        out_shape=jax.ShapeDtypeStruct((m, n), F32),
        compiler_params=_params("parallel", "parallel", "arbitrary"),
        name=name,
    )(x, w)


def _gate_up_kernel(h_ref, wg_ref, wu_ref, o_ref):
    h = h_ref[...]
    g = _dot(h, wg_ref[...])
    u = _dot(h, wu_ref[...])
    o_ref[...] = (g * jax.nn.sigmoid(g) * u).astype(BF16)


def _gate_up(h, wg, wu, tm, tf):
    m, k = h.shape
    _, f = wg.shape
    wspec = pl.BlockSpec((k, tf), lambda i, j: (0, j))
    return pl.pallas_call(
        _gate_up_kernel,
        grid=(m // tm, f // tf),
        in_specs=[pl.BlockSpec((tm, k), lambda i, j: (i, 0)), wspec, wspec],
        out_specs=pl.BlockSpec((tm, tf), lambda i, j: (i, j)),
        out_shape=jax.ShapeDtypeStruct((m, f), BF16),
        compiler_params=_params("parallel", "arbitrary"),
        name="ffn_gate_up",
    )(h, wg, wu)


def _norm_kernel(with_h, x_ref, y_ref, gate_ref, lg_ref, lb_ref, *rest):
    z = ALPHA * x_ref[...] + gate_ref[...] * y_ref[...]
    mu = jnp.mean(z, axis=-1, keepdims=True)
    zc = z - mu
    var = jnp.mean(zc * zc, axis=-1, keepdims=True)
    xn = zc * lax.rsqrt(var + LN_EPS) * lg_ref[...] + lb_ref[...]
    if with_h:
        sc_ref, sh_ref, xo_ref, h_ref = rest
        xo_ref[...] = xn
        h_ref[...] = (xn * (1.0 + sc_ref[...]) + sh_ref[...]).astype(BF16)
    else:
        (xo_ref,) = rest
        xo_ref[...] = xn


def _residual_norm(x, y, gate, lg, lb, sc=None, sh=None):
    b, s, d = x.shape
    tm = 256
    with_h = sc is not None
    tile = pl.BlockSpec((None, tm, d), lambda i, j: (i, j, 0))
    vec = pl.BlockSpec((None, 1, d), lambda i, j: (i, 0, 0))
    par = pl.BlockSpec((1, d), lambda i, j: (0, 0))
    in_specs = [tile, tile, vec, par, par]
    args = [x, y, gate, lg, lb]
    out_specs = [tile]
    out_shape = [jax.ShapeDtypeStruct((b, s, d), F32)]
    if with_h:
        in_specs += [vec, vec]
        args += [sc, sh]
        out_specs.append(tile)
        out_shape.append(jax.ShapeDtypeStruct((b, s, d), BF16))
    return pl.pallas_call(
        functools.partial(_norm_kernel, with_h),
        grid=(b, s // tm),
        in_specs=in_specs,
        out_specs=out_specs,
        out_shape=out_shape,
        compiler_params=_params("parallel", "parallel"),
        name="residual_norm",
    )(*args)


def _lane_tiles(a):
    return [a[:, i * LANE:(i + 1) * LANE] for i in range(a.shape[1] // LANE)]


def _row_max_rep(s):
    tiles = _lane_tiles(s)
    t = tiles[0]
    for u in tiles[1:]:
        t = jnp.maximum(t, u)
    return jnp.broadcast_to(jnp.max(t, axis=1, keepdims=True), t.shape)


def _probs(s, m_rep):
    return jnp.concatenate([jnp.exp2(u - m_rep) for u in _lane_tiles(s)], axis=1).astype(BF16)


def _softmax_reset(m_ref, acc_ref):
    m_ref[...] = jnp.full(m_ref.shape, NEG, F32)
    acc_ref[...] = jnp.zeros(acc_ref.shape, F32)


def _softmax_update(s, v_aug, m_ref, acc_ref):
    m_old = m_ref[...]
    m = jnp.maximum(m_old, _row_max_rep(s))
    corr = jnp.exp2(m_old - m)
    m_ref[...] = m
    acc_ref[...] = jnp.concatenate([corr, corr], axis=1) * acc_ref[...] + _dot(_probs(s, m), v_aug)


def _augment_values(v_ref, vaug_ref):
    vaug_ref[:, :HEAD_DIM] = v_ref[...]
    vaug_ref[:, HEAD_DIM:] = jnp.ones((v_ref.shape[0], HEAD_DIM), BF16)


def _bias_tiles(tab, bias_ref):
    t = bias_ref.shape[1]
    i = lax.broadcasted_iota(jnp.int32, (t, t), 0)
    j = lax.broadcasted_iota(jnp.int32, (t, t), 1)
    far = tab(N_BUCKETS - 1)
    d0 = (_bias_from_dist(i - j, tab) - far) * LOG2E
    bias_ref[2] = jnp.where(i >= j, d0, NEG)
    bias_ref[1] = (_bias_from_dist(i - j + t, tab) - far) * LOG2E
    bias_ref[0] = jnp.zeros((t, t), F32)


def _causal_tiles(qi, logits_fn, update_fn, sa_ref, sb_ref):
    n = qi + 1
    sa_ref[...] = logits_fn(qi)

    def pair(p, carry):
        left = 2 * p
        kt = qi - left
        sb_ref[...] = logits_fn(jnp.maximum(kt - 1, 0))
        update_fn(sa_ref[...], left, kt)
        sa_ref[...] = logits_fn(jnp.maximum(kt - 2, 0))
        update_fn(sb_ref[...], left + 1, kt - 1)
        return carry

    lax.fori_loop(0, n // 2, pair, 0)

    @pl.when(n % 2 == 1)
    def _():
        update_fn(sa_ref[...], qi, 0)


def _moba_kernel(tab_ref, q_ref, k_ref, v_ref, o_ref, kmean_ref, bias_ref, vaug_ref, sa_ref, sb_ref,
                 m_ref, acc_ref):
    h = pl.program_id(0)
    bi = pl.program_id(1)
    qi = pl.program_id(2)
    t = q_ref.shape[0]
    blk = MOBA_BLOCK
    bpt = t // blk
    nb = k_ref.shape[0] // blk
    cs = HEAD_DIM ** -0.5 * LOG2E

    @pl.when((bi == 0) & (qi == 0))
    def _():
        _bias_tiles(lambda k: tab_ref[h, k], bias_ref)

    @pl.when(qi == 0)
    def _():
        for jb in range(nb):
            kb = k_ref[jb * blk:(jb + 1) * blk, :].astype(F32)
            kmean_ref[jb:jb + 1, :] = jnp.sum(kb, axis=0, keepdims=True) * (1.0 / blk)
        _augment_values(v_ref, vaug_ref)

    q = q_ref[...]
    km = kmean_ref[...]
    km_hi = km.astype(BF16)
    km_lo = (km - km_hi.astype(F32)).astype(BF16)
    gate = _dot_nt(q, km_hi) + _dot_nt(q, km_lo)
    col = lax.broadcasted_iota(jnp.int32, gate.shape, 1)
    row = lax.broadcasted_iota(jnp.int32, (t, 1), 0)
    qblk = qi * bpt + row // blk
    g = jnp.where(col < qblk, gate, -jnp.inf)
    bits = jnp.left_shift(1, qblk)
    for r in range(MOBA_TOPK):
        mx = jnp.max(g, axis=1, keepdims=True)
        first = jnp.min(jnp.where(g == mx, col, nb), axis=1, keepdims=True)
        bits = bits | jnp.where(r < qblk, jnp.left_shift(1, first), 0)
        g = jnp.where(col == first, -jnp.inf, g)
    bits = jnp.broadcast_to(bits, (t, LANE))

    def logits(kt):
        return _dot_nt(q, k_ref[pl.ds(pl.multiple_of(kt * t, t), t), :]) * cs

    def update(s, left, kt):
        s = s + bias_ref[jnp.maximum(2 - left, 0)]
        tiles = _lane_tiles(s)
        per_blk = blk // LANE
        out = []
        for c in range(bpt):
            vis = (jnp.right_shift(bits, kt * bpt + c) & 1) == 1
            out += [jnp.where(vis, u, NEG) for u in tiles[c * per_blk:(c + 1) * per_blk]]
        s = jnp.concatenate(out, axis=1)
        _softmax_update(s, vaug_ref[pl.ds(pl.multiple_of(kt * t, t), t), :], m_ref, acc_ref)

    _softmax_reset(m_ref, acc_ref)
    _causal_tiles(qi, logits, update, sa_ref, sb_ref)
    acc = acc_ref[...]
    o_ref[...] = (acc[:, :HEAD_DIM] / acc[:, HEAD_DIM:]).astype(o_ref.dtype)


def _moba(proj, tab_a):
    b, s, _ = proj.shape
    t = ATTN_TILE
    assert s % t == 0 and t % MOBA_BLOCK == 0 and t >= FAR_DIST and s // MOBA_BLOCK < 31
    qo, ko, vo = (PROJ_OFFS[i] // HEAD_DIM for i in (0, 1, 2))
    return pl.pallas_call(
        _moba_kernel,
        grid=(H_A, b, s // t),
        in_specs=[pl.BlockSpec(memory_space=pltpu.SMEM),
                  pl.BlockSpec((None, t, HEAD_DIM), lambda h, bi, qi: (bi, qi, qo + h)),
                  pl.BlockSpec((None, s, HEAD_DIM), lambda h, bi, qi: (bi, 0, ko + h)),
                  pl.BlockSpec((None, s, HEAD_DIM), lambda h, bi, qi: (bi, 0, vo + h))],
        out_specs=pl.BlockSpec((None, t, HEAD_DIM), lambda h, bi, qi: (bi, qi, h)),
        out_shape=jax.ShapeDtypeStruct((b, s, QA_W), BF16),
        scratch_shapes=[pltpu.VMEM((s // MOBA_BLOCK, HEAD_DIM), F32),
                        pltpu.VMEM((3, t, t), F32),
                        pltpu.VMEM((s, 2 * HEAD_DIM), BF16),
                        pltpu.VMEM((t, t), F32), pltpu.VMEM((t, t), F32),
                        pltpu.VMEM((t, LANE), F32),
                        pltpu.VMEM((t, 2 * HEAD_DIM), F32)],
        compiler_params=_params("arbitrary", "arbitrary", "arbitrary"),
        name="moba_attention",
    )(tab_a, proj, proj, proj)


def _swa_kernel(tab_ref, sink_ref, q_ref, k_ref, v_ref, o_ref, bias_ref, sink_rep_ref, vaug_ref):
    g = pl.program_id(1)
    ti = pl.program_id(2)
    w = WINDOW
    nsub = q_ref.shape[0] // w
    cs = HEAD_DIM ** -0.5 * LOG2E

    @pl.when(ti == 0)
    def _():
        i = lax.broadcasted_iota(jnp.int32, (w, 2 * w), 0)
        j = lax.broadcasted_iota(jnp.int32, (w, 2 * w), 1)
        dist = i + w - j
        band = (dist >= 0) & (dist < w)
        for hh in range(G_B):
            head = g * G_B + hh
            bias = _bias_from_dist(dist, lambda k: tab_ref[head, k]) * LOG2E
            bias_ref[hh * w:(hh + 1) * w, :] = jnp.where(band, bias, NEG)
            sink_rep_ref[hh * w:(hh + 1) * w, :] = jnp.full((w, LANE), sink_ref[head] * LOG2E, F32)
        _augment_values(v_ref, vaug_ref)

    def step(n, first_block):
        blk = ti * nsub + n
        r0 = pl.multiple_of(n * w, w)
        k0 = pl.multiple_of(jnp.maximum(blk - 1, 0) * w, w)
        k1 = pl.multiple_of(blk * w, w)
        qs = jnp.concatenate([q_ref[pl.ds(r0, w), hh * HEAD_DIM:(hh + 1) * HEAD_DIM] for hh in range(G_B)], axis=0)
        kband = jnp.concatenate([k_ref[pl.ds(k0, w), :], k_ref[pl.ds(k1, w), :]], axis=0)
        vband = jnp.concatenate([vaug_ref[pl.ds(k0, w), :], vaug_ref[pl.ds(k1, w), :]], axis=0)
        bias = bias_ref[...]
        if first_block:
            jj = lax.broadcasted_iota(jnp.int32, bias.shape, 1)
            bias = jnp.where(jj >= w, bias, NEG)
        s = _dot_nt(qs, kband) * cs + bias
        sink = sink_rep_ref[...]
        m = jnp.maximum(_row_max_rep(s), sink)
        acc = _dot(_probs(s, m), vband)
        o = acc[:, :HEAD_DIM] / (acc[:, HEAD_DIM:] + jnp.exp2(sink - m))
        for hh in range(G_B):
            o_ref[pl.ds(r0, w), hh * HEAD_DIM:(hh + 1) * HEAD_DIM] = o[hh * w:(hh + 1) * w, :].astype(o_ref.dtype)

    @pl.when(ti == 0)
    def _():
        step(0, True)

    def body(n, carry):
        step(n, False)
        return carry

    lax.fori_loop(jnp.where(ti == 0, 1, 0), nsub, body, 0)


def _swa(proj, tab_b, sinks):
    b, s, _ = proj.shape
    tq = 1024
    gw = G_B * HEAD_DIM
    qo = PROJ_OFFS[3] // gw
    ko, vo = (PROJ_OFFS[i] // HEAD_DIM for i in (4, 5))
    assert PROJ_OFFS[3] % gw == 0 and s % tq == 0
    smem = pl.BlockSpec(memory_space=pltpu.SMEM)
    return pl.pallas_call(
        _swa_kernel,
        grid=(b, KV_B, s // tq),
        in_specs=[smem, smem,
                  pl.BlockSpec((None, tq, gw), lambda bi, g, ti: (bi, ti, qo + g)),
                  pl.BlockSpec((None, s, HEAD_DIM), lambda bi, g, ti: (bi, 0, ko + g)),
                  pl.BlockSpec((None, s, HEAD_DIM), lambda bi, g, ti: (bi, 0, vo + g))],
        out_specs=pl.BlockSpec((None, tq, gw), lambda bi, g, ti: (bi, ti, g)),
        out_shape=jax.ShapeDtypeStruct((b, s, QB_W), BF16),
        scratch_shapes=[pltpu.VMEM((G_B * WINDOW, 2 * WINDOW), F32),
                        pltpu.VMEM((G_B * WINDOW, LANE), F32),
                        pltpu.VMEM((s, 2 * HEAD_DIM), BF16)],
        compiler_params=_params("parallel", "parallel", "arbitrary"),
        name="swa_attention",
    )(tab_b, sinks, proj, proj, proj)


def _diff_kernel(lam_init, tab_ref, q_ref, k_ref, v_ref, lamp_ref, subg_ref, o_ref,
                 bias_ref, vaug_ref, sa_ref, sb_ref, m_ref, acc_ref):
    h = pl.program_id(0)
    bi = pl.program_id(1)
    qi = pl.program_id(2)
    t = q_ref.shape[0]
    cs = DIFF_DIM ** -0.5 * LOG2E

    @pl.when((bi == 0) & (qi == 0))
    def _():
        _bias_tiles(lambda k: tab_ref[h, k], bias_ref)

    @pl.when(qi == 0)
    def _():
        _augment_values(v_ref, vaug_ref)

    q = q_ref[...]
    lane = lax.broadcasted_iota(jnp.int32, q.shape, 1)
    zero = jnp.zeros_like(q)
    q2 = jnp.concatenate([jnp.where(lane < DIFF_DIM, q, zero), jnp.where(lane >= DIFF_DIM, q, zero)], axis=0)

    def logits(kt):
        return _dot_nt(q2, k_ref[pl.ds(pl.multiple_of(kt * t, t), t), :]) * cs

    def update(s, left, kt):
        d = bias_ref[jnp.maximum(2 - left, 0)]
        s = jnp.concatenate([s[:t] + d, s[t:] + d], axis=0)
        _softmax_update(s, vaug_ref[pl.ds(pl.multiple_of(kt * t, t), t), :], m_ref, acc_ref)

    _softmax_reset(m_ref, acc_ref)
    _causal_tiles(qi, logits, update, sa_ref, sb_ref)

    lp = lamp_ref[...]
    lam = (jnp.exp(jnp.sum(lp[0:1] * lp[1:2], axis=1, keepdims=True))
           - jnp.exp(jnp.sum(lp[2:3] * lp[3:4], axis=1, keepdims=True)) + lam_init)
    acc = acc_ref[...]
    a = acc[:, :HEAD_DIM] / acc[:, HEAD_DIM:]
    o = a[:t] - lam * a[t:]
    o = o * lax.rsqrt(jnp.mean(o * o, axis=-1, keepdims=True) + LN_EPS) * subg_ref[...]
    o_ref[...] = (o * (1.0 - lam_init)).astype(o_ref.dtype)


def _diff(proj, tab_c, lam_p, subln_g, lam_init):
    b, s, _ = proj.shape
    t = ATTN_TILE
    assert s % t == 0 and t >= FAR_DIST
    qo, ko, vo = (PROJ_OFFS[i] // HEAD_DIM for i in (6, 7, 8))
    return pl.pallas_call(
        functools.partial(_diff_kernel, lam_init),
        grid=(H_C, b, s // t),
        in_specs=[pl.BlockSpec(memory_space=pltpu.SMEM),
                  pl.BlockSpec((None, t, HEAD_DIM), lambda h, bi, qi: (bi, qi, qo + h)),
                  pl.BlockSpec((None, s, HEAD_DIM), lambda h, bi, qi: (bi, 0, ko + h)),
                  pl.BlockSpec((None, s, HEAD_DIM), lambda h, bi, qi: (bi, 0, vo + h)),
                  pl.BlockSpec((4, DIFF_DIM), lambda h, bi, qi: (0, 0)),
                  pl.BlockSpec((1, HEAD_DIM), lambda h, bi, qi: (0, 0))],
        out_specs=pl.BlockSpec((None, t, HEAD_DIM), lambda h, bi, qi: (bi, qi, h)),
        out_shape=jax.ShapeDtypeStruct((b, s, VC_W), BF16),
        scratch_shapes=[pltpu.VMEM((3, t, t), F32),
                        pltpu.VMEM((s, 2 * HEAD_DIM), BF16),
                        pltpu.VMEM((2 * t, t), F32), pltpu.VMEM((2 * t, t), F32),
                        pltpu.VMEM((2 * t, LANE), F32),
                        pltpu.VMEM((2 * t, 2 * HEAD_DIM), F32)],
        compiler_params=_params("arbitrary", "arbitrary", "arbitrary"),
        name="diff_attention",
    )(tab_c, proj, proj, proj, lam_p, subln_g)


def kernel(x, c, rel_bias, w_ada, b_ada, w_in, w_o, attn_sinks, diff_lambda, diff_subln_g,
           ln_g, ln_b, w_gate, w_up, w_down):
    b, s, d = x.shape
    m = b * s
    tab = rel_bias.T
    tab_a, tab_b, tab_c = tab[:H_A], tab[H_A:H_A + H_B], tab[H_A + H_B:]

    c8 = jnp.pad(c, ((0, 8 - b), (0, 0)))
    mod = _ada_mod(c8, w_ada, b_ada[:, None, :])[:, :b]
    mod = mod.reshape(DEPTH, b, 6, 1, d)

    fpad = D_FF_PAD - D_FF
    h = None
    for l in range(DEPTH):
        sh1, sc1, g1, sh2, sc2, g2 = (mod[l, :, i] for i in range(6))
        w_in_l = w_in[l].astype(BF16)
        w_o_l = w_o[l].astype(BF16)
        wg_l = jnp.pad(w_gate[l].astype(BF16), ((0, 0), (0, fpad)))
        wu_l = jnp.pad(w_up[l].astype(BF16), ((0, 0), (0, fpad)))
        wd_l = jnp.pad(w_down[l].astype(BF16), ((0, fpad), (0, 0)))

        if h is None:
            h = _modulate(x, sc1, sh1)
        proj = _matmul(h.reshape(m, d), w_in_l, BF16, 1024, 512, "proj_in").reshape(b, s, D_PROJ)
        lam_init = 0.8 - 0.6 * math.exp(-0.3 * l)
        o_a = _moba(proj, tab_a)
        o_b = _swa(proj, tab_b, attn_sinks[l])
        o_c = _diff(proj, tab_c, diff_lambda[l], diff_subln_g[l][None, :], lam_init)
        cat = jnp.concatenate([o_a, o_b, o_c], axis=-1).reshape(m, d)
        mix = _matmul(cat, w_o_l, F32, 1024, 1024, "proj_out").reshape(b, s, d)
        x, h = _residual_norm(x, mix, g1, ln_g[l, 0][None, :], ln_b[l, 0][None, :], sc2, sh2)

        act = _gate_up(h.reshape(m, d), wg_l, wu_l, 1024, 512)
        ff = _matmul_ksplit(act, wd_l, 1024, 1024, D_FF_PAD // 4, "ffn_down").reshape(b, s, d)
        if l + 1 < DEPTH:
            nsh, nsc = mod[l + 1, :, 0], mod[l + 1, :, 1]
            x, h = _residual_norm(x, ff, g2, ln_g[l, 1][None, :], ln_b[l, 1][None, :], nsc, nsh)
        else:
            (x,) = _residual_norm(x, ff, g2, ln_g[l, 1][None, :], ln_b[l, 1][None, :])
    return x
```

```python
import functools
import math

import numpy as np
import jax
import jax.numpy as jnp
from jax import lax
from jax.experimental import pallas as pl
from jax.experimental.pallas import tpu as pltpu

D_MODEL = 4096
DEPTH = 2
HEAD_DIM = 128
N_HEADS = D_MODEL // HEAD_DIM
H_A = N_HEADS // 4
H_C = N_HEADS // 4
H_B = N_HEADS - H_A - H_C
KV_B = max(1, H_B // 8)
G_B = H_B // KV_B
DIFF_DIM = HEAD_DIM // 2
MOBA_BLOCK = 256
MOBA_TOPK = 3
WINDOW = 128
N_BUCKETS = 32
MAX_DISTANCE = 128
D_FF = -(-8 * D_MODEL // (3 * 256)) * 256
ALPHA = (2.0 * DEPTH) ** 0.25
LN_EPS = 1e-5
NEG = -1e30
LOG2E = math.log2(math.e)

QA_W = H_A * HEAD_DIM
QB_W = H_B * HEAD_DIM
KB_W = KV_B * HEAD_DIM
QC_W = H_C * 2 * DIFF_DIM
VC_W = H_C * HEAD_DIM
PROJ_SIZES = (QA_W, QA_W, QA_W, QB_W, KB_W, KB_W, QC_W, QC_W, VC_W)
PROJ_OFFS = tuple(int(sum(PROJ_SIZES[:i])) for i in range(len(PROJ_SIZES)))
D_PROJ = sum(PROJ_SIZES)

LANE = 128
ATTN_TILE = 512
D_FF_PAD = -(-D_FF // 1024) * 1024
VMEM_LIMIT = 56 * 1024 * 1024

F32 = jnp.float32
BF16 = jnp.bfloat16


def _bucket_upper_bounds():
    n = np.arange(0, 4 * MAX_DISTANCE, dtype=np.int64)
    max_exact = N_BUCKETS // 2
    nf = np.maximum(n, 1).astype(np.float32)
    large = max_exact + (np.log(nf / np.float32(max_exact)) / np.float32(math.log(MAX_DISTANCE / max_exact))
                         * np.float32(N_BUCKETS - max_exact)).astype(np.int32)
    large = np.minimum(large, N_BUCKETS - 1)
    bucket = np.where(n < max_exact, n, large)
    assert bucket[-1] == N_BUCKETS - 1 and np.all(np.diff(bucket) >= 0)
    return tuple(int(np.argmax(bucket > k)) for k in range(N_BUCKETS - 1))


BUCKET_UPPER = _bucket_upper_bounds()
FAR_DIST = BUCKET_UPPER[-1]


def _bias_from_dist(dist, tab):
    b = jnp.full(dist.shape, tab(N_BUCKETS - 1), F32)
    for k in range(N_BUCKETS - 2, -1, -1):
        b = jnp.where(dist < BUCKET_UPPER[k], tab(k), b)
    return b


def _dot(a, b):
    return jnp.dot(a, b, preferred_element_type=F32)


def _dot_nt(a, b):
    return lax.dot_general(a, b, (((1,), (1,)), ((), ())), preferred_element_type=F32)


def _params(*sem):
    return pltpu.CompilerParams(dimension_semantics=sem, vmem_limit_bytes=VMEM_LIMIT)


def _mod_kernel(c_ref, w_ref, b_ref, o_ref):
    c = c_ref[...]
    s = (c * jax.nn.sigmoid(c)).astype(BF16)
    o_ref[...] = _dot(s, w_ref[...].astype(BF16)) + b_ref[...]


def _ada_mod(c8, w_ada, b_ada3):
    depth, d, n = w_ada.shape
    tn = 512
    return pl.pallas_call(
        _mod_kernel,
        grid=(depth, n // tn),
        in_specs=[pl.BlockSpec((8, d), lambda l, j: (0, 0)),
                  pl.BlockSpec((None, d, tn), lambda l, j: (l, 0, j)),
                  pl.BlockSpec((None, 1, tn), lambda l, j: (l, 0, j))],
        out_specs=pl.BlockSpec((None, 8, tn), lambda l, j: (l, 0, j)),
        out_shape=jax.ShapeDtypeStruct((depth, 8, n), F32),
        compiler_params=_params("parallel", "parallel"),
        name="ada_mod",
    )(c8, w_ada, b_ada3)


def _modulate_kernel(x_ref, sc_ref, sh_ref, h_ref):
    h_ref[...] = (x_ref[...] * (1.0 + sc_ref[...]) + sh_ref[...]).astype(BF16)


def _modulate(x, sc, sh):
    b, s, d = x.shape
    tm = 512
    vec = pl.BlockSpec((None, 1, d), lambda i, j: (i, 0, 0))
    return pl.pallas_call(
        _modulate_kernel,
        grid=(b, s // tm),
        in_specs=[pl.BlockSpec((None, tm, d), lambda i, j: (i, j, 0)), vec, vec],
        out_specs=pl.BlockSpec((None, tm, d), lambda i, j: (i, j, 0)),
        out_shape=jax.ShapeDtypeStruct((b, s, d), BF16),
        compiler_params=_params("parallel", "parallel"),
        name="modulate",
    )(x, sc, sh)


def _mm_kernel(x_ref, w_ref, o_ref):
    o_ref[...] = _dot(x_ref[...], w_ref[...]).astype(o_ref.dtype)


def _matmul(x, w, l, out_dtype, tm, tn, name):
    m, k = x.shape
    _, _, n = w.shape
    return pl.pallas_call(
        _mm_kernel,
        grid=(m // tm, n // tn),
        in_specs=[pl.BlockSpec((tm, k), lambda i, j: (i, 0)),
                  pl.BlockSpec((None, k, tn), lambda i, j: (l, 0, j))],
        out_specs=pl.BlockSpec((tm, tn), lambda i, j: (i, j)),
        out_shape=jax.ShapeDtypeStruct((m, n), out_dtype),
        compiler_params=_params("parallel", "arbitrary"),
        name=name,
    )(x, w)


def _mm_acc_kernel(x_ref, w_ref, o_ref):
    part = _dot(x_ref[...], w_ref[...])

    @pl.when(pl.program_id(2) == 0)
    def _():
        o_ref[...] = part

    @pl.when(pl.program_id(2) > 0)
    def _():
        o_ref[...] += part


def _matmul_ksplit(x, w, l, tm, tn, tk, name):
    m, k = x.shape
    _, _, n = w.shape
    return pl.pallas_call(
        _mm_acc_kernel,
        grid=(m // tm, n // tn, k // tk),
        in_specs=[pl.BlockSpec((tm, tk), lambda i, j, kk: (i, kk)),
                  pl.BlockSpec((None, tk, tn), lambda i, j, kk: (l, kk, j))],
        out_specs=pl.BlockSpec((tm, tn), lambda i, j, kk: (i, j)),
        out_shape=jax.ShapeDtypeStruct((m, n), F32),
        compiler_params=_params("parallel", "parallel", "arbitrary"),
        name=name,
    )(x, w)


def _gate_up_kernel(h_ref, wg_ref, wu_ref, o_ref):
    h = h_ref[...]
    g = _dot(h, wg_ref[...])
    u = _dot(h, wu_ref[...])
    o_ref[...] = (g * jax.nn.sigmoid(g) * u).astype(BF16)


def _gate_up(h, wg, wu, l, tm, tf):
    m, k = h.shape
    _, _, f = wg.shape
    wspec = pl.BlockSpec((None, k, tf), lambda i, j: (l, 0, j))
    return pl.pallas_call(
        _gate_up_kernel,
        grid=(m // tm, f // tf),
        in_specs=[pl.BlockSpec((tm, k), lambda i, j: (i, 0)), wspec, wspec],
        out_specs=pl.BlockSpec((tm, tf), lambda i, j: (i, j)),
        out_shape=jax.ShapeDtypeStruct((m, f), BF16),
        compiler_params=_params("parallel", "arbitrary"),
        name="ffn_gate_up",
    )(h, wg, wu)


def _norm_kernel(with_h, x_ref, y_ref, gate_ref, lg_ref, lb_ref, *rest):
    z = ALPHA * x_ref[...] + gate_ref[...] * y_ref[...]
    mu = jnp.mean(z, axis=-1, keepdims=True)
    zc = z - mu
    var = jnp.mean(zc * zc, axis=-1, keepdims=True)
    xn = zc * lax.rsqrt(var + LN_EPS) * lg_ref[...] + lb_ref[...]
    if with_h:
        sc_ref, sh_ref, xo_ref, h_ref = rest
        xo_ref[...] = xn
        h_ref[...] = (xn * (1.0 + sc_ref[...]) + sh_ref[...]).astype(BF16)
    else:
        (xo_ref,) = rest
        xo_ref[...] = xn


def _residual_norm(x, y, gate, lg, lb, sc=None, sh=None):
    b, s, d = x.shape
    tm = 256
    with_h = sc is not None
    tile = pl.BlockSpec((None, tm, d), lambda i, j: (i, j, 0))
    vec = pl.BlockSpec((None, 1, d), lambda i, j: (i, 0, 0))
    par = pl.BlockSpec((1, d), lambda i, j: (0, 0))
    in_specs = [tile, tile, vec, par, par]
    args = [x, y, gate, lg, lb]
    out_specs = [tile]
    out_shape = [jax.ShapeDtypeStruct((b, s, d), F32)]
    if with_h:
        in_specs += [vec, vec]
        args += [sc, sh]
        out_specs.append(tile)
        out_shape.append(jax.ShapeDtypeStruct((b, s, d), BF16))
    return pl.pallas_call(
        functools.partial(_norm_kernel, with_h),
        grid=(b, s // tm),
        in_specs=in_specs,
        out_specs=out_specs,
        out_shape=out_shape,
        compiler_params=_params("parallel", "parallel"),
        name="residual_norm",
    )(*args)


def _lane_tiles(a):
    return [a[:, i * LANE:(i + 1) * LANE] for i in range(a.shape[1] // LANE)]


def _row_max_rep(s):
    tiles = _lane_tiles(s)
    t = tiles[0]
    for u in tiles[1:]:
        t = jnp.maximum(t, u)
    return jnp.broadcast_to(jnp.max(t, axis=1, keepdims=True), t.shape)


def _probs(s, m_rep):
    return jnp.concatenate([jnp.exp2(u - m_rep) for u in _lane_tiles(s)], axis=1).astype(BF16)


def _softmax_reset(m_ref, acc_ref):
    m_ref[...] = jnp.full(m_ref.shape, NEG, F32)
    acc_ref[...] = jnp.zeros(acc_ref.shape, F32)


def _softmax_update(s, v_aug, m_ref, acc_ref):
    m_old = m_ref[...]
    m = jnp.maximum(m_old, _row_max_rep(s))
    corr = jnp.exp2(m_old - m)
    m_ref[...] = m
    acc_ref[...] = jnp.concatenate([corr, corr], axis=1) * acc_ref[...] + _dot(_probs(s, m), v_aug)


def _augment_values(v_ref, vaug_ref):
    vaug_ref[:, :HEAD_DIM] = v_ref[...]
    vaug_ref[:, HEAD_DIM:] = jnp.ones((v_ref.shape[0], HEAD_DIM), BF16)


def _bias_tiles(tab, bias_ref):
    t = bias_ref.shape[1]
    i = lax.broadcasted_iota(jnp.int32, (t, t), 0)
    j = lax.broadcasted_iota(jnp.int32, (t, t), 1)
    far = tab(N_BUCKETS - 1)
    d0 = (_bias_from_dist(i - j, tab) - far) * LOG2E
    bias_ref[2] = jnp.where(i >= j, d0, NEG)
    bias_ref[1] = (_bias_from_dist(i - j + t, tab) - far) * LOG2E
    bias_ref[0] = jnp.zeros((t, t), F32)


def _causal_tiles(qi, logits_fn, update_fn, sa_ref, sb_ref):
    n = qi + 1
    sa_ref[...] = logits_fn(qi)

    def pair(p, carry):
        left = 2 * p
        kt = qi - left
        sb_ref[...] = logits_fn(jnp.maximum(kt - 1, 0))
        update_fn(sa_ref[...], left, kt)
        sa_ref[...] = logits_fn(jnp.maximum(kt - 2, 0))
        update_fn(sb_ref[...], left + 1, kt - 1)
        return carry

    lax.fori_loop(0, n // 2, pair, 0)

    @pl.when(n % 2 == 1)
    def _():
        update_fn(sa_ref[...], qi, 0)


def _moba_kernel(tab_ref, q_ref, k_ref, v_ref, o_ref, kmean_ref, bias_ref, vaug_ref, sa_ref, sb_ref,
                 m_ref, acc_ref):
    h = pl.program_id(0)
    bi = pl.program_id(1)
    qi = pl.program_id(2)
    t = q_ref.shape[0]
    blk = MOBA_BLOCK
    bpt = t // blk
    nb = k_ref.shape[0] // blk
    cs = HEAD_DIM ** -0.5 * LOG2E

    @pl.when((bi == 0) & (qi == 0))
    def _():
        _bias_tiles(lambda k: tab_ref[h, k], bias_ref)

    @pl.when(qi == 0)
    def _():
        for jb in range(nb):
            kb = k_ref[jb * blk:(jb + 1) * blk, :].astype(F32)
            kmean_ref[jb:jb + 1, :] = jnp.sum(kb, axis=0, keepdims=True) * (1.0 / blk)
        _augment_values(v_ref, vaug_ref)

    q = q_ref[...]
    km = kmean_ref[...]
    km_hi = km.astype(BF16)
    km_lo = (km - km_hi.astype(F32)).astype(BF16)
    gate = _dot_nt(q, km_hi) + _dot_nt(q, km_lo)
    col = lax.broadcasted_iota(jnp.int32, gate.shape, 1)
    row = lax.broadcasted_iota(jnp.int32, (t, 1), 0)
    qblk = qi * bpt + row // blk
    g = jnp.where(col < qblk, gate, -jnp.inf)
    bits = jnp.left_shift(1, qblk)
    for r in range(MOBA_TOPK):
        mx = jnp.max(g, axis=1, keepdims=True)
        first = jnp.min(jnp.where(g == mx, col, nb), axis=1, keepdims=True)
        bits = bits | jnp.where(r < qblk, jnp.left_shift(1, first), 0)
        g = jnp.where(col == first, -jnp.inf, g)
    bits = jnp.broadcast_to(bits, (t, LANE))

    def logits(kt):
        return _dot_nt(q, k_ref[pl.ds(pl.multiple_of(kt * t, t), t), :]) * cs

    def update(s, left, kt):
        s = s + bias_ref[jnp.maximum(2 - left, 0)]
        tiles = _lane_tiles(s)
        per_blk = blk // LANE
        out = []
        for c in range(bpt):
            vis = (jnp.right_shift(bits, kt * bpt + c) & 1) == 1
            out += [jnp.where(vis, u, NEG) for u in tiles[c * per_blk:(c + 1) * per_blk]]
        s = jnp.concatenate(out, axis=1)
        _softmax_update(s, vaug_ref[pl.ds(pl.multiple_of(kt * t, t), t), :], m_ref, acc_ref)

    _softmax_reset(m_ref, acc_ref)
    _causal_tiles(qi, logits, update, sa_ref, sb_ref)
    acc = acc_ref[...]
    o_ref[...] = (acc[:, :HEAD_DIM] / acc[:, HEAD_DIM:]).astype(o_ref.dtype)


def _moba(proj, tab_a):
    b, s, _ = proj.shape
    t = ATTN_TILE
    assert s % t == 0 and t % MOBA_BLOCK == 0 and t >= FAR_DIST and s // MOBA_BLOCK < 31
    qo, ko, vo = (PROJ_OFFS[i] // HEAD_DIM for i in (0, 1, 2))
    return pl.pallas_call(
        _moba_kernel,
        grid=(H_A, b, s // t),
        in_specs=[pl.BlockSpec(memory_space=pltpu.SMEM),
                  pl.BlockSpec((None, t, HEAD_DIM), lambda h, bi, qi: (bi, qi, qo + h)),
                  pl.BlockSpec((None, s, HEAD_DIM), lambda h, bi, qi: (bi, 0, ko + h)),
                  pl.BlockSpec((None, s, HEAD_DIM), lambda h, bi, qi: (bi, 0, vo + h))],
        out_specs=pl.BlockSpec((None, t, HEAD_DIM), lambda h, bi, qi: (bi, qi, h)),
        out_shape=jax.ShapeDtypeStruct((b, s, D_MODEL), BF16),
        scratch_shapes=[pltpu.VMEM((s // MOBA_BLOCK, HEAD_DIM), F32),
                        pltpu.VMEM((3, t, t), F32),
                        pltpu.VMEM((s, 2 * HEAD_DIM), BF16),
                        pltpu.VMEM((t, t), F32), pltpu.VMEM((t, t), F32),
                        pltpu.VMEM((t, LANE), F32),
                        pltpu.VMEM((t, 2 * HEAD_DIM), F32)],
        compiler_params=_params("arbitrary", "arbitrary", "arbitrary"),
        name="moba_attention",
    )(tab_a, proj, proj, proj)


def _swa_kernel(tab_ref, sink_ref, q_ref, k_ref, v_ref, cat_ref, o_ref, bias_ref, sink_rep_ref, vaug_ref):
    del cat_ref
    g = pl.program_id(1)
    ti = pl.program_id(2)
    w = WINDOW
    nsub = q_ref.shape[0] // w
    cs = HEAD_DIM ** -0.5 * LOG2E

    @pl.when(ti == 0)
    def _():
        i = lax.broadcasted_iota(jnp.int32, (w, 2 * w), 0)
        j = lax.broadcasted_iota(jnp.int32, (w, 2 * w), 1)
        dist = i + w - j
        band = (dist >= 0) & (dist < w)
        for hh in range(G_B):
            head = g * G_B + hh
            bias = _bias_from_dist(dist, lambda k: tab_ref[head, k]) * LOG2E
            bias_ref[hh * w:(hh + 1) * w, :] = jnp.where(band, bias, NEG)
            sink_rep_ref[hh * w:(hh + 1) * w, :] = jnp.full((w, LANE), sink_ref[head] * LOG2E, F32)
        _augment_values(v_ref, vaug_ref)

    def step(n, first_block):
        blk = ti * nsub + n
        r0 = pl.multiple_of(n * w, w)
        k0 = pl.multiple_of(jnp.maximum(blk - 1, 0) * w, w)
        k1 = pl.multiple_of(blk * w, w)
        qs = jnp.concatenate([q_ref[pl.ds(r0, w), hh * HEAD_DIM:(hh + 1) * HEAD_DIM] for hh in range(G_B)], axis=0)
        kband = jnp.concatenate([k_ref[pl.ds(k0, w), :], k_ref[pl.ds(k1, w), :]], axis=0)
        vband = jnp.concatenate([vaug_ref[pl.ds(k0, w), :], vaug_ref[pl.ds(k1, w), :]], axis=0)
        bias = bias_ref[...]
        if first_block:
            jj = lax.broadcasted_iota(jnp.int32, bias.shape, 1)
            bias = jnp.where(jj >= w, bias, NEG)
        s = _dot_nt(qs, kband) * cs + bias
        sink = sink_rep_ref[...]
        m = jnp.maximum(_row_max_rep(s), sink)
        acc = _dot(_probs(s, m), vband)
        o = acc[:, :HEAD_DIM] / (acc[:, HEAD_DIM:] + jnp.exp2(sink - m))
        for hh in range(G_B):
            o_ref[pl.ds(r0, w), hh * HEAD_DIM:(hh + 1) * HEAD_DIM] = o[hh * w:(hh + 1) * w, :].astype(o_ref.dtype)

    @pl.when(ti == 0)
    def _():
        step(0, True)

    def body(n, carry):
        step(n, False)
        return carry

    lax.fori_loop(jnp.where(ti == 0, 1, 0), nsub, body, 0)


def _swa(proj, tab_b, sinks, cat):
    b, s, _ = proj.shape
    tq = 1024
    gw = G_B * HEAD_DIM
    qo = PROJ_OFFS[3] // gw
    oo = QA_W // gw
    ko, vo = (PROJ_OFFS[i] // HEAD_DIM for i in (4, 5))
    assert PROJ_OFFS[3] % gw == 0 and QA_W % gw == 0 and s % tq == 0
    smem = pl.BlockSpec(memory_space=pltpu.SMEM)
    return pl.pallas_call(
        _swa_kernel,
        grid=(b, KV_B, s // tq),
        in_specs=[smem, smem,
                  pl.BlockSpec((None, tq, gw), lambda bi, g, ti: (bi, ti, qo + g)),
                  pl.BlockSpec((None, s, HEAD_DIM), lambda bi, g, ti: (bi, 0, ko + g)),
                  pl.BlockSpec((None, s, HEAD_DIM), lambda bi, g, ti: (bi, 0, vo + g)),
                  pl.BlockSpec(memory_space=pl.ANY)],
        out_specs=pl.BlockSpec((None, tq, gw), lambda bi, g, ti: (bi, ti, oo + g)),
        out_shape=jax.ShapeDtypeStruct(cat.shape, cat.dtype),
        input_output_aliases={5: 0},
        scratch_shapes=[pltpu.VMEM((G_B * WINDOW, 2 * WINDOW), F32),
                        pltpu.VMEM((G_B * WINDOW, LANE), F32),
                        pltpu.VMEM((s, 2 * HEAD_DIM), BF16)],
        compiler_params=_params("parallel", "parallel", "arbitrary"),
        name="swa_attention",
    )(tab_b, sinks, proj, proj, proj, cat)


def _diff_kernel(lam_init, tab_ref, q_ref, k_ref, v_ref, lamp_ref, subg_ref, cat_ref, o_ref,
                 bias_ref, vaug_ref, sa_ref, sb_ref, m_ref, acc_ref):
    del cat_ref
    h = pl.program_id(0)
    bi = pl.program_id(1)
    qi = pl.program_id(2)
    t = q_ref.shape[0]
    cs = DIFF_DIM ** -0.5 * LOG2E

    @pl.when((bi == 0) & (qi == 0))
    def _():
        _bias_tiles(lambda k: tab_ref[h, k], bias_ref)

    @pl.when(qi == 0)
    def _():
        _augment_values(v_ref, vaug_ref)

    q = q_ref[...]
    lane = lax.broadcasted_iota(jnp.int32, q.shape, 1)
    zero = jnp.zeros_like(q)
    q2 = jnp.concatenate([jnp.where(lane < DIFF_DIM, q, zero), jnp.where(lane >= DIFF_DIM, q, zero)], axis=0)

    def logits(kt):
        return _dot_nt(q2, k_ref[pl.ds(pl.multiple_of(kt * t, t), t), :]) * cs

    def update(s, left, kt):
        d = bias_ref[jnp.maximum(2 - left, 0)]
        s = jnp.concatenate([s[:t] + d, s[t:] + d], axis=0)
        _softmax_update(s, vaug_ref[pl.ds(pl.multiple_of(kt * t, t), t), :], m_ref, acc_ref)

    _softmax_reset(m_ref, acc_ref)
    _causal_tiles(qi, logits, update, sa_ref, sb_ref)

    lp = lamp_ref[...]
    lam = (jnp.exp(jnp.sum(lp[0:1] * lp[1:2], axis=1, keepdims=True))
           - jnp.exp(jnp.sum(lp[2:3] * lp[3:4], axis=1, keepdims=True)) + lam_init)
    acc = acc_ref[...]
    a = acc[:, :HEAD_DIM] / acc[:, HEAD_DIM:]
    o = a[:t] - lam * a[t:]
    o = o * lax.rsqrt(jnp.mean(o * o, axis=-1, keepdims=True) + LN_EPS) * subg_ref[...]
    o_ref[...] = (o * (1.0 - lam_init)).astype(o_ref.dtype)


def _diff(proj, tab_c, lam_p, subln_g, lam_init, cat):
    b, s, _ = proj.shape
    t = ATTN_TILE
    assert s % t == 0 and t >= FAR_DIST
    qo, ko, vo = (PROJ_OFFS[i] // HEAD_DIM for i in (6, 7, 8))
    oo = (QA_W + QB_W) // HEAD_DIM
    return pl.pallas_call(
        functools.partial(_diff_kernel, lam_init),
        grid=(H_C, b, s // t),
        in_specs=[pl.BlockSpec(memory_space=pltpu.SMEM),
                  pl.BlockSpec((None, t, HEAD_DIM), lambda h, bi, qi: (bi, qi, qo + h)),
                  pl.BlockSpec((None, s, HEAD_DIM), lambda h, bi, qi: (bi, 0, ko + h)),
                  pl.BlockSpec((None, s, HEAD_DIM), lambda h, bi, qi: (bi, 0, vo + h)),
                  pl.BlockSpec((4, DIFF_DIM), lambda h, bi, qi: (0, 0)),
                  pl.BlockSpec((1, HEAD_DIM), lambda h, bi, qi: (0, 0)),
                  pl.BlockSpec(memory_space=pl.ANY)],
        out_specs=pl.BlockSpec((None, t, HEAD_DIM), lambda h, bi, qi: (bi, qi, oo + h)),
        out_shape=jax.ShapeDtypeStruct(cat.shape, cat.dtype),
        input_output_aliases={6: 0},
        scratch_shapes=[pltpu.VMEM((3, t, t), F32),
                        pltpu.VMEM((s, 2 * HEAD_DIM), BF16),
                        pltpu.VMEM((2 * t, t), F32), pltpu.VMEM((2 * t, t), F32),
                        pltpu.VMEM((2 * t, LANE), F32),
                        pltpu.VMEM((2 * t, 2 * HEAD_DIM), F32)],
        compiler_params=_params("arbitrary", "arbitrary", "arbitrary"),
        name="diff_attention",
    )(tab_c, proj, proj, proj, lam_p, subln_g, cat)


def kernel(x, c, rel_bias, w_ada, b_ada, w_in, w_o, attn_sinks, diff_lambda, diff_subln_g,
           ln_g, ln_b, w_gate, w_up, w_down):
    b, s, d = x.shape
    m = b * s
    tab = rel_bias.T
    tab_a, tab_b, tab_c = tab[:H_A], tab[H_A:H_A + H_B], tab[H_A + H_B:]

    c8 = jnp.pad(c, ((0, 8 - b), (0, 0)))
    mod = _ada_mod(c8, w_ada, b_ada[:, None, :])[:, :b]
    mod = mod.reshape(DEPTH, b, 6, 1, d)

    fpad = D_FF_PAD - D_FF
    w_in_b = w_in.astype(BF16)
    w_o_b = w_o.astype(BF16)
    wg_b = jnp.pad(w_gate.astype(BF16), ((0, 0), (0, 0), (0, fpad)))
    wu_b = jnp.pad(w_up.astype(BF16), ((0, 0), (0, 0), (0, fpad)))
    wd_b = jnp.pad(w_down.astype(BF16), ((0, 0), (0, fpad), (0, 0)))

    h = None
    for l in range(DEPTH):
        sh1, sc1, g1, sh2, sc2, g2 = (mod[l, :, i] for i in range(6))
        if h is None:
            h = _modulate(x, sc1, sh1)
        proj = _matmul(h.reshape(m, d), w_in_b, l, BF16, 1024, 512, "proj_in").reshape(b, s, D_PROJ)
        lam_init = 0.8 - 0.6 * math.exp(-0.3 * l)
        cat = _moba(proj, tab_a)
        cat = _swa(proj, tab_b, attn_sinks[l], cat)
        cat = _diff(proj, tab_c, diff_lambda[l], diff_subln_g[l][None, :], lam_init, cat)
        mix = _matmul(cat.reshape(m, d), w_o_b, l, F32, 1024, 1024, "proj_out").reshape(b, s, d)
        x, h = _residual_norm(x, mix, g1, ln_g[l, 0][None, :], ln_b[l, 0][None, :], sc2, sh2)

        act = _gate_up(h.reshape(m, d), wg_b, wu_b, l, 1024, 512)
        ff = _matmul_ksplit(act, wd_b, l, 1024, 1024, D_FF_PAD // 4, "ffn_down").reshape(b, s, d)
        if l + 1 < DEPTH:
            nsh, nsc = mod[l + 1, :, 0], mod[l + 1, :, 1]
            x, h = _residual_norm(x, ff, g2, ln_g[l, 1][None, :], ln_b[l, 1][None, :], nsc, nsh)
        else:
            (x,) = _residual_norm(x, ff, g2, ln_g[l, 1][None, :], ln_b[l, 1][None, :])
    return x
```

```python
import functools
import math

import numpy as np
import jax
import jax.numpy as jnp
from jax import lax
from jax.experimental import pallas as pl
from jax.experimental.pallas import tpu as pltpu

D_MODEL = 4096
DEPTH = 2
HEAD_DIM = 128
N_HEADS = D_MODEL // HEAD_DIM
H_A = N_HEADS // 4
H_C = N_HEADS // 4
H_B = N_HEADS - H_A - H_C
KV_B = max(1, H_B // 8)
G_B = H_B // KV_B
DIFF_DIM = HEAD_DIM // 2
MOBA_BLOCK = 256
MOBA_TOPK = 3
WINDOW = 128
N_BUCKETS = 32
MAX_DISTANCE = 128
D_FF = -(-8 * D_MODEL // (3 * 256)) * 256
ALPHA = (2.0 * DEPTH) ** 0.25
LN_EPS = 1e-5
NEG = -1e30
LOG2E = math.log2(math.e)

QA_W = H_A * HEAD_DIM
QB_W = H_B * HEAD_DIM
KB_W = KV_B * HEAD_DIM
QC_W = H_C * 2 * DIFF_DIM
VC_W = H_C * HEAD_DIM
PROJ_SIZES = (QA_W, QA_W, QA_W, QB_W, KB_W, KB_W, QC_W, QC_W, VC_W)
PROJ_OFFS = tuple(int(sum(PROJ_SIZES[:i])) for i in range(len(PROJ_SIZES)))
D_PROJ = sum(PROJ_SIZES)

LANE = 128
ATTN_TILE = 512
D_FF_PAD = -(-D_FF // 1024) * 1024
VMEM_LIMIT = 56 * 1024 * 1024

F32 = jnp.float32
BF16 = jnp.bfloat16


def _bucket_upper_bounds():
    n = np.arange(0, 4 * MAX_DISTANCE, dtype=np.int64)
    max_exact = N_BUCKETS // 2
    nf = np.maximum(n, 1).astype(np.float32)
    large = max_exact + (np.log(nf / np.float32(max_exact)) / np.float32(math.log(MAX_DISTANCE / max_exact))
                         * np.float32(N_BUCKETS - max_exact)).astype(np.int32)
    large = np.minimum(large, N_BUCKETS - 1)
    bucket = np.where(n < max_exact, n, large)
    assert bucket[-1] == N_BUCKETS - 1 and np.all(np.diff(bucket) >= 0)
    return tuple(int(np.argmax(bucket > k)) for k in range(N_BUCKETS - 1))


BUCKET_UPPER = _bucket_upper_bounds()
FAR_DIST = BUCKET_UPPER[-1]


def _bias_from_dist(dist, tab):
    b = jnp.full(dist.shape, tab(N_BUCKETS - 1), F32)
    for k in range(N_BUCKETS - 2, -1, -1):
        b = jnp.where(dist < BUCKET_UPPER[k], tab(k), b)
    return b


def _dot(a, b):
    return jnp.dot(a, b, preferred_element_type=F32)


def _dot_nt(a, b):
    return lax.dot_general(a, b, (((1,), (1,)), ((), ())), preferred_element_type=F32)


def _params(*sem):
    return pltpu.CompilerParams(dimension_semantics=sem, vmem_limit_bytes=VMEM_LIMIT)


def _mod_kernel(c_ref, w_ref, b_ref, o_ref):
    c = c_ref[...]
    s = (c * jax.nn.sigmoid(c)).astype(BF16)
    o_ref[...] = _dot(s, w_ref[...].astype(BF16)) + b_ref[...]


def _ada_mod(c8, w_ada, b_ada3):
    depth, d, n = w_ada.shape
    tn = 512
    return pl.pallas_call(
        _mod_kernel,
        grid=(depth, n // tn),
        in_specs=[pl.BlockSpec((8, d), lambda l, j: (0, 0)),
                  pl.BlockSpec((None, d, tn), lambda l, j: (l, 0, j)),
                  pl.BlockSpec((None, 1, tn), lambda l, j: (l, 0, j))],
        out_specs=pl.BlockSpec((None, 8, tn), lambda l, j: (l, 0, j)),
        out_shape=jax.ShapeDtypeStruct((depth, 8, n), F32),
        compiler_params=_params("parallel", "parallel"),
        name="ada_mod",
    )(c8, w_ada, b_ada3)


def _modulate_kernel(x_ref, sc_ref, sh_ref, h_ref):
    h_ref[...] = (x_ref[...] * (1.0 + sc_ref[...]) + sh_ref[...]).astype(BF16)


def _modulate(x, sc, sh):
    b, s, d = x.shape
    tm = 512
    vec = pl.BlockSpec((None, 1, d), lambda i, j: (i, 0, 0))
    return pl.pallas_call(
        _modulate_kernel,
        grid=(b, s // tm),
        in_specs=[pl.BlockSpec((None, tm, d), lambda i, j: (i, j, 0)), vec, vec],
        out_specs=pl.BlockSpec((None, tm, d), lambda i, j: (i, j, 0)),
        out_shape=jax.ShapeDtypeStruct((b, s, d), BF16),
        compiler_params=_params("parallel", "parallel"),
        name="modulate",
    )(x, sc, sh)


def _mm_kernel(x_ref, w_ref, o_ref):
    o_ref[...] = _dot(x_ref[...], w_ref[...]).astype(o_ref.dtype)


def _matmul(x, w, l, out_dtype, tm, tn, name):
    m, k = x.shape
    _, _, n = w.shape
    return pl.pallas_call(
        _mm_kernel,
        grid=(m // tm, n // tn),
        in_specs=[pl.BlockSpec((tm, k), lambda i, j: (i, 0)),
                  pl.BlockSpec((None, k, tn), lambda i, j: (l, 0, j))],
        out_specs=pl.BlockSpec((tm, tn), lambda i, j: (i, j)),
        out_shape=jax.ShapeDtypeStruct((m, n), out_dtype),
        compiler_params=_params("parallel", "arbitrary"),
        name=name,
    )(x, w)


def _mm_acc_kernel(x_ref, w_ref, o_ref, acc_ref):
    kk = pl.program_id(2)
    part = _dot(x_ref[...], w_ref[...])

    @pl.when(kk == 0)
    def _():
        acc_ref[...] = part

    @pl.when((kk > 0) & (kk < pl.num_programs(2) - 1))
    def _():
        acc_ref[...] += part

    @pl.when(kk == pl.num_programs(2) - 1)
    def _():
        o_ref[...] = (acc_ref[...] + part).astype(o_ref.dtype)


def _matmul_ksplit(x, w, l, out_dtype, tm, tn, tk, name):
    m, k = x.shape
    _, _, n = w.shape
    assert k // tk >= 2
    return pl.pallas_call(
        _mm_acc_kernel,
        grid=(m // tm, n // tn, k // tk),
        in_specs=[pl.BlockSpec((tm, tk), lambda i, j, kk: (i, kk)),
                  pl.BlockSpec((None, tk, tn), lambda i, j, kk: (l, kk, j))],
        out_specs=pl.BlockSpec((tm, tn), lambda i, j, kk: (i, j)),
        out_shape=jax.ShapeDtypeStruct((m, n), out_dtype),
        scratch_shapes=[pltpu.VMEM((tm, tn), F32)],
        compiler_params=_params("parallel", "parallel", "arbitrary"),
        name=name,
    )(x, w)


def _cast_pad_cols_kernel(x_ref, o_ref):
    n = x_ref.shape[1]
    o_ref[:, :n] = x_ref[...].astype(BF16)
    o_ref[:, n:] = jnp.zeros((o_ref.shape[0], o_ref.shape[1] - n), BF16)


def _cast_pad_cols(w, n_pad):
    depth, k, n = w.shape
    tr = 256
    return pl.pallas_call(
        _cast_pad_cols_kernel,
        grid=(depth, k // tr),
        in_specs=[pl.BlockSpec((None, tr, n), lambda l, r: (l, r, 0))],
        out_specs=pl.BlockSpec((None, tr, n_pad), lambda l, r: (l, r, 0)),
        out_shape=jax.ShapeDtypeStruct((depth, k, n_pad), BF16),
        compiler_params=_params("parallel", "parallel"),
        name="cast_pad_cols",
    )(w)


def _cast_pad_rows_kernel(n_src, x_ref, o_ref):
    r = pl.program_id(1)

    @pl.when(r < n_src)
    def _():
        o_ref[...] = x_ref[...].astype(BF16)

    @pl.when(r >= n_src)
    def _():
        o_ref[...] = jnp.zeros(o_ref.shape, BF16)


def _cast_pad_rows(w, k_pad):
    depth, k, n = w.shape
    tr = 256
    assert k % tr == 0 and k_pad % tr == 0
    n_src = k // tr
    return pl.pallas_call(
        functools.partial(_cast_pad_rows_kernel, n_src),
        grid=(depth, k_pad // tr),
        in_specs=[pl.BlockSpec((None, tr, n), lambda l, r: (l, jnp.minimum(r, n_src - 1), 0))],
        out_specs=pl.BlockSpec((None, tr, n), lambda l, r: (l, r, 0)),
        out_shape=jax.ShapeDtypeStruct((depth, k_pad, n), BF16),
        compiler_params=_params("parallel", "arbitrary"),
        name="cast_pad_rows",
    )(w)


def _gate_up_kernel(h_ref, wg_ref, wu_ref, o_ref):
    h = h_ref[...]
    g = _dot(h, wg_ref[...])
    u = _dot(h, wu_ref[...])
    o_ref[...] = (g * jax.nn.sigmoid(g) * u).astype(BF16)


def _gate_up(h, wg, wu, l, tm, tf):
    m, k = h.shape
    _, _, f = wg.shape
    wspec = pl.BlockSpec((None, k, tf), lambda i, j: (l, 0, j))
    return pl.pallas_call(
        _gate_up_kernel,
        grid=(m // tm, f // tf),
        in_specs=[pl.BlockSpec((tm, k), lambda i, j: (i, 0)), wspec, wspec],
        out_specs=pl.BlockSpec((tm, tf), lambda i, j: (i, j)),
        out_shape=jax.ShapeDtypeStruct((m, f), BF16),
        compiler_params=_params("parallel", "arbitrary"),
        name="ffn_gate_up",
    )(h, wg, wu)


def _norm_kernel(with_h, x_ref, y_ref, gate_ref, lg_ref, lb_ref, *rest):
    z = ALPHA * x_ref[...] + gate_ref[...] * y_ref[...].astype(F32)
    mu = jnp.mean(z, axis=-1, keepdims=True)
    zc = z - mu
    var = jnp.mean(zc * zc, axis=-1, keepdims=True)
    xn = zc * lax.rsqrt(var + LN_EPS) * lg_ref[...] + lb_ref[...]
    if with_h:
        sc_ref, sh_ref, xo_ref, h_ref = rest
        xo_ref[...] = xn
        h_ref[...] = (xn * (1.0 + sc_ref[...]) + sh_ref[...]).astype(BF16)
    else:
        (xo_ref,) = rest
        xo_ref[...] = xn


def _residual_norm(x, y, gate, lg, lb, sc=None, sh=None):
    b, s, d = x.shape
    tm = 256
    with_h = sc is not None
    tile = pl.BlockSpec((None, tm, d), lambda i, j: (i, j, 0))
    vec = pl.BlockSpec((None, 1, d), lambda i, j: (i, 0, 0))
    par = pl.BlockSpec((1, d), lambda i, j: (0, 0))
    in_specs = [tile, tile, vec, par, par]
    args = [x, y, gate, lg, lb]
    out_specs = [tile]
    out_shape = [jax.ShapeDtypeStruct((b, s, d), F32)]
    if with_h:
        in_specs += [vec, vec]
        args += [sc, sh]
        out_specs.append(tile)
        out_shape.append(jax.ShapeDtypeStruct((b, s, d), BF16))
    return pl.pallas_call(
        functools.partial(_norm_kernel, with_h),
        grid=(b, s // tm),
        in_specs=in_specs,
        out_specs=out_specs,
        out_shape=out_shape,
        compiler_params=_params("parallel", "parallel"),
        name="residual_norm",
    )(*args)


def _lane_tiles(a):
    return [a[:, i * LANE:(i + 1) * LANE] for i in range(a.shape[1] // LANE)]


def _row_max_rep(s):
    tiles = _lane_tiles(s)
    t = tiles[0]
    for u in tiles[1:]:
        t = jnp.maximum(t, u)
    return jnp.broadcast_to(jnp.max(t, axis=1, keepdims=True), t.shape)


def _probs(s, m_rep):
    return jnp.concatenate([jnp.exp2(u - m_rep) for u in _lane_tiles(s)], axis=1).astype(BF16)


def _softmax_reset(m_ref, acc_ref):
    m_ref[...] = jnp.full(m_ref.shape, NEG, F32)
    acc_ref[...] = jnp.zeros(acc_ref.shape, F32)


def _softmax_update(s, v_aug, m_ref, acc_ref):
    m_old = m_ref[...]
    m = jnp.maximum(m_old, _row_max_rep(s))
    corr = jnp.exp2(m_old - m)
    m_ref[...] = m
    acc_ref[...] = jnp.concatenate([corr, corr], axis=1) * acc_ref[...] + _dot(_probs(s, m), v_aug)


def _augment_values(v_ref, vaug_ref):
    vaug_ref[:, :HEAD_DIM] = v_ref[...]
    vaug_ref[:, HEAD_DIM:] = jnp.ones((v_ref.shape[0], HEAD_DIM), BF16)


def _bias_tiles(tab, bias_ref):
    t = bias_ref.shape[1]
    i = lax.broadcasted_iota(jnp.int32, (t, t), 0)
    j = lax.broadcasted_iota(jnp.int32, (t, t), 1)
    far = tab(N_BUCKETS - 1)
    d0 = (_bias_from_dist(i - j, tab) - far) * LOG2E
    bias_ref[2] = jnp.where(i >= j, d0, NEG)
    bias_ref[1] = (_bias_from_dist(i - j + t, tab) - far) * LOG2E
    bias_ref[0] = jnp.zeros((t, t), F32)


def _causal_tiles(qi, logits_fn, update_fn, sa_ref, sb_ref):
    n = qi + 1
    sa_ref[...] = logits_fn(qi)

    def pair(p, carry):
        left = 2 * p
        kt = qi - left
        sb_ref[...] = logits_fn(jnp.maximum(kt - 1, 0))
        update_fn(sa_ref[...], left, kt)
        sa_ref[...] = logits_fn(jnp.maximum(kt - 2, 0))
        update_fn(sb_ref[...], left + 1, kt - 1)
        return carry

    lax.fori_loop(0, n // 2, pair, 0)

    @pl.when(n % 2 == 1)
    def _():
        update_fn(sa_ref[...], qi, 0)


def _moba_kernel(tab_ref, q_ref, k_ref, v_ref, o_ref, kmean_ref, bias_ref, vaug_ref, sa_ref, sb_ref,
                 m_ref, acc_ref):
    h = pl.program_id(0)
    bi = pl.program_id(1)
    qi = pl.program_id(2)
    t = q_ref.shape[0]
    blk = MOBA_BLOCK
    bpt = t // blk
    nb = k_ref.shape[0] // blk
    cs = HEAD_DIM ** -0.5 * LOG2E

    @pl.when((bi == 0) & (qi == 0))
    def _():
        _bias_tiles(lambda k: tab_ref[h, k], bias_ref)

    @pl.when(qi == 0)
    def _():
        for jb in range(nb):
            kb = k_ref[jb * blk:(jb + 1) * blk, :].astype(F32)
            kmean_ref[jb:jb + 1, :] = jnp.sum(kb, axis=0, keepdims=True) * (1.0 / blk)
        _augment_values(v_ref, vaug_ref)

    q = q_ref[...]
    km = kmean_ref[...]
    km_hi = km.astype(BF16)
    km_lo = (km - km_hi.astype(F32)).astype(BF16)
    gate = _dot_nt(q, km_hi) + _dot_nt(q, km_lo)
    col = lax.broadcasted_iota(jnp.int32, gate.shape, 1)
    row = lax.broadcasted_iota(jnp.int32, (t, 1), 0)
    qblk = qi * bpt + row // blk
    g = jnp.where(col < qblk, gate, -jnp.inf)
    bits = jnp.left_shift(1, qblk)
    for r in range(MOBA_TOPK):
        mx = jnp.max(g, axis=1, keepdims=True)
        first = jnp.min(jnp.where(g == mx, col, nb), axis=1, keepdims=True)
        bits = bits | jnp.where(r < qblk, jnp.left_shift(1, first), 0)
        g = jnp.where(col == first, -jnp.inf, g)
    bits = jnp.broadcast_to(bits, (t, LANE))

    def logits(kt):
        return _dot_nt(q, k_ref[pl.ds(pl.multiple_of(kt * t, t), t), :]) * cs

    def update(s, left, kt):
        s = s + bias_ref[jnp.maximum(2 - left, 0)]
        tiles = _lane_tiles(s)
        per_blk = blk // LANE
        out = []
        for c in range(bpt):
            vis = (jnp.right_shift(bits, kt * bpt + c) & 1) == 1
            out += [jnp.where(vis, u, NEG) for u in tiles[c * per_blk:(c + 1) * per_blk]]
        s = jnp.concatenate(out, axis=1)
        _softmax_update(s, vaug_ref[pl.ds(pl.multiple_of(kt * t, t), t), :], m_ref, acc_ref)

    _softmax_reset(m_ref, acc_ref)
    _causal_tiles(qi, logits, update, sa_ref, sb_ref)
    acc = acc_ref[...]
    o_ref[...] = (acc[:, :HEAD_DIM] / acc[:, HEAD_DIM:]).astype(o_ref.dtype)


def _moba(proj, tab_a):
    b, s, _ = proj.shape
    t = ATTN_TILE
    assert s % t == 0 and t % MOBA_BLOCK == 0 and t >= FAR_DIST and s // MOBA_BLOCK < 31
    qo, ko, vo = (PROJ_OFFS[i] // HEAD_DIM for i in (0, 1, 2))
    return pl.pallas_call(
        _moba_kernel,
        grid=(H_A, b, s // t),
        in_specs=[pl.BlockSpec(memory_space=pltpu.SMEM),
                  pl.BlockSpec((None, t, HEAD_DIM), lambda h, bi, qi: (bi, qi, qo + h)),
                  pl.BlockSpec((None, s, HEAD_DIM), lambda h, bi, qi: (bi, 0, ko + h)),
                  pl.BlockSpec((None, s, HEAD_DIM), lambda h, bi, qi: (bi, 0, vo + h))],
        out_specs=pl.BlockSpec((None, t, HEAD_DIM), lambda h, bi, qi: (bi, qi, h)),
        out_shape=jax.ShapeDtypeStruct((b, s, D_MODEL), BF16),
        scratch_shapes=[pltpu.VMEM((s // MOBA_BLOCK, HEAD_DIM), F32),
                        pltpu.VMEM((3, t, t), F32),
                        pltpu.VMEM((s, 2 * HEAD_DIM), BF16),
                        pltpu.VMEM((t, t), F32), pltpu.VMEM((t, t), F32),
                        pltpu.VMEM((t, LANE), F32),
                        pltpu.VMEM((t, 2 * HEAD_DIM), F32)],
        compiler_params=_params("arbitrary", "arbitrary", "arbitrary"),
        name="moba_attention",
    )(tab_a, proj, proj, proj)


def _swa_kernel(tab_ref, sink_ref, q_ref, k_ref, v_ref, cat_ref, o_ref, bias_ref, sink_rep_ref, vaug_ref):
    del cat_ref
    g = pl.program_id(1)
    ti = pl.program_id(2)
    w = WINDOW
    nsub = q_ref.shape[0] // w
    cs = HEAD_DIM ** -0.5 * LOG2E

    @pl.when(ti == 0)
    def _():
        i = lax.broadcasted_iota(jnp.int32, (w, 2 * w), 0)
        j = lax.broadcasted_iota(jnp.int32, (w, 2 * w), 1)
        dist = i + w - j
        band = (dist >= 0) & (dist < w)
        for hh in range(G_B):
            head = g * G_B + hh
            bias = _bias_from_dist(dist, lambda k: tab_ref[head, k]) * LOG2E
            bias_ref[hh * w:(hh + 1) * w, :] = jnp.where(band, bias, NEG)
            sink_rep_ref[hh * w:(hh + 1) * w, :] = jnp.full((w, LANE), sink_ref[head] * LOG2E, F32)
        _augment_values(v_ref, vaug_ref)

    def step(n, first_block):
        blk = ti * nsub + n
        r0 = pl.multiple_of(n * w, w)
        k0 = pl.multiple_of(jnp.maximum(blk - 1, 0) * w, w)
        k1 = pl.multiple_of(blk * w, w)
        qs = jnp.concatenate([q_ref[pl.ds(r0, w), hh * HEAD_DIM:(hh + 1) * HEAD_DIM] for hh in range(G_B)], axis=0)
        kband = jnp.concatenate([k_ref[pl.ds(k0, w), :], k_ref[pl.ds(k1, w), :]], axis=0)
        vband = jnp.concatenate([vaug_ref[pl.ds(k0, w), :], vaug_ref[pl.ds(k1, w), :]], axis=0)
        bias = bias_ref[...]
        if first_block:
            jj = lax.broadcasted_iota(jnp.int32, bias.shape, 1)
            bias = jnp.where(jj >= w, bias, NEG)
        s = _dot_nt(qs, kband) * cs + bias
        sink = sink_rep_ref[...]
        m = jnp.maximum(_row_max_rep(s), sink)
        acc = _dot(_probs(s, m), vband)
        o = acc[:, :HEAD_DIM] / (acc[:, HEAD_DIM:] + jnp.exp2(sink - m))
        for hh in range(G_B):
            o_ref[pl.ds(r0, w), hh * HEAD_DIM:(hh + 1) * HEAD_DIM] = o[hh * w:(hh + 1) * w, :].astype(o_ref.dtype)

    @pl.when(ti == 0)
    def _():
        step(0, True)

    def body(n, carry):
        step(n, False)
        return carry

    lax.fori_loop(jnp.where(ti == 0, 1, 0), nsub, body, 0)


def _swa(proj, tab_b, sinks, cat):
    b, s, _ = proj.shape
    tq = 1024
    gw = G_B * HEAD_DIM
    qo = PROJ_OFFS[3] // gw
    oo = QA_W // gw
    ko, vo = (PROJ_OFFS[i] // HEAD_DIM for i in (4, 5))
    assert PROJ_OFFS[3] % gw == 0 and QA_W % gw == 0 and s % tq == 0
    smem = pl.BlockSpec(memory_space=pltpu.SMEM)
    return pl.pallas_call(
        _swa_kernel,
        grid=(b, KV_B, s // tq),
        in_specs=[smem, smem,
                  pl.BlockSpec((None, tq, gw), lambda bi, g, ti: (bi, ti, qo + g)),
                  pl.BlockSpec((None, s, HEAD_DIM), lambda bi, g, ti: (bi, 0, ko + g)),
                  pl.BlockSpec((None, s, HEAD_DIM), lambda bi, g, ti: (bi, 0, vo + g)),
                  pl.BlockSpec(memory_space=pl.ANY)],
        out_specs=pl.BlockSpec((None, tq, gw), lambda bi, g, ti: (bi, ti, oo + g)),
        out_shape=jax.ShapeDtypeStruct(cat.shape, cat.dtype),
        input_output_aliases={5: 0},
        scratch_shapes=[pltpu.VMEM((G_B * WINDOW, 2 * WINDOW), F32),
                        pltpu.VMEM((G_B * WINDOW, LANE), F32),
                        pltpu.VMEM((s, 2 * HEAD_DIM), BF16)],
        compiler_params=_params("parallel", "parallel", "arbitrary"),
        name="swa_attention",
    )(tab_b, sinks, proj, proj, proj, cat)


def _diff_kernel(lam_init, tab_ref, q_ref, k_ref, v_ref, lamp_ref, subg_ref, cat_ref, o_ref,
                 bias_ref, vaug_ref, sa_ref, sb_ref, m_ref, acc_ref):
    del cat_ref
    h = pl.program_id(0)
    bi = pl.program_id(1)
    qi = pl.program_id(2)
    t = q_ref.shape[0]
    cs = DIFF_DIM ** -0.5 * LOG2E

    @pl.when((bi == 0) & (qi == 0))
    def _():
        _bias_tiles(lambda k: tab_ref[h, k], bias_ref)

    @pl.when(qi == 0)
    def _():
        _augment_values(v_ref, vaug_ref)

    q = q_ref[...]
    lane = lax.broadcasted_iota(jnp.int32, q.shape, 1)
    zero = jnp.zeros_like(q)
    q2 = jnp.concatenate([jnp.where(lane < DIFF_DIM, q, zero), jnp.where(lane >= DIFF_DIM, q, zero)], axis=0)

    def logits(kt):
        return _dot_nt(q2, k_ref[pl.ds(pl.multiple_of(kt * t, t), t), :]) * cs

    def update(s, left, kt):
        d = bias_ref[jnp.maximum(2 - left, 0)]
        s = jnp.concatenate([s[:t] + d, s[t:] + d], axis=0)
        _softmax_update(s, vaug_ref[pl.ds(pl.multiple_of(kt * t, t), t), :], m_ref, acc_ref)

    _softmax_reset(m_ref, acc_ref)
    _causal_tiles(qi, logits, update, sa_ref, sb_ref)

    lp = lamp_ref[...]
    lam = (jnp.exp(jnp.sum(lp[0:1] * lp[1:2], axis=1, keepdims=True))
           - jnp.exp(jnp.sum(lp[2:3] * lp[3:4], axis=1, keepdims=True)) + lam_init)
    acc = acc_ref[...]
    a = acc[:, :HEAD_DIM] / acc[:, HEAD_DIM:]
    o = a[:t] - lam * a[t:]
    o = o * lax.rsqrt(jnp.mean(o * o, axis=-1, keepdims=True) + LN_EPS) * subg_ref[...]
    o_ref[...] = (o * (1.0 - lam_init)).astype(o_ref.dtype)


def _diff(proj, tab_c, lam_p, subln_g, lam_init, cat):
    b, s, _ = proj.shape
    t = ATTN_TILE
    assert s % t == 0 and t >= FAR_DIST
    qo, ko, vo = (PROJ_OFFS[i] // HEAD_DIM for i in (6, 7, 8))
    oo = (QA_W + QB_W) // HEAD_DIM
    return pl.pallas_call(
        functools.partial(_diff_kernel, lam_init),
        grid=(H_C, b, s // t),
        in_specs=[pl.BlockSpec(memory_space=pltpu.SMEM),
                  pl.BlockSpec((None, t, HEAD_DIM), lambda h, bi, qi: (bi, qi, qo + h)),
                  pl.BlockSpec((None, s, HEAD_DIM), lambda h, bi, qi: (bi, 0, ko + h)),
                  pl.BlockSpec((None, s, HEAD_DIM), lambda h, bi, qi: (bi, 0, vo + h)),
                  pl.BlockSpec((4, DIFF_DIM), lambda h, bi, qi: (0, 0)),
                  pl.BlockSpec((1, HEAD_DIM), lambda h, bi, qi: (0, 0)),
                  pl.BlockSpec(memory_space=pl.ANY)],
        out_specs=pl.BlockSpec((None, t, HEAD_DIM), lambda h, bi, qi: (bi, qi, oo + h)),
        out_shape=jax.ShapeDtypeStruct(cat.shape, cat.dtype),
        input_output_aliases={6: 0},
        scratch_shapes=[pltpu.VMEM((3, t, t), F32),
                        pltpu.VMEM((s, 2 * HEAD_DIM), BF16),
                        pltpu.VMEM((2 * t, t), F32), pltpu.VMEM((2 * t, t), F32),
                        pltpu.VMEM((2 * t, LANE), F32),
                        pltpu.VMEM((2 * t, 2 * HEAD_DIM), F32)],
        compiler_params=_params("arbitrary", "arbitrary", "arbitrary"),
        name="diff_attention",
    )(tab_c, proj, proj, proj, lam_p, subln_g, cat)


def kernel(x, c, rel_bias, w_ada, b_ada, w_in, w_o, attn_sinks, diff_lambda, diff_subln_g,
           ln_g, ln_b, w_gate, w_up, w_down):
    b, s, d = x.shape
    m = b * s
    tab = rel_bias.T
    tab_a, tab_b, tab_c = tab[:H_A], tab[H_A:H_A + H_B], tab[H_A + H_B:]

    c8 = jnp.pad(c, ((0, 8 - b), (0, 0)))
    mod = _ada_mod(c8, w_ada, b_ada[:, None, :])[:, :b]
    mod = mod.reshape(DEPTH, b, 6, 1, d)

    w_in_b = w_in.astype(BF16)
    w_o_b = w_o.astype(BF16)
    wg_b = _cast_pad_cols(w_gate, D_FF_PAD)
    wu_b = _cast_pad_cols(w_up, D_FF_PAD)
    wd_b = _cast_pad_rows(w_down, D_FF_PAD)

    h = None
    for l in range(DEPTH):
        sh1, sc1, g1, sh2, sc2, g2 = (mod[l, :, i] for i in range(6))
        if h is None:
            h = _modulate(x, sc1, sh1)
        proj = _matmul(h.reshape(m, d), w_in_b, l, BF16, 1024, 512, "proj_in").reshape(b, s, D_PROJ)
        lam_init = 0.8 - 0.6 * math.exp(-0.3 * l)
        cat = _moba(proj, tab_a)
        cat = _swa(proj, tab_b, attn_sinks[l], cat)
        cat = _diff(proj, tab_c, diff_lambda[l], diff_subln_g[l][None, :], lam_init, cat)
        mix = _matmul(cat.reshape(m, d), w_o_b, l, BF16, 1024, 1024, "proj_out").reshape(b, s, d)
        x, h = _residual_norm(x, mix, g1, ln_g[l, 0][None, :], ln_b[l, 0][None, :], sc2, sh2)

        act = _gate_up(h.reshape(m, d), wg_b, wu_b, l, 1024, 512)
        ff = _matmul_ksplit(act, wd_b, l, BF16, 1024, 1024, D_FF_PAD // 2, "ffn_down").reshape(b, s, d)
        if l + 1 < DEPTH:
            nsh, nsc = mod[l + 1, :, 0], mod[l + 1, :, 1]
            x, h = _residual_norm(x, ff, g2, ln_g[l, 1][None, :], ln_b[l, 1][None, :], nsc, nsh)
        else:
            (x,) = _residual_norm(x, ff, g2, ln_g[l, 1][None, :], ln_b[l, 1][None, :])
    return x
```

```python
import functools
import math

import numpy as np
import jax
import jax.numpy as jnp
from jax import lax
from jax.experimental import pallas as pl
from jax.experimental.pallas import tpu as pltpu

D_MODEL = 4096
DEPTH = 2
HEAD_DIM = 128
N_HEADS = D_MODEL // HEAD_DIM
H_A = N_HEADS // 4
H_C = N_HEADS // 4
H_B = N_HEADS - H_A - H_C
KV_B = max(1, H_B // 8)
G_B = H_B // KV_B
DIFF_DIM = HEAD_DIM // 2
MOBA_BLOCK = 256
MOBA_TOPK = 3
WINDOW = 128
N_BUCKETS = 32
MAX_DISTANCE = 128
D_FF = -(-8 * D_MODEL // (3 * 256)) * 256
ALPHA = (2.0 * DEPTH) ** 0.25
LN_EPS = 1e-5
NEG = -1e30
LOG2E = math.log2(math.e)

QA_W = H_A * HEAD_DIM
QB_W = H_B * HEAD_DIM
KB_W = KV_B * HEAD_DIM
QC_W = H_C * 2 * DIFF_DIM
VC_W = H_C * HEAD_DIM
PROJ_SIZES = (QA_W, QA_W, QA_W, QB_W, KB_W, KB_W, QC_W, QC_W, VC_W)
PROJ_OFFS = tuple(int(sum(PROJ_SIZES[:i])) for i in range(len(PROJ_SIZES)))
D_PROJ = sum(PROJ_SIZES)

LANE = 128
ATTN_TILE = 512
D_FF_PAD = -(-D_FF // 1024) * 1024
VMEM_LIMIT = 56 * 1024 * 1024

F32 = jnp.float32
BF16 = jnp.bfloat16


def _bucket_upper_bounds():
    n = np.arange(0, 4 * MAX_DISTANCE, dtype=np.int64)
    max_exact = N_BUCKETS // 2
    nf = np.maximum(n, 1).astype(np.float32)
    large = max_exact + (np.log(nf / np.float32(max_exact)) / np.float32(math.log(MAX_DISTANCE / max_exact))
                         * np.float32(N_BUCKETS - max_exact)).astype(np.int32)
    large = np.minimum(large, N_BUCKETS - 1)
    bucket = np.where(n < max_exact, n, large)
    assert bucket[-1] == N_BUCKETS - 1 and np.all(np.diff(bucket) >= 0)
    return tuple(int(np.argmax(bucket > k)) for k in range(N_BUCKETS - 1))


BUCKET_UPPER = _bucket_upper_bounds()
FAR_DIST = BUCKET_UPPER[-1]


def _bias_from_dist(dist, tab):
    b = jnp.full(dist.shape, tab(N_BUCKETS - 1), F32)
    for k in range(N_BUCKETS - 2, -1, -1):
        b = jnp.where(dist < BUCKET_UPPER[k], tab(k), b)
    return b


def _dot(a, b):
    return jnp.dot(a, b, preferred_element_type=F32)


def _dot_nt(a, b):
    return lax.dot_general(a, b, (((1,), (1,)), ((), ())), preferred_element_type=F32)


def _params(*sem):
    return pltpu.CompilerParams(dimension_semantics=sem, vmem_limit_bytes=VMEM_LIMIT)


def _mod_kernel(c_ref, w_ref, b_ref, o_ref):
    c = c_ref[...]
    s = (c * jax.nn.sigmoid(c)).astype(BF16)
    o_ref[...] = _dot(s, w_ref[...].astype(BF16)) + b_ref[...]


def _ada_mod(c8, w_ada, b_ada3):
    depth, d, n = w_ada.shape
    tn = 512
    return pl.pallas_call(
        _mod_kernel,
        grid=(depth, n // tn),
        in_specs=[pl.BlockSpec((8, d), lambda l, j: (0, 0)),
                  pl.BlockSpec((None, d, tn), lambda l, j: (l, 0, j)),
                  pl.BlockSpec((None, 1, tn), lambda l, j: (l, 0, j))],
        out_specs=pl.BlockSpec((None, 8, tn), lambda l, j: (l, 0, j)),
        out_shape=jax.ShapeDtypeStruct((depth, 8, n), F32),
        compiler_params=_params("parallel", "parallel"),
        name="ada_mod",
    )(c8, w_ada, b_ada3)


def _modulate_kernel(x_ref, sc_ref, sh_ref, h_ref):
    h_ref[...] = (x_ref[...] * (1.0 + sc_ref[...]) + sh_ref[...]).astype(BF16)


def _modulate(x, sc, sh):
    b, s, d = x.shape
    tm = 512
    vec = pl.BlockSpec((None, 1, d), lambda i, j: (i, 0, 0))
    return pl.pallas_call(
        _modulate_kernel,
        grid=(b, s // tm),
        in_specs=[pl.BlockSpec((None, tm, d), lambda i, j: (i, j, 0)), vec, vec],
        out_specs=pl.BlockSpec((None, tm, d), lambda i, j: (i, j, 0)),
        out_shape=jax.ShapeDtypeStruct((b, s, d), BF16),
        compiler_params=_params("parallel", "parallel"),
        name="modulate",
    )(x, sc, sh)


def _mm_kernel(x_ref, w_ref, o_ref):
    o_ref[...] = _dot(x_ref[...], w_ref[...]).astype(o_ref.dtype)


def _matmul(x, w, l, out_dtype, tm, tn, name):
    m, k = x.shape
    _, _, n = w.shape
    return pl.pallas_call(
        _mm_kernel,
        grid=(m // tm, n // tn),
        in_specs=[pl.BlockSpec((tm, k), lambda i, j: (i, 0)),
                  pl.BlockSpec((None, k, tn), lambda i, j: (l, 0, j))],
        out_specs=pl.BlockSpec((tm, tn), lambda i, j: (i, j)),
        out_shape=jax.ShapeDtypeStruct((m, n), out_dtype),
        compiler_params=_params("parallel", "arbitrary"),
        name=name,
    )(x, w)


def _mm_acc_kernel(x_ref, w_ref, o_ref, acc_ref):
    kk = pl.program_id(2)
    part = _dot(x_ref[...], w_ref[...])

    @pl.when(kk == 0)
    def _():
        acc_ref[...] = part

    @pl.when((kk > 0) & (kk < pl.num_programs(2) - 1))
    def _():
        acc_ref[...] += part

    @pl.when(kk == pl.num_programs(2) - 1)
    def _():
        o_ref[...] = (acc_ref[...] + part).astype(o_ref.dtype)


def _matmul_ksplit(x, w, l, out_dtype, tm, tn, tk, name):
    m, k = x.shape
    _, _, n = w.shape
    assert k // tk >= 2
    return pl.pallas_call(
        _mm_acc_kernel,
        grid=(m // tm, n // tn, k // tk),
        in_specs=[pl.BlockSpec((tm, tk), lambda i, j, kk: (i, kk)),
                  pl.BlockSpec((None, tk, tn), lambda i, j, kk: (l, kk, j))],
        out_specs=pl.BlockSpec((tm, tn), lambda i, j, kk: (i, j)),
        out_shape=jax.ShapeDtypeStruct((m, n), out_dtype),
        scratch_shapes=[pltpu.VMEM((tm, tn), F32)],
        compiler_params=_params("parallel", "parallel", "arbitrary"),
        name=name,
    )(x, w)


def _cast_pad_cols_kernel(x_ref, o_ref):
    n = x_ref.shape[1]
    o_ref[:, :n] = x_ref[...].astype(BF16)
    if o_ref.shape[1] > n:
        o_ref[:, n:] = jnp.zeros((o_ref.shape[0], o_ref.shape[1] - n), BF16)


def _gate_up_kernel(h_ref, wg_ref, wu_ref, o_ref):
    h = h_ref[...]
    g = _dot(h, wg_ref[...])
    u = _dot(h, wu_ref[...])
    o_ref[...] = (g * jax.nn.sigmoid(g) * u).astype(BF16)


def _gate_up(h, wg, wu, l, tm, tf):
    m, k = h.shape
    _, _, f = wg.shape
    wspec = pl.BlockSpec((None, k, tf), lambda i, j: (l, 0, j))
    return pl.pallas_call(
        _gate_up_kernel,
        grid=(m // tm, f // tf),
        in_specs=[pl.BlockSpec((tm, k), lambda i, j: (i, 0)), wspec, wspec],
        out_specs=pl.BlockSpec((tm, tf), lambda i, j: (i, j)),
        out_shape=jax.ShapeDtypeStruct((m, f), BF16),
        compiler_params=_params("parallel", "arbitrary"),
        name="ffn_gate_up",
    )(h, wg, wu)


def _norm_kernel(with_h, x_ref, y_ref, gate_ref, lg_ref, lb_ref, *rest):
    z = ALPHA * x_ref[...] + gate_ref[...] * y_ref[...].astype(F32)
    mu = jnp.mean(z, axis=-1, keepdims=True)
    zc = z - mu
    var = jnp.mean(zc * zc, axis=-1, keepdims=True)
    xn = zc * lax.rsqrt(var + LN_EPS) * lg_ref[...] + lb_ref[...]
    if with_h:
        sc_ref, sh_ref, xo_ref, h_ref = rest
        xo_ref[...] = xn
        h_ref[...] = (xn * (1.0 + sc_ref[...]) + sh_ref[...]).astype(BF16)
    else:
        (xo_ref,) = rest
        xo_ref[...] = xn


def _residual_norm(x, y, gate, lg, lb, sc=None, sh=None):
    b, s, d = x.shape
    tm = 256
    with_h = sc is not None
    tile = pl.BlockSpec((None, tm, d), lambda i, j: (i, j, 0))
    vec = pl.BlockSpec((None, 1, d), lambda i, j: (i, 0, 0))
    par = pl.BlockSpec((1, d), lambda i, j: (0, 0))
    in_specs = [tile, tile, vec, par, par]
    args = [x, y, gate, lg, lb]
    out_specs = [tile]
    out_shape = [jax.ShapeDtypeStruct((b, s, d), F32)]
    if with_h:
        in_specs += [vec, vec]
        args += [sc, sh]
        out_specs.append(tile)
        out_shape.append(jax.ShapeDtypeStruct((b, s, d), BF16))
    return pl.pallas_call(
        functools.partial(_norm_kernel, with_h),
        grid=(b, s // tm),
        in_specs=in_specs,
        out_specs=out_specs,
        out_shape=out_shape,
        compiler_params=_params("parallel", "parallel"),
        name="residual_norm",
    )(*args)


def _lane_tiles(a):
    return [a[:, i * LANE:(i + 1) * LANE] for i in range(a.shape[1] // LANE)]


def _row_max_rep(s):
    tiles = _lane_tiles(s)
    t = tiles[0]
    for u in tiles[1:]:
        t = jnp.maximum(t, u)
    return jnp.broadcast_to(jnp.max(t, axis=1, keepdims=True), t.shape)


def _probs(s, m_rep):
    return jnp.concatenate([jnp.exp2(u - m_rep) for u in _lane_tiles(s)], axis=1).astype(BF16)


def _softmax_reset(m_ref, acc_ref):
    m_ref[...] = jnp.full(m_ref.shape, NEG, F32)
    acc_ref[...] = jnp.zeros(acc_ref.shape, F32)


def _softmax_update(s, v_aug, m_ref, acc_ref):
    m_old = m_ref[...]
    m = jnp.maximum(m_old, _row_max_rep(s))
    corr = jnp.exp2(m_old - m)
    m_ref[...] = m
    acc_ref[...] = jnp.concatenate([corr, corr], axis=1) * acc_ref[...] + _dot(_probs(s, m), v_aug)


def _augment_values(v_ref, vaug_ref):
    vaug_ref[:, :HEAD_DIM] = v_ref[...]
    vaug_ref[:, HEAD_DIM:] = jnp.ones((v_ref.shape[0], HEAD_DIM), BF16)


def _bias_tiles(tab, bias_ref):
    t = bias_ref.shape[1]
    i = lax.broadcasted_iota(jnp.int32, (t, t), 0)
    j = lax.broadcasted_iota(jnp.int32, (t, t), 1)
    far = tab(N_BUCKETS - 1)
    d0 = (_bias_from_dist(i - j, tab) - far) * LOG2E
    bias_ref[2] = jnp.where(i >= j, d0, NEG)
    bias_ref[1] = (_bias_from_dist(i - j + t, tab) - far) * LOG2E
    bias_ref[0] = jnp.zeros((t, t), F32)


def _causal_tiles(qi, logits_fn, update_fn, sa_ref, sb_ref):
    n = qi + 1
    sa_ref[...] = logits_fn(qi)

    def pair(p, carry):
        left = 2 * p
        kt = qi - left
        sb_ref[...] = logits_fn(jnp.maximum(kt - 1, 0))
        update_fn(sa_ref[...], left, kt)
        sa_ref[...] = logits_fn(jnp.maximum(kt - 2, 0))
        update_fn(sb_ref[...], left + 1, kt - 1)
        return carry

    lax.fori_loop(0, n // 2, pair, 0)

    @pl.when(n % 2 == 1)
    def _():
        update_fn(sa_ref[...], qi, 0)


def _moba_kernel(n_side, tab_ref, q_ref, k_ref, v_ref, *rest):
    side_in, o_ref, side_out = rest[:n_side], rest[n_side], rest[n_side + 1:2 * n_side + 1]
    kmean_ref, bias_ref, vaug_ref, sa_ref, sb_ref, m_ref, acc_ref = rest[2 * n_side + 1:]
    for x_ref, y_ref in zip(side_in, side_out):
        _cast_pad_cols_kernel(x_ref, y_ref)
    h = pl.program_id(0)
    bi = pl.program_id(1)
    qi = pl.program_id(2)
    t = q_ref.shape[0]
    blk = MOBA_BLOCK
    bpt = t // blk
    nb = k_ref.shape[0] // blk
    cs = HEAD_DIM ** -0.5 * LOG2E

    @pl.when((bi == 0) & (qi == 0))
    def _():
        _bias_tiles(lambda k: tab_ref[h, k], bias_ref)

    @pl.when(qi == 0)
    def _():
        for jb in range(nb):
            kb = k_ref[jb * blk:(jb + 1) * blk, :].astype(F32)
            kmean_ref[jb:jb + 1, :] = jnp.sum(kb, axis=0, keepdims=True) * (1.0 / blk)
        _augment_values(v_ref, vaug_ref)

    q = q_ref[...]
    km = kmean_ref[...]
    km_hi = km.astype(BF16)
    km_lo = (km - km_hi.astype(F32)).astype(BF16)
    gate = _dot_nt(km_hi, q) + _dot_nt(km_lo, q)
    jblk = lax.broadcasted_iota(jnp.int32, gate.shape, 0)
    qpos = lax.broadcasted_iota(jnp.int32, (1, t), 1)
    qblk = qi * bpt + qpos // blk
    g = jnp.where(jblk < qblk, gate, -jnp.inf)
    bits = jnp.left_shift(1, qblk)
    for r in range(MOBA_TOPK):
        mx = jnp.max(g, axis=0, keepdims=True)
        first = jnp.min(jnp.where(g == mx, jblk, nb), axis=0, keepdims=True)
        bits = bits | jnp.where(r < qblk, jnp.left_shift(1, first), 0)
        g = jnp.where(jblk == first, -jnp.inf, g)
    bits = jnp.broadcast_to(bits.astype(F32), (LANE, t)).T.astype(jnp.int32)

    def logits(kt):
        return _dot_nt(q, k_ref[pl.ds(pl.multiple_of(kt * t, t), t), :]) * cs

    def update(s, left, kt):
        s = s + bias_ref[jnp.maximum(2 - left, 0)]
        tiles = _lane_tiles(s)
        per_blk = blk // LANE
        out = []
        for c in range(bpt):
            vis = (jnp.right_shift(bits, kt * bpt + c) & 1) == 1
            out += [jnp.where(vis, u, NEG) for u in tiles[c * per_blk:(c + 1) * per_blk]]
        s = jnp.concatenate(out, axis=1)
        _softmax_update(s, vaug_ref[pl.ds(pl.multiple_of(kt * t, t), t), :], m_ref, acc_ref)

    _softmax_reset(m_ref, acc_ref)
    _causal_tiles(qi, logits, update, sa_ref, sb_ref)
    acc = acc_ref[...]
    o_ref[...] = (acc[:, :HEAD_DIM] / acc[:, HEAD_DIM:]).astype(o_ref.dtype)


def _row_slab_specs(w, n_steps, step_fn, n_pad):
    rows, n = w.shape
    assert rows % n_steps == 0 and (rows // n_steps) % 16 == 0
    tr = rows // n_steps
    return (pl.BlockSpec((tr, n), lambda *g: (step_fn(*g), 0)),
            pl.BlockSpec((tr, n_pad), lambda *g: (step_fn(*g), 0)),
            jax.ShapeDtypeStruct((rows, n_pad), BF16))


def _moba(proj, tab_a, side=()):
    b, s, _ = proj.shape
    t = ATTN_TILE
    nq = s // t
    assert s % t == 0 and t % MOBA_BLOCK == 0 and t >= FAR_DIST and s // MOBA_BLOCK < 24
    qo, ko, vo = (PROJ_OFFS[i] // HEAD_DIM for i in (0, 1, 2))
    side_specs = [_row_slab_specs(w, H_A * b * nq, lambda h, bi, qi: (h * b + bi) * nq + qi, D_FF_PAD) for w in side]
    return pl.pallas_call(
        functools.partial(_moba_kernel, len(side)),
        grid=(H_A, b, nq),
        in_specs=[pl.BlockSpec(memory_space=pltpu.SMEM),
                  pl.BlockSpec((None, t, HEAD_DIM), lambda h, bi, qi: (bi, qi, qo + h)),
                  pl.BlockSpec((None, s, HEAD_DIM), lambda h, bi, qi: (bi, 0, ko + h)),
                  pl.BlockSpec((None, s, HEAD_DIM), lambda h, bi, qi: (bi, 0, vo + h))]
                 + [sp[0] for sp in side_specs],
        out_specs=[pl.BlockSpec((None, t, HEAD_DIM), lambda h, bi, qi: (bi, qi, h))] + [sp[1] for sp in side_specs],
        out_shape=[jax.ShapeDtypeStruct((b, s, D_MODEL), BF16)]
                  + [sp[2] for sp in side_specs],
        scratch_shapes=[pltpu.VMEM((s // MOBA_BLOCK, HEAD_DIM), F32),
                        pltpu.VMEM((3, t, t), F32),
                        pltpu.VMEM((s, 2 * HEAD_DIM), BF16),
                        pltpu.VMEM((t, t), F32), pltpu.VMEM((t, t), F32),
                        pltpu.VMEM((t, LANE), F32),
                        pltpu.VMEM((t, 2 * HEAD_DIM), F32)],
        compiler_params=_params("arbitrary", "arbitrary", "arbitrary"),
        name="moba_attention",
    )(tab_a, proj, proj, proj, *side)


def _swa_kernel(n_side, tab_ref, sink_ref, q_ref, k_ref, v_ref, cat_ref, *rest):
    del cat_ref
    side_in, o_ref, side_out = rest[:n_side], rest[n_side], rest[n_side + 1:2 * n_side + 1]
    bias_ref, sink_rep_ref, vaug_ref = rest[2 * n_side + 1:]
    for x_ref, y_ref in zip(side_in, side_out):
        _cast_pad_cols_kernel(x_ref, y_ref)
    g = pl.program_id(1)
    ti = pl.program_id(2)
    w = WINDOW
    nsub = q_ref.shape[0] // w
    cs = HEAD_DIM ** -0.5 * LOG2E

    @pl.when(ti == 0)
    def _():
        i = lax.broadcasted_iota(jnp.int32, (w, 2 * w), 0)
        j = lax.broadcasted_iota(jnp.int32, (w, 2 * w), 1)
        dist = i + w - j
        band = (dist >= 0) & (dist < w)
        for hh in range(G_B):
            head = g * G_B + hh
            bias = _bias_from_dist(dist, lambda k: tab_ref[head, k]) * LOG2E
            bias_ref[hh * w:(hh + 1) * w, :] = jnp.where(band, bias, NEG)
            sink_rep_ref[hh * w:(hh + 1) * w, :] = jnp.full((w, LANE), sink_ref[head] * LOG2E, F32)
        _augment_values(v_ref, vaug_ref)

    def step(n, first_block):
        blk = ti * nsub + n
        r0 = pl.multiple_of(n * w, w)
        k0 = pl.multiple_of(jnp.maximum(blk - 1, 0) * w, w)
        k1 = pl.multiple_of(blk * w, w)
        qs = jnp.concatenate([q_ref[pl.ds(r0, w), hh * HEAD_DIM:(hh + 1) * HEAD_DIM] for hh in range(G_B)], axis=0)
        kband = jnp.concatenate([k_ref[pl.ds(k0, w), :], k_ref[pl.ds(k1, w), :]], axis=0)
        vband = jnp.concatenate([vaug_ref[pl.ds(k0, w), :], vaug_ref[pl.ds(k1, w), :]], axis=0)
        bias = bias_ref[...]
        if first_block:
            jj = lax.broadcasted_iota(jnp.int32, bias.shape, 1)
            bias = jnp.where(jj >= w, bias, NEG)
        s = _dot_nt(qs, kband) * cs + bias
        sink = sink_rep_ref[...]
        m = jnp.maximum(_row_max_rep(s), sink)
        acc = _dot(_probs(s, m), vband)
        o = acc[:, :HEAD_DIM] / (acc[:, HEAD_DIM:] + jnp.exp2(sink - m))
        for hh in range(G_B):
            o_ref[pl.ds(r0, w), hh * HEAD_DIM:(hh + 1) * HEAD_DIM] = o[hh * w:(hh + 1) * w, :].astype(o_ref.dtype)

    @pl.when(ti == 0)
    def _():
        step(0, True)

    def body(n, carry):
        step(n, False)
        return carry

    lax.fori_loop(jnp.where(ti == 0, 1, 0), nsub, body, 0)


def _swa(proj, tab_b, sinks, cat, side=()):
    b, s, _ = proj.shape
    tq = 1024
    nt = s // tq
    gw = G_B * HEAD_DIM
    qo = PROJ_OFFS[3] // gw
    oo = QA_W // gw
    ko, vo = (PROJ_OFFS[i] // HEAD_DIM for i in (4, 5))
    assert PROJ_OFFS[3] % gw == 0 and QA_W % gw == 0 and s % tq == 0
    smem = pl.BlockSpec(memory_space=pltpu.SMEM)
    side_specs = [_row_slab_specs(w, b * KV_B * nt, lambda bi, g, ti: (bi * KV_B + g) * nt + ti, w.shape[1])
                  for w in side]
    return pl.pallas_call(
        functools.partial(_swa_kernel, len(side)),
        grid=(b, KV_B, nt),
        in_specs=[smem, smem,
                  pl.BlockSpec((None, tq, gw), lambda bi, g, ti: (bi, ti, qo + g)),
                  pl.BlockSpec((None, s, HEAD_DIM), lambda bi, g, ti: (bi, 0, ko + g)),
                  pl.BlockSpec((None, s, HEAD_DIM), lambda bi, g, ti: (bi, 0, vo + g)),
                  pl.BlockSpec(memory_space=pl.ANY)] + [sp[0] for sp in side_specs],
        out_specs=[pl.BlockSpec((None, tq, gw), lambda bi, g, ti: (bi, ti, oo + g))] + [sp[1] for sp in side_specs],
        out_shape=[jax.ShapeDtypeStruct(cat.shape, cat.dtype)] + [sp[2] for sp in side_specs],
        input_output_aliases={5: 0},
        scratch_shapes=[pltpu.VMEM((G_B * WINDOW, 2 * WINDOW), F32),
                        pltpu.VMEM((G_B * WINDOW, LANE), F32),
                        pltpu.VMEM((s, 2 * HEAD_DIM), BF16)],
        compiler_params=_params("parallel", "parallel", "arbitrary"),
        name="swa_attention",
    )(tab_b, sinks, proj, proj, proj, cat, *side)


def _diff_kernel(lam_init, row_tiles, tab_ref, q_ref, k_ref, v_ref, lamp_ref, subg_ref, cat_ref, *rest):
    del cat_ref
    n_side = 0 if row_tiles is None else 1
    side_in, o_ref, side_out = rest[:n_side], rest[n_side], rest[n_side + 1:2 * n_side + 1]
    bias_ref, vaug_ref, sa_ref, sb_ref, m_ref, acc_ref = rest[2 * n_side + 1:]
    h = pl.program_id(0)
    bi = pl.program_id(1)
    qi = pl.program_id(2)
    t = q_ref.shape[0]
    cs = DIFF_DIM ** -0.5 * LOG2E

    if row_tiles is not None:
        n_src, n_dst, depth = row_tiles
        step = (h * pl.num_programs(1) + bi) * pl.num_programs(2) + qi

        @pl.when((step < depth * n_dst) & (step % n_dst < n_src))
        def _():
            side_out[0][...] = side_in[0][...].astype(BF16)

        @pl.when((step < depth * n_dst) & (step % n_dst >= n_src))
        def _():
            side_out[0][...] = jnp.zeros(side_out[0].shape, BF16)

    @pl.when((bi == 0) & (qi == 0))
    def _():
        _bias_tiles(lambda k: tab_ref[h, k], bias_ref)

    @pl.when(qi == 0)
    def _():
        _augment_values(v_ref, vaug_ref)

    q = q_ref[...]
    lane = lax.broadcasted_iota(jnp.int32, q.shape, 1)
    zero = jnp.zeros_like(q)
    q2 = jnp.concatenate([jnp.where(lane < DIFF_DIM, q, zero), jnp.where(lane >= DIFF_DIM, q, zero)], axis=0)

    def logits(kt):
        return _dot_nt(q2, k_ref[pl.ds(pl.multiple_of(kt * t, t), t), :]) * cs

    def update(s, left, kt):
        d = bias_ref[jnp.maximum(2 - left, 0)]
        s = jnp.concatenate([s[:t] + d, s[t:] + d], axis=0)
        _softmax_update(s, vaug_ref[pl.ds(pl.multiple_of(kt * t, t), t), :], m_ref, acc_ref)

    _softmax_reset(m_ref, acc_ref)
    _causal_tiles(qi, logits, update, sa_ref, sb_ref)

    lp = lamp_ref[...]
    lam = (jnp.exp(jnp.sum(lp[0:1] * lp[1:2], axis=1, keepdims=True))
           - jnp.exp(jnp.sum(lp[2:3] * lp[3:4], axis=1, keepdims=True)) + lam_init)
    acc = acc_ref[...]
    a = acc[:, :HEAD_DIM] / acc[:, HEAD_DIM:]
    o = a[:t] - lam * a[t:]
    o = o * lax.rsqrt(jnp.mean(o * o, axis=-1, keepdims=True) + LN_EPS) * subg_ref[...]
    o_ref[...] = (o * (1.0 - lam_init)).astype(o_ref.dtype)


def _diff(proj, tab_c, lam_p, subln_g, lam_init, cat, pad_rows=None):
    b, s, _ = proj.shape
    t = ATTN_TILE
    nq = s // t
    assert s % t == 0 and t >= FAR_DIST
    qo, ko, vo = (PROJ_OFFS[i] // HEAD_DIM for i in (6, 7, 8))
    oo = (QA_W + QB_W) // HEAD_DIM
    side, side_in, side_out, side_shape, row_tiles = (), [], [], [], None
    if pad_rows is not None:
        w, k_pad = pad_rows
        depth, k, n = w.shape
        tr = 256
        assert k % tr == 0 and k_pad % tr == 0
        n_src, n_dst = k // tr, k_pad // tr
        assert depth * n_dst <= H_C * b * nq
        row_tiles = (n_src, n_dst, depth)

        def dst_tile(h, bi, qi):
            return jnp.minimum((h * b + bi) * nq + qi, depth * n_dst - 1)

        def src_tile(h, bi, qi):
            d = dst_tile(h, bi, qi)
            return (d // n_dst) * n_src + jnp.minimum(d % n_dst, n_src - 1)

        side = (w.reshape(depth * k, n),)
        side_in = [pl.BlockSpec((tr, n), lambda h, bi, qi: (src_tile(h, bi, qi), 0))]
        side_out = [pl.BlockSpec((tr, n), lambda h, bi, qi: (dst_tile(h, bi, qi), 0))]
        side_shape = [jax.ShapeDtypeStruct((depth * k_pad, n), BF16)]
    return pl.pallas_call(
        functools.partial(_diff_kernel, lam_init, row_tiles),
        grid=(H_C, b, nq),
        in_specs=[pl.BlockSpec(memory_space=pltpu.SMEM),
                  pl.BlockSpec((None, t, HEAD_DIM), lambda h, bi, qi: (bi, qi, qo + h)),
                  pl.BlockSpec((None, s, HEAD_DIM), lambda h, bi, qi: (bi, 0, ko + h)),
                  pl.BlockSpec((None, s, HEAD_DIM), lambda h, bi, qi: (bi, 0, vo + h)),
                  pl.BlockSpec((4, DIFF_DIM), lambda h, bi, qi: (0, 0)),
                  pl.BlockSpec((1, HEAD_DIM), lambda h, bi, qi: (0, 0)),
                  pl.BlockSpec(memory_space=pl.ANY)] + side_in,
        out_specs=[pl.BlockSpec((None, t, HEAD_DIM), lambda h, bi, qi: (bi, qi, oo + h))] + side_out,
        out_shape=[jax.ShapeDtypeStruct(cat.shape, cat.dtype)] + side_shape,
        input_output_aliases={6: 0},
        scratch_shapes=[pltpu.VMEM((3, t, t), F32),
                        pltpu.VMEM((s, 2 * HEAD_DIM), BF16),
                        pltpu.VMEM((2 * t, t), F32), pltpu.VMEM((2 * t, t), F32),
                        pltpu.VMEM((2 * t, LANE), F32),
                        pltpu.VMEM((2 * t, 2 * HEAD_DIM), F32)],
        compiler_params=_params("arbitrary", "arbitrary", "arbitrary"),
        name="diff_attention",
    )(tab_c, proj, proj, proj, lam_p, subln_g, cat, *side)


def kernel(x, c, rel_bias, w_ada, b_ada, w_in, w_o, attn_sinks, diff_lambda, diff_subln_g,
           ln_g, ln_b, w_gate, w_up, w_down):
    b, s, d = x.shape
    m = b * s
    tab = rel_bias.T
    tab_a, tab_b, tab_c = tab[:H_A], tab[H_A:H_A + H_B], tab[H_A + H_B:]

    c8 = jnp.pad(c, ((0, 8 - b), (0, 0)))
    mod = _ada_mod(c8, w_ada, b_ada[:, None, :])[:, :b]
    mod = mod.reshape(DEPTH, b, 6, 1, d)

    w_in_b = [w_in[:1].astype(BF16), None]
    h = None
    for l in range(DEPTH):
        sh1, sc1, g1, sh2, sc2, g2 = (mod[l, :, i] for i in range(6))
        if h is None:
            h = _modulate(x, sc1, sh1)
        proj = _matmul(h.reshape(m, d), w_in_b[l], 0, BF16, 1024, 512, "proj_in").reshape(b, s, D_PROJ)
        lam_init = 0.8 - 0.6 * math.exp(-0.3 * l)
        if l == 0:
            assert DEPTH == 2
            cat, wg_b, wu_b = _moba(proj, tab_a, (w_gate.reshape(DEPTH * d, D_FF), w_up.reshape(DEPTH * d, D_FF)))
            cat, w_o_b, w_in1_b = _swa(proj, tab_b, attn_sinks[l], cat, (w_o.reshape(DEPTH * d, d), w_in[1]))
            cat, wd_b = _diff(proj, tab_c, diff_lambda[l], diff_subln_g[l][None, :], lam_init, cat,
                              (w_down, D_FF_PAD))
            wg_b, wu_b = wg_b.reshape(DEPTH, d, D_FF_PAD), wu_b.reshape(DEPTH, d, D_FF_PAD)
            w_o_b, wd_b = w_o_b.reshape(DEPTH, d, d), wd_b.reshape(DEPTH, D_FF_PAD, d)
            w_in_b[1] = w_in1_b[None]
        else:
            (cat,) = _moba(proj, tab_a)
            (cat,) = _swa(proj, tab_b, attn_sinks[l], cat)
            (cat,) = _diff(proj, tab_c, diff_lambda[l], diff_subln_g[l][None, :], lam_init, cat)
        mix = _matmul(cat.reshape(m, d), w_o_b, l, BF16, 1024, 1024, "proj_out").reshape(b, s, d)
        x, h = _residual_norm(x, mix, g1, ln_g[l, 0][None, :], ln_b[l, 0][None, :], sc2, sh2)

        act = _gate_up(h.reshape(m, d), wg_b, wu_b, l, 1024, 512)
        ff = _matmul_ksplit(act, wd_b, l, BF16, 1024, 1024, D_FF_PAD // 2, "ffn_down").reshape(b, s, d)
        if l + 1 < DEPTH:
            nsh, nsc = mod[l + 1, :, 0], mod[l + 1, :, 1]
            x, h = _residual_norm(x, ff, g2, ln_g[l, 1][None, :], ln_b[l, 1][None, :], nsc, nsh)
        else:
            (x,) = _residual_norm(x, ff, g2, ln_g[l, 1][None, :], ln_b[l, 1][None, :])
    return x
```

```python
import functools
import math

import numpy as np
import jax
import jax.numpy as jnp
from jax import lax
from jax.experimental import pallas as pl
from jax.experimental.pallas import tpu as pltpu

D_MODEL = 4096
DEPTH = 2
HEAD_DIM = 128
N_HEADS = D_MODEL // HEAD_DIM
H_A = N_HEADS // 4
H_C = N_HEADS // 4
H_B = N_HEADS - H_A - H_C
KV_B = max(1, H_B // 8)
G_B = H_B // KV_B
DIFF_DIM = HEAD_DIM // 2
MOBA_BLOCK = 256
MOBA_TOPK = 3
WINDOW = 128
N_BUCKETS = 32
MAX_DISTANCE = 128
D_FF = -(-8 * D_MODEL // (3 * 256)) * 256
ALPHA = (2.0 * DEPTH) ** 0.25
LN_EPS = 1e-5
NEG = -1e30
LOG2E = math.log2(math.e)

QA_W = H_A * HEAD_DIM
QB_W = H_B * HEAD_DIM
KB_W = KV_B * HEAD_DIM
QC_W = H_C * 2 * DIFF_DIM
VC_W = H_C * HEAD_DIM
PROJ_SIZES = (QA_W, QA_W, QA_W, QB_W, KB_W, KB_W, QC_W, QC_W, VC_W)
PROJ_OFFS = tuple(int(sum(PROJ_SIZES[:i])) for i in range(len(PROJ_SIZES)))
D_PROJ = sum(PROJ_SIZES)

LANE = 128
ATTN_TILE = 512
CAST_ROW_TILE = 256
D_FF_PAD = -(-D_FF // 1024) * 1024
VMEM_LIMIT = 56 * 1024 * 1024

F32 = jnp.float32
BF16 = jnp.bfloat16


def _bucket_upper_bounds():
    n = np.arange(0, 4 * MAX_DISTANCE, dtype=np.int64)
    max_exact = N_BUCKETS // 2
    nf = np.maximum(n, 1).astype(np.float32)
    large = max_exact + (np.log(nf / np.float32(max_exact)) / np.float32(math.log(MAX_DISTANCE / max_exact))
                         * np.float32(N_BUCKETS - max_exact)).astype(np.int32)
    large = np.minimum(large, N_BUCKETS - 1)
    bucket = np.where(n < max_exact, n, large)
    assert bucket[-1] == N_BUCKETS - 1 and np.all(np.diff(bucket) >= 0)
    return tuple(int(np.argmax(bucket > k)) for k in range(N_BUCKETS - 1))


BUCKET_UPPER = _bucket_upper_bounds()
FAR_DIST = BUCKET_UPPER[-1]


def _bias_from_dist(dist, tab):
    b = jnp.full(dist.shape, tab(N_BUCKETS - 1), F32)
    for k in range(N_BUCKETS - 2, -1, -1):
        b = jnp.where(dist < BUCKET_UPPER[k], tab(k), b)
    return b


def _dot(a, b):
    return jnp.dot(a, b, preferred_element_type=F32)


def _dot_nt(a, b):
    return lax.dot_general(a, b, (((1,), (1,)), ((), ())), preferred_element_type=F32)


def _params(*sem):
    return pltpu.CompilerParams(dimension_semantics=sem, vmem_limit_bytes=VMEM_LIMIT)


def _mod_kernel(c_ref, w_ref, b_ref, o_ref):
    c = c_ref[...]
    s = (c * jax.nn.sigmoid(c)).astype(BF16)
    o_ref[...] = _dot(s, w_ref[...].astype(BF16)) + b_ref[...]


def _ada_mod(c8, w_ada, b_ada3):
    depth, d, n = w_ada.shape
    tn = 512
    return pl.pallas_call(
        _mod_kernel,
        grid=(depth, n // tn),
        in_specs=[pl.BlockSpec((8, d), lambda l, j: (0, 0)),
                  pl.BlockSpec((None, d, tn), lambda l, j: (l, 0, j)),
                  pl.BlockSpec((None, 1, tn), lambda l, j: (l, 0, j))],
        out_specs=pl.BlockSpec((None, 8, tn), lambda l, j: (l, 0, j)),
        out_shape=jax.ShapeDtypeStruct((depth, 8, n), F32),
        compiler_params=_params("parallel", "parallel"),
        name="ada_mod",
    )(c8, w_ada, b_ada3)


def _mm_kernel(x_ref, w_ref, o_ref):
    o_ref[...] = _dot(x_ref[...], w_ref[...]).astype(o_ref.dtype)


def _matmul(x, w, l, out_dtype, tm, tn, name):
    m, k = x.shape
    _, _, n = w.shape
    return pl.pallas_call(
        _mm_kernel,
        grid=(m // tm, n // tn),
        in_specs=[pl.BlockSpec((tm, k), lambda i, j: (i, 0)),
                  pl.BlockSpec((None, k, tn), lambda i, j: (l, 0, j))],
        out_specs=pl.BlockSpec((tm, tn), lambda i, j: (i, j)),
        out_shape=jax.ShapeDtypeStruct((m, n), out_dtype),
        compiler_params=_params("parallel", "arbitrary"),
        name=name,
    )(x, w)


def _mod_mm_kernel(x_ref, sc_ref, sh_ref, w_ref, o_ref, h_ref):
    @pl.when(pl.program_id(1) == 0)
    def _():
        h_ref[...] = (x_ref[...] * (1.0 + sc_ref[...]) + sh_ref[...]).astype(BF16)

    o_ref[...] = _dot(h_ref[...], w_ref[...]).astype(o_ref.dtype)


def _modulate_matmul(x, sc, sh, w, l, out_dtype, tm, tn, name):
    b, s, k = x.shape
    _, _, n = w.shape
    assert s % tm == 0
    per_b = s // tm
    vec = pl.BlockSpec((None, 1, k), lambda i, j: (i // per_b, 0, 0))
    return pl.pallas_call(
        _mod_mm_kernel,
        grid=(b * per_b, n // tn),
        in_specs=[pl.BlockSpec((tm, k), lambda i, j: (i, 0)), vec, vec,
                  pl.BlockSpec((None, k, tn), lambda i, j: (l, 0, j))],
        out_specs=pl.BlockSpec((tm, tn), lambda i, j: (i, j)),
        out_shape=jax.ShapeDtypeStruct((b * s, n), out_dtype),
        scratch_shapes=[pltpu.VMEM((tm, k), BF16)],
        compiler_params=_params("parallel", "arbitrary"),
        name=name,
    )(x.reshape(b * s, k), sc, sh, w)


def _mm_acc_kernel(x_ref, w_ref, o_ref, acc_ref):
    kk = pl.program_id(2)
    part = _dot(x_ref[...], w_ref[...])

    @pl.when(kk == 0)
    def _():
        acc_ref[...] = part

    @pl.when((kk > 0) & (kk < pl.num_programs(2) - 1))
    def _():
        acc_ref[...] += part

    @pl.when(kk == pl.num_programs(2) - 1)
    def _():
        o_ref[...] = (acc_ref[...] + part).astype(o_ref.dtype)


def _matmul_ksplit(x, w, l, out_dtype, tm, tn, tk, name):
    m, k = x.shape
    _, _, n = w.shape
    assert k // tk >= 2
    return pl.pallas_call(
        _mm_acc_kernel,
        grid=(m // tm, n // tn, k // tk),
        in_specs=[pl.BlockSpec((tm, tk), lambda i, j, kk: (i, kk)),
                  pl.BlockSpec((None, tk, tn), lambda i, j, kk: (l, kk, j))],
        out_specs=pl.BlockSpec((tm, tn), lambda i, j, kk: (i, j)),
        out_shape=jax.ShapeDtypeStruct((m, n), out_dtype),
        scratch_shapes=[pltpu.VMEM((tm, tn), F32)],
        compiler_params=_params("parallel", "parallel", "arbitrary"),
        name=name,
    )(x, w)


def _cast_pad_cols_kernel(x_ref, o_ref):
    n = x_ref.shape[1]
    o_ref[:, :n] = x_ref[...].astype(BF16)
    if o_ref.shape[1] > n:
        o_ref[:, n:] = jnp.zeros((o_ref.shape[0], o_ref.shape[1] - n), BF16)


def _gate_up_kernel(h_ref, wg_ref, wu_ref, o_ref):
    h = h_ref[...]
    g = _dot(h, wg_ref[...])
    u = _dot(h, wu_ref[...])
    o_ref[...] = (g * jax.nn.sigmoid(g) * u).astype(BF16)


def _gate_up(h, wg, wu, l, tm, tf):
    m, k = h.shape
    _, _, f = wg.shape
    wspec = pl.BlockSpec((None, k, tf), lambda i, j: (l, 0, j))
    return pl.pallas_call(
        _gate_up_kernel,
        grid=(m // tm, f // tf),
        in_specs=[pl.BlockSpec((tm, k), lambda i, j: (i, 0)), wspec, wspec],
        out_specs=pl.BlockSpec((tm, tf), lambda i, j: (i, j)),
        out_shape=jax.ShapeDtypeStruct((m, f), BF16),
        compiler_params=_params("parallel", "arbitrary"),
        name="ffn_gate_up",
    )(h, wg, wu)


def _norm_kernel(with_h, x_ref, y_ref, gate_ref, lg_ref, lb_ref, *rest):
    z = ALPHA * x_ref[...] + gate_ref[...] * y_ref[...].astype(F32)
    mu = jnp.mean(z, axis=-1, keepdims=True)
    zc = z - mu
    var = jnp.mean(zc * zc, axis=-1, keepdims=True)
    xn = zc * lax.rsqrt(var + LN_EPS) * lg_ref[...] + lb_ref[...]
    if with_h:
        sc_ref, sh_ref, xo_ref, h_ref = rest
        xo_ref[...] = xn
        h_ref[...] = (xn * (1.0 + sc_ref[...]) + sh_ref[...]).astype(BF16)
    else:
        (xo_ref,) = rest
        xo_ref[...] = xn


def _residual_norm(x, y, gate, lg, lb, sc=None, sh=None):
    b, s, d = x.shape
    tm = 256
    with_h = sc is not None
    tile = pl.BlockSpec((None, tm, d), lambda i, j: (i, j, 0))
    vec = pl.BlockSpec((None, 1, d), lambda i, j: (i, 0, 0))
    par = pl.BlockSpec((1, d), lambda i, j: (0, 0))
    in_specs = [tile, tile, vec, par, par]
    args = [x, y, gate, lg, lb]
    out_specs = [tile]
    out_shape = [jax.ShapeDtypeStruct((b, s, d), F32)]
    if with_h:
        in_specs += [vec, vec]
        args += [sc, sh]
        out_specs.append(tile)
        out_shape.append(jax.ShapeDtypeStruct((b, s, d), BF16))
    return pl.pallas_call(
        functools.partial(_norm_kernel, with_h),
        grid=(b, s // tm),
        in_specs=in_specs,
        out_specs=out_specs,
        out_shape=out_shape,
        compiler_params=_params("parallel", "parallel"),
        name="residual_norm",
    )(*args)


def _lane_tiles(a):
    return [a[:, i * LANE:(i + 1) * LANE] for i in range(a.shape[1] // LANE)]


def _row_max_rep(s):
    tiles = _lane_tiles(s)
    t = tiles[0]
    for u in tiles[1:]:
        t = jnp.maximum(t, u)
    return jnp.broadcast_to(jnp.max(t, axis=1, keepdims=True), t.shape)


def _probs(s, m_rep):
    return jnp.concatenate([jnp.exp2(u - m_rep) for u in _lane_tiles(s)], axis=1).astype(BF16)


def _softmax_reset(m_ref, acc_ref):
    m_ref[...] = jnp.full(m_ref.shape, NEG, F32)
    acc_ref[...] = jnp.zeros(acc_ref.shape, F32)


def _softmax_update(s, v_aug, m_ref, acc_ref):
    m_old = m_ref[...]
    m = jnp.maximum(m_old, _row_max_rep(s))
    corr = jnp.exp2(m_old - m)
    m_ref[...] = m
    acc_ref[...] = jnp.concatenate([corr, corr], axis=1) * acc_ref[...] + _dot(_probs(s, m), v_aug)


def _augment_values(v_ref, vaug_ref):
    vaug_ref[:, :HEAD_DIM] = v_ref[...]
    vaug_ref[:, HEAD_DIM:] = jnp.ones((v_ref.shape[0], HEAD_DIM), BF16)


def _bias_tiles(tab, bias_ref):
    t = bias_ref.shape[1]
    i = lax.broadcasted_iota(jnp.int32, (t, t), 0)
    j = lax.broadcasted_iota(jnp.int32, (t, t), 1)
    far = tab(N_BUCKETS - 1)
    d0 = (_bias_from_dist(i - j, tab) - far) * LOG2E
    bias_ref[2] = jnp.where(i >= j, d0, NEG)
    bias_ref[1] = (_bias_from_dist(i - j + t, tab) - far) * LOG2E
    bias_ref[0] = jnp.zeros((t, t), F32)


def _causal_tiles(qi, logits_fn, update_fn, sa_ref, sb_ref):
    n = qi + 1
    sa_ref[...] = logits_fn(qi)

    def pair(p, carry):
        left = 2 * p
        kt = qi - left
        sb_ref[...] = logits_fn(jnp.maximum(kt - 1, 0))
        update_fn(sa_ref[...], left, kt)
        sa_ref[...] = logits_fn(jnp.maximum(kt - 2, 0))
        update_fn(sb_ref[...], left + 1, kt - 1)
        return carry

    lax.fori_loop(0, n // 2, pair, 0)

    @pl.when(n % 2 == 1)
    def _():
        update_fn(sa_ref[...], qi, 0)


def _moba_kernel(n_side, tab_ref, q_ref, k_ref, v_ref, *rest):
    side_in, o_ref, side_out = rest[:n_side], rest[n_side], rest[n_side + 1:2 * n_side + 1]
    kmean_ref, bias_ref, vaug_ref, sa_ref, sb_ref, m_ref, acc_ref = rest[2 * n_side + 1:]
    for x_ref, y_ref in zip(side_in, side_out):
        _cast_pad_cols_kernel(x_ref, y_ref)
    h = pl.program_id(0)
    bi = pl.program_id(1)
    qi = pl.program_id(2)
    t = q_ref.shape[0]
    blk = MOBA_BLOCK
    bpt = t // blk
    nb = k_ref.shape[0] // blk
    cs = HEAD_DIM ** -0.5 * LOG2E

    @pl.when((bi == 0) & (qi == 0))
    def _():
        _bias_tiles(lambda k: tab_ref[h, k], bias_ref)

    @pl.when(qi == 0)
    def _():
        for jb in range(nb):
            kb = k_ref[jb * blk:(jb + 1) * blk, :].astype(F32)
            kmean_ref[jb:jb + 1, :] = jnp.sum(kb, axis=0, keepdims=True) * (1.0 / blk)
        _augment_values(v_ref, vaug_ref)

    q = q_ref[...]
    km = kmean_ref[...]
    km_hi = km.astype(BF16)
    km_lo = (km - km_hi.astype(F32)).astype(BF16)
    gate = _dot_nt(km_hi, q) + _dot_nt(km_lo, q)
    jblk = lax.broadcasted_iota(jnp.int32, gate.shape, 0)
    qpos = lax.broadcasted_iota(jnp.int32, (1, t), 1)
    qblk = qi * bpt + qpos // blk
    g = jnp.where(jblk < qblk, gate, -jnp.inf)
    bits = jnp.left_shift(1, qblk)
    for r in range(MOBA_TOPK):
        mx = jnp.max(g, axis=0, keepdims=True)
        first = jnp.min(jnp.where(g == mx, jblk, nb), axis=0, keepdims=True)
        bits = bits | jnp.where(r < qblk, jnp.left_shift(1, first), 0)
        g = jnp.where(jblk == first, -jnp.inf, g)
    bits = jnp.broadcast_to(bits.astype(F32), (LANE, t)).T.astype(jnp.int32)

    def logits(kt):
        return _dot_nt(q, k_ref[pl.ds(pl.multiple_of(kt * t, t), t), :]) * cs

    def update(s, left, kt):
        s = s + bias_ref[jnp.maximum(2 - left, 0)]
        tiles = _lane_tiles(s)
        per_blk = blk // LANE
        out = []
        for c in range(bpt):
            vis = (jnp.right_shift(bits, kt * bpt + c) & 1) == 1
            out += [jnp.where(vis, u, NEG) for u in tiles[c * per_blk:(c + 1) * per_blk]]
        s = jnp.concatenate(out, axis=1)
        _softmax_update(s, vaug_ref[pl.ds(pl.multiple_of(kt * t, t), t), :], m_ref, acc_ref)

    _softmax_reset(m_ref, acc_ref)
    _causal_tiles(qi, logits, update, sa_ref, sb_ref)
    acc = acc_ref[...]
    o_ref[...] = (acc[:, :HEAD_DIM] / acc[:, HEAD_DIM:]).astype(o_ref.dtype)


def _row_slab_specs(w, n_steps, step_fn, n_pad, row0=0):
    rows, n = w.shape[0] - row0, w.shape[1]
    assert rows % n_steps == 0 and (rows // n_steps) % 16 == 0 and row0 % (rows // n_steps) == 0
    tr = rows // n_steps
    first = row0 // tr
    return (pl.BlockSpec((tr, n), lambda *g: (first + step_fn(*g), 0)),
            pl.BlockSpec((tr, n_pad), lambda *g: (step_fn(*g), 0)),
            jax.ShapeDtypeStruct((rows, n_pad), BF16))


def _moba(proj, tab_a, side=()):
    b, s, _ = proj.shape
    t = ATTN_TILE
    nq = s // t
    assert s % t == 0 and t % MOBA_BLOCK == 0 and t >= FAR_DIST and s // MOBA_BLOCK < 24
    qo, ko, vo = (PROJ_OFFS[i] // HEAD_DIM for i in (0, 1, 2))
    side_specs = [_row_slab_specs(w, H_A * b * nq, lambda h, bi, qi: (h * b + bi) * nq + qi, D_FF_PAD) for w in side]
    return pl.pallas_call(
        functools.partial(_moba_kernel, len(side)),
        grid=(H_A, b, nq),
        in_specs=[pl.BlockSpec(memory_space=pltpu.SMEM),
                  pl.BlockSpec((None, t, HEAD_DIM), lambda h, bi, qi: (bi, qi, qo + h)),
                  pl.BlockSpec((None, s, HEAD_DIM), lambda h, bi, qi: (bi, 0, ko + h)),
                  pl.BlockSpec((None, s, HEAD_DIM), lambda h, bi, qi: (bi, 0, vo + h))]
                 + [sp[0] for sp in side_specs],
        out_specs=[pl.BlockSpec((None, t, HEAD_DIM), lambda h, bi, qi: (bi, qi, h))] + [sp[1] for sp in side_specs],
        out_shape=[jax.ShapeDtypeStruct((b, s, D_MODEL), BF16)]
                  + [sp[2] for sp in side_specs],
        scratch_shapes=[pltpu.VMEM((s // MOBA_BLOCK, HEAD_DIM), F32),
                        pltpu.VMEM((3, t, t), F32),
                        pltpu.VMEM((s, 2 * HEAD_DIM), BF16),
                        pltpu.VMEM((t, t), F32), pltpu.VMEM((t, t), F32),
                        pltpu.VMEM((t, LANE), F32),
                        pltpu.VMEM((t, 2 * HEAD_DIM), F32)],
        compiler_params=_params("arbitrary", "arbitrary", "arbitrary"),
        name="moba_attention",
    )(tab_a, proj, proj, proj, *side)


def _swa_kernel(n_side, tab_ref, sink_ref, q_ref, k_ref, v_ref, cat_ref, *rest):
    del cat_ref
    side_in, o_ref, side_out = rest[:n_side], rest[n_side], rest[n_side + 1:2 * n_side + 1]
    bias_ref, sink_rep_ref, vaug_ref = rest[2 * n_side + 1:]
    for x_ref, y_ref in zip(side_in, side_out):
        _cast_pad_cols_kernel(x_ref, y_ref)
    g = pl.program_id(1)
    ti = pl.program_id(2)
    w = WINDOW
    nsub = q_ref.shape[0] // w
    cs = HEAD_DIM ** -0.5 * LOG2E

    @pl.when(ti == 0)
    def _():
        i = lax.broadcasted_iota(jnp.int32, (w, 2 * w), 0)
        j = lax.broadcasted_iota(jnp.int32, (w, 2 * w), 1)
        dist = i + w - j
        band = (dist >= 0) & (dist < w)
        for hh in range(G_B):
            head = g * G_B + hh
            bias = _bias_from_dist(dist, lambda k: tab_ref[head, k]) * LOG2E
            bias_ref[hh * w:(hh + 1) * w, :] = jnp.where(band, bias, NEG)
            sink_rep_ref[hh * w:(hh + 1) * w, :] = jnp.full((w, LANE), sink_ref[head] * LOG2E, F32)
        _augment_values(v_ref, vaug_ref)

    def step(n, first_block):
        blk = ti * nsub + n
        r0 = pl.multiple_of(n * w, w)
        k0 = pl.multiple_of(jnp.maximum(blk - 1, 0) * w, w)
        k1 = pl.multiple_of(blk * w, w)
        qs = jnp.concatenate([q_ref[pl.ds(r0, w), hh * HEAD_DIM:(hh + 1) * HEAD_DIM] for hh in range(G_B)], axis=0)
        kband = jnp.concatenate([k_ref[pl.ds(k0, w), :], k_ref[pl.ds(k1, w), :]], axis=0)
        vband = jnp.concatenate([vaug_ref[pl.ds(k0, w), :], vaug_ref[pl.ds(k1, w), :]], axis=0)
        bias = bias_ref[...]
        if first_block:
            jj = lax.broadcasted_iota(jnp.int32, bias.shape, 1)
            bias = jnp.where(jj >= w, bias, NEG)
        s = _dot_nt(qs, kband) * cs + bias
        sink = sink_rep_ref[...]
        m = jnp.maximum(_row_max_rep(s), sink)
        acc = _dot(_probs(s, m), vband)
        o = acc[:, :HEAD_DIM] / (acc[:, HEAD_DIM:] + jnp.exp2(sink - m))
        for hh in range(G_B):
            o_ref[pl.ds(r0, w), hh * HEAD_DIM:(hh + 1) * HEAD_DIM] = o[hh * w:(hh + 1) * w, :].astype(o_ref.dtype)

    @pl.when(ti == 0)
    def _():
        step(0, True)

    def body(n, carry):
        step(n, False)
        return carry

    lax.fori_loop(jnp.where(ti == 0, 1, 0), nsub, body, 0)


def _swa(proj, tab_b, sinks, cat, side=()):
    b, s, _ = proj.shape
    tq = 1024
    nt = s // tq
    gw = G_B * HEAD_DIM
    qo = PROJ_OFFS[3] // gw
    oo = QA_W // gw
    ko, vo = (PROJ_OFFS[i] // HEAD_DIM for i in (4, 5))
    assert PROJ_OFFS[3] % gw == 0 and QA_W % gw == 0 and s % tq == 0
    smem = pl.BlockSpec(memory_space=pltpu.SMEM)
    side_specs = [_row_slab_specs(w, b * KV_B * nt, lambda bi, g, ti: (bi * KV_B + g) * nt + ti, w.shape[1], row0)
                  for w, row0 in side]
    side = [w for w, _ in side]
    return pl.pallas_call(
        functools.partial(_swa_kernel, len(side)),
        grid=(b, KV_B, nt),
        in_specs=[smem, smem,
                  pl.BlockSpec((None, tq, gw), lambda bi, g, ti: (bi, ti, qo + g)),
                  pl.BlockSpec((None, s, HEAD_DIM), lambda bi, g, ti: (bi, 0, ko + g)),
                  pl.BlockSpec((None, s, HEAD_DIM), lambda bi, g, ti: (bi, 0, vo + g)),
                  pl.BlockSpec(memory_space=pl.ANY)] + [sp[0] for sp in side_specs],
        out_specs=[pl.BlockSpec((None, tq, gw), lambda bi, g, ti: (bi, ti, oo + g))] + [sp[1] for sp in side_specs],
        out_shape=[jax.ShapeDtypeStruct(cat.shape, cat.dtype)] + [sp[2] for sp in side_specs],
        input_output_aliases={5: 0},
        scratch_shapes=[pltpu.VMEM((G_B * WINDOW, 2 * WINDOW), F32),
                        pltpu.VMEM((G_B * WINDOW, LANE), F32),
                        pltpu.VMEM((s, 2 * HEAD_DIM), BF16)],
        compiler_params=_params("parallel", "parallel", "arbitrary"),
        name="swa_attention",
    )(tab_b, sinks, proj, proj, proj, cat, *side)


def _diff_kernel(lam_init, row_tiles, tab_ref, q_ref, k_ref, v_ref, lamp_ref, subg_ref, cat_ref, *rest):
    del cat_ref
    n_side = 0 if row_tiles is None else 1
    side_in, o_ref, side_out = rest[:n_side], rest[n_side], rest[n_side + 1:2 * n_side + 1]
    bias_ref, vaug_ref, sa_ref, sb_ref, m_ref, acc_ref = rest[2 * n_side + 1:]
    h = pl.program_id(0)
    bi = pl.program_id(1)
    qi = pl.program_id(2)
    t = q_ref.shape[0]
    cs = DIFF_DIM ** -0.5 * LOG2E

    if row_tiles is not None:
        n_src, n_dst, depth = row_tiles
        step = (h * pl.num_programs(1) + bi) * pl.num_programs(2) + qi

        @pl.when((step < depth * n_dst) & (step % n_dst < n_src))
        def _():
            side_out[0][...] = side_in[0][...].astype(BF16)

        @pl.when((step < depth * n_dst) & (step % n_dst >= n_src))
        def _():
            side_out[0][...] = jnp.zeros(side_out[0].shape, BF16)

    @pl.when((bi == 0) & (qi == 0))
    def _():
        _bias_tiles(lambda k: tab_ref[h, k], bias_ref)

    @pl.when(qi == 0)
    def _():
        _augment_values(v_ref, vaug_ref)

    q = q_ref[...]
    lane = lax.broadcasted_iota(jnp.int32, q.shape, 1)
    zero = jnp.zeros_like(q)
    q2 = jnp.concatenate([jnp.where(lane < DIFF_DIM, q, zero), jnp.where(lane >= DIFF_DIM, q, zero)], axis=0)

    def logits(kt):
        return _dot_nt(q2, k_ref[pl.ds(pl.multiple_of(kt * t, t), t), :]) * cs

    def update(s, left, kt):
        d = bias_ref[jnp.maximum(2 - left, 0)]
        s = jnp.concatenate([s[:t] + d, s[t:] + d], axis=0)
        _softmax_update(s, vaug_ref[pl.ds(pl.multiple_of(kt * t, t), t), :], m_ref, acc_ref)

    _softmax_reset(m_ref, acc_ref)
    _causal_tiles(qi, logits, update, sa_ref, sb_ref)

    lp = lamp_ref[...]
    lam = (jnp.exp(jnp.sum(lp[0:1] * lp[1:2], axis=1, keepdims=True))
           - jnp.exp(jnp.sum(lp[2:3] * lp[3:4], axis=1, keepdims=True)) + lam_init)
    acc = acc_ref[...]
    a = acc[:, :HEAD_DIM] / acc[:, HEAD_DIM:]
    o = a[:t] - lam * a[t:]
    o = o * lax.rsqrt(jnp.mean(o * o, axis=-1, keepdims=True) + LN_EPS) * subg_ref[...]
    o_ref[...] = (o * (1.0 - lam_init)).astype(o_ref.dtype)


def _diff(proj, tab_c, lam_p, subln_g, lam_init, cat, pad_rows=None):
    b, s, _ = proj.shape
    t = ATTN_TILE
    nq = s // t
    assert s % t == 0 and t >= FAR_DIST
    qo, ko, vo = (PROJ_OFFS[i] // HEAD_DIM for i in (6, 7, 8))
    oo = (QA_W + QB_W) // HEAD_DIM
    side, side_in, side_out, side_shape, row_tiles = (), [], [], [], None
    if pad_rows is not None:
        w, k_pad = pad_rows
        depth, k, n = w.shape
        tr = CAST_ROW_TILE
        assert k % tr == 0 and k_pad % tr == 0
        n_src, n_dst = k // tr, k_pad // tr
        assert depth * n_dst <= H_C * b * nq
        row_tiles = (n_src, n_dst, depth)

        def dst_tile(h, bi, qi):
            return jnp.minimum((h * b + bi) * nq + qi, depth * n_dst - 1)

        def src_tile(h, bi, qi):
            d = dst_tile(h, bi, qi)
            return (d // n_dst) * n_src + jnp.minimum(d % n_dst, n_src - 1)

        side = (w.reshape(depth * k, n),)
        side_in = [pl.BlockSpec((tr, n), lambda h, bi, qi: (src_tile(h, bi, qi), 0))]
        side_out = [pl.BlockSpec((tr, n), lambda h, bi, qi: (dst_tile(h, bi, qi), 0))]
        side_shape = [jax.ShapeDtypeStruct((depth * k_pad, n), BF16)]
    return pl.pallas_call(
        functools.partial(_diff_kernel, lam_init, row_tiles),
        grid=(H_C, b, nq),
        in_specs=[pl.BlockSpec(memory_space=pltpu.SMEM),
                  pl.BlockSpec((None, t, HEAD_DIM), lambda h, bi, qi: (bi, qi, qo + h)),
                  pl.BlockSpec((None, s, HEAD_DIM), lambda h, bi, qi: (bi, 0, ko + h)),
                  pl.BlockSpec((None, s, HEAD_DIM), lambda h, bi, qi: (bi, 0, vo + h)),
                  pl.BlockSpec((4, DIFF_DIM), lambda h, bi, qi: (0, 0)),
                  pl.BlockSpec((1, HEAD_DIM), lambda h, bi, qi: (0, 0)),
                  pl.BlockSpec(memory_space=pl.ANY)] + side_in,
        out_specs=[pl.BlockSpec((None, t, HEAD_DIM), lambda h, bi, qi: (bi, qi, oo + h))] + side_out,
        out_shape=[jax.ShapeDtypeStruct(cat.shape, cat.dtype)] + side_shape,
        input_output_aliases={6: 0},
        scratch_shapes=[pltpu.VMEM((3, t, t), F32),
                        pltpu.VMEM((s, 2 * HEAD_DIM), BF16),
                        pltpu.VMEM((2 * t, t), F32), pltpu.VMEM((2 * t, t), F32),
                        pltpu.VMEM((2 * t, LANE), F32),
                        pltpu.VMEM((2 * t, 2 * HEAD_DIM), F32)],
        compiler_params=_params("arbitrary", "arbitrary", "arbitrary"),
        name="diff_attention",
    )(tab_c, proj, proj, proj, lam_p, subln_g, cat, *side)


def kernel(x, c, rel_bias, w_ada, b_ada, w_in, w_o, attn_sinks, diff_lambda, diff_subln_g,
           ln_g, ln_b, w_gate, w_up, w_down):
    b, s, d = x.shape
    m = b * s
    tab = rel_bias.T
    tab_a, tab_b, tab_c = tab[:H_A], tab[H_A:H_A + H_B], tab[H_A + H_B:]

    c8 = jnp.pad(c, ((0, 8 - b), (0, 0)))
    mod = _ada_mod(c8, w_ada, b_ada[:, None, :])[:, :b]
    mod = mod.reshape(DEPTH, b, 6, 1, d)

    w_in_b = [w_in[:1].astype(BF16), None]
    h = None
    for l in range(DEPTH):
        sh1, sc1, g1, sh2, sc2, g2 = (mod[l, :, i] for i in range(6))
        if h is None:
            proj = _modulate_matmul(x, sc1, sh1, w_in_b[l], 0, BF16, 1024, 512, "proj_in")
        else:
            proj = _matmul(h.reshape(m, d), w_in_b[l], 0, BF16, 1024, 512, "proj_in")
        proj = proj.reshape(b, s, D_PROJ)
        lam_init = 0.8 - 0.6 * math.exp(-0.3 * l)
        if l == 0:
            assert DEPTH == 2
            cat, wg_b, wu_b = _moba(proj, tab_a, (w_gate.reshape(DEPTH * d, D_FF), w_up.reshape(DEPTH * d, D_FF)))
            cat, w_o_b, w_in1_b = _swa(proj, tab_b, attn_sinks[l], cat,
                                       ((w_o.reshape(DEPTH * d, d), 0), (w_in.reshape(DEPTH * d, D_PROJ), d)))
            cat, wd_b = _diff(proj, tab_c, diff_lambda[l], diff_subln_g[l][None, :], lam_init, cat,
                              (w_down, D_FF_PAD))
            wg_b, wu_b = wg_b.reshape(DEPTH, d, D_FF_PAD), wu_b.reshape(DEPTH, d, D_FF_PAD)
            w_o_b, wd_b = w_o_b.reshape(DEPTH, d, d), wd_b.reshape(DEPTH, D_FF_PAD, d)
            w_in_b[1] = w_in1_b[None]
        else:
            (cat,) = _moba(proj, tab_a)
            (cat,) = _swa(proj, tab_b, attn_sinks[l], cat)
            (cat,) = _diff(proj, tab_c, diff_lambda[l], diff_subln_g[l][None, :], lam_init, cat)
        mix = _matmul(cat.reshape(m, d), w_o_b, l, BF16, 1024, 1024, "proj_out").reshape(b, s, d)
        x, h = _residual_norm(x, mix, g1, ln_g[l, 0][None, :], ln_b[l, 0][None, :], sc2, sh2)

        act = _gate_up(h.reshape(m, d), wg_b, wu_b, l, 1024, 512)
        ff = _matmul_ksplit(act, wd_b, l, BF16, 1024, 1024, D_FF_PAD // 2, "ffn_down").reshape(b, s, d)
        if l + 1 < DEPTH:
            nsh, nsc = mod[l + 1, :, 0], mod[l + 1, :, 1]
            x, h = _residual_norm(x, ff, g2, ln_g[l, 1][None, :], ln_b[l, 1][None, :], nsc, nsh)
        else:
            (x,) = _residual_norm(x, ff, g2, ln_g[l, 1][None, :], ln_b[l, 1][None, :])
    return x
```

```python
import functools
import math

import numpy as np
import jax
import jax.numpy as jnp
from jax import lax
from jax.experimental import pallas as pl
from jax.experimental.pallas import tpu as pltpu

D_MODEL = 4096
DEPTH = 2
HEAD_DIM = 128
N_HEADS = D_MODEL // HEAD_DIM
H_A = N_HEADS // 4
H_C = N_HEADS // 4
H_B = N_HEADS - H_A - H_C
KV_B = max(1, H_B // 8)
G_B = H_B // KV_B
DIFF_DIM = HEAD_DIM // 2
MOBA_BLOCK = 256
MOBA_TOPK = 3
WINDOW = 128
N_BUCKETS = 32
MAX_DISTANCE = 128
D_FF = -(-8 * D_MODEL // (3 * 256)) * 256
ALPHA = (2.0 * DEPTH) ** 0.25
LN_EPS = 1e-5
NEG = -1e30
LOG2E = math.log2(math.e)

QA_W = H_A * HEAD_DIM
QB_W = H_B * HEAD_DIM
KB_W = KV_B * HEAD_DIM
QC_W = H_C * 2 * DIFF_DIM
VC_W = H_C * HEAD_DIM
PROJ_SIZES = (QA_W, QA_W, QA_W, QB_W, KB_W, KB_W, QC_W, QC_W, VC_W)
PROJ_OFFS = tuple(int(sum(PROJ_SIZES[:i])) for i in range(len(PROJ_SIZES)))
D_PROJ = sum(PROJ_SIZES)

LANE = 128
ATTN_TILE = 512
TILE_UNROLL = 4
CAST_ROW_TILE = 256
D_FF_PAD = D_FF
VMEM_LIMIT = 56 * 1024 * 1024

F32 = jnp.float32
BF16 = jnp.bfloat16


def _bucket_upper_bounds():
    n = np.arange(0, 4 * MAX_DISTANCE, dtype=np.int64)
    max_exact = N_BUCKETS // 2
    nf = np.maximum(n, 1).astype(np.float32)
    large = max_exact + (np.log(nf / np.float32(max_exact)) / np.float32(math.log(MAX_DISTANCE / max_exact))
                         * np.float32(N_BUCKETS - max_exact)).astype(np.int32)
    large = np.minimum(large, N_BUCKETS - 1)
    bucket = np.where(n < max_exact, n, large)
    assert bucket[-1] == N_BUCKETS - 1 and np.all(np.diff(bucket) >= 0)
    return tuple(int(np.argmax(bucket > k)) for k in range(N_BUCKETS - 1))


BUCKET_UPPER = _bucket_upper_bounds()
FAR_DIST = BUCKET_UPPER[-1]


def _bias_from_dist(dist, tab):
    b = jnp.full(dist.shape, tab(N_BUCKETS - 1), F32)
    for k in range(N_BUCKETS - 2, -1, -1):
        b = jnp.where(dist < BUCKET_UPPER[k], tab(k), b)
    return b


def _dot(a, b):
    return jnp.dot(a, b, preferred_element_type=F32)


def _dot_nt(a, b):
    return lax.dot_general(a, b, (((1,), (1,)), ((), ())), preferred_element_type=F32)


def _params(*sem):
    return pltpu.CompilerParams(dimension_semantics=sem, vmem_limit_bytes=VMEM_LIMIT)


def _mod_kernel(c_ref, w_ref, b_ref, o_ref):
    c = c_ref[...]
    s = (c * jax.nn.sigmoid(c)).astype(BF16)
    o_ref[...] = _dot(s, w_ref[...].astype(BF16)) + b_ref[...]


def _ada_mod(c8, w_ada, b_ada3):
    depth, d, n = w_ada.shape
    tn = 512
    return pl.pallas_call(
        _mod_kernel,
        grid=(depth, n // tn),
        in_specs=[pl.BlockSpec((8, d), lambda l, j: (0, 0)),
                  pl.BlockSpec((None, d, tn), lambda l, j: (l, 0, j)),
                  pl.BlockSpec((None, 1, tn), lambda l, j: (l, 0, j))],
        out_specs=pl.BlockSpec((None, 8, tn), lambda l, j: (l, 0, j)),
        out_shape=jax.ShapeDtypeStruct((depth, 8, n), F32),
        compiler_params=_params("parallel", "parallel"),
        name="ada_mod",
    )(c8, w_ada, b_ada3)


def _mm_kernel(x_ref, w_ref, o_ref):
    o_ref[...] = _dot(x_ref[...], w_ref[...]).astype(o_ref.dtype)


def _matmul(x, w, l, out_dtype, tm, tn, name):
    m, k = x.shape
    _, _, n = w.shape
    return pl.pallas_call(
        _mm_kernel,
        grid=(m // tm, n // tn),
        in_specs=[pl.BlockSpec((tm, k), lambda i, j: (i, 0)),
                  pl.BlockSpec((None, k, tn), lambda i, j: (l, 0, j))],
        out_specs=pl.BlockSpec((tm, tn), lambda i, j: (i, j)),
        out_shape=jax.ShapeDtypeStruct((m, n), out_dtype),
        compiler_params=_params("parallel", "arbitrary"),
        name=name,
    )(x, w)


def _mod_mm_kernel(x_ref, sc_ref, sh_ref, w_ref, o_ref, h_ref):
    @pl.when(pl.program_id(1) == 0)
    def _():
        h_ref[...] = (x_ref[...] * (1.0 + sc_ref[...]) + sh_ref[...]).astype(BF16)

    o_ref[...] = _dot(h_ref[...], w_ref[...]).astype(o_ref.dtype)


def _modulate_matmul(x, sc, sh, w, l, out_dtype, tm, tn, name):
    b, s, k = x.shape
    _, _, n = w.shape
    assert s % tm == 0
    per_b = s // tm
    vec = pl.BlockSpec((None, 1, k), lambda i, j: (i // per_b, 0, 0))
    return pl.pallas_call(
        _mod_mm_kernel,
        grid=(b * per_b, n // tn),
        in_specs=[pl.BlockSpec((tm, k), lambda i, j: (i, 0)), vec, vec,
                  pl.BlockSpec((None, k, tn), lambda i, j: (l, 0, j))],
        out_specs=pl.BlockSpec((tm, tn), lambda i, j: (i, j)),
        out_shape=jax.ShapeDtypeStruct((b * s, n), out_dtype),
        scratch_shapes=[pltpu.VMEM((tm, k), BF16)],
        compiler_params=_params("parallel", "arbitrary"),
        name=name,
    )(x.reshape(b * s, k), sc, sh, w)


def _mm_acc_kernel(x_ref, w_ref, o_ref, acc_ref):
    kk = pl.program_id(2)
    part = _dot(x_ref[...], w_ref[...])

    @pl.when(kk == 0)
    def _():
        acc_ref[...] = part

    @pl.when((kk > 0) & (kk < pl.num_programs(2) - 1))
    def _():
        acc_ref[...] += part

    @pl.when(kk == pl.num_programs(2) - 1)
    def _():
        o_ref[...] = (acc_ref[...] + part).astype(o_ref.dtype)


def _matmul_ksplit(x, w, l, out_dtype, tm, tn, tk, name):
    m, k = x.shape
    _, _, n = w.shape
    assert k // tk >= 2
    return pl.pallas_call(
        _mm_acc_kernel,
        grid=(m // tm, n // tn, k // tk),
        in_specs=[pl.BlockSpec((tm, tk), lambda i, j, kk: (i, kk)),
                  pl.BlockSpec((None, tk, tn), lambda i, j, kk: (l, kk, j))],
        out_specs=pl.BlockSpec((tm, tn), lambda i, j, kk: (i, j)),
        out_shape=jax.ShapeDtypeStruct((m, n), out_dtype),
        scratch_shapes=[pltpu.VMEM((tm, tn), F32)],
        compiler_params=_params("parallel", "parallel", "arbitrary"),
        name=name,
    )(x, w)


def _cast_pad_cols_kernel(x_ref, o_ref):
    n = x_ref.shape[1]
    o_ref[:, :n] = x_ref[...].astype(BF16)
    if o_ref.shape[1] > n:
        o_ref[:, n:] = jnp.zeros((o_ref.shape[0], o_ref.shape[1] - n), BF16)


def _gate_up_kernel(h_ref, wg_ref, wu_ref, o_ref):
    h = h_ref[...]
    g = _dot(h, wg_ref[...])
    u = _dot(h, wu_ref[...])
    o_ref[...] = (g * jax.nn.sigmoid(g) * u).astype(BF16)


def _gate_up(h, wg, wu, l, tm, tf):
    m, k = h.shape
    _, _, f = wg.shape
    wspec = pl.BlockSpec((None, k, tf), lambda i, j: (l, 0, j))
    return pl.pallas_call(
        _gate_up_kernel,
        grid=(m // tm, f // tf),
        in_specs=[pl.BlockSpec((tm, k), lambda i, j: (i, 0)), wspec, wspec],
        out_specs=pl.BlockSpec((tm, tf), lambda i, j: (i, j)),
        out_shape=jax.ShapeDtypeStruct((m, f), BF16),
        compiler_params=_params("parallel", "arbitrary"),
        name="ffn_gate_up",
    )(h, wg, wu)


def _norm_kernel(with_h, x_ref, y_ref, gate_ref, lg_ref, lb_ref, *rest):
    z = ALPHA * x_ref[...] + gate_ref[...] * y_ref[...].astype(F32)
    mu = jnp.mean(z, axis=-1, keepdims=True)
    zc = z - mu
    var = jnp.mean(zc * zc, axis=-1, keepdims=True)
    xn = zc * lax.rsqrt(var + LN_EPS) * lg_ref[...] + lb_ref[...]
    if with_h:
        sc_ref, sh_ref, xo_ref, h_ref = rest
        xo_ref[...] = xn
        h_ref[...] = (xn * (1.0 + sc_ref[...]) + sh_ref[...]).astype(BF16)
    else:
        (xo_ref,) = rest
        xo_ref[...] = xn


def _residual_norm(x, y, gate, lg, lb, sc=None, sh=None):
    b, s, d = x.shape
    tm = 256
    with_h = sc is not None
    tile = pl.BlockSpec((None, tm, d), lambda i, j: (i, j, 0))
    vec = pl.BlockSpec((None, 1, d), lambda i, j: (i, 0, 0))
    par = pl.BlockSpec((1, d), lambda i, j: (0, 0))
    in_specs = [tile, tile, vec, par, par]
    args = [x, y, gate, lg, lb]
    out_specs = [tile]
    out_shape = [jax.ShapeDtypeStruct((b, s, d), F32)]
    if with_h:
        in_specs += [vec, vec]
        args += [sc, sh]
        out_specs.append(tile)
        out_shape.append(jax.ShapeDtypeStruct((b, s, d), BF16))
    return pl.pallas_call(
        functools.partial(_norm_kernel, with_h),
        grid=(b, s // tm),
        in_specs=in_specs,
        out_specs=out_specs,
        out_shape=out_shape,
        compiler_params=_params("parallel", "parallel"),
        name="residual_norm",
    )(*args)


def _lane_tiles(a):
    return [a[:, i * LANE:(i + 1) * LANE] for i in range(a.shape[1] // LANE)]


def _row_max_rep(s):
    tiles = _lane_tiles(s)
    t = tiles[0]
    for u in tiles[1:]:
        t = jnp.maximum(t, u)
    return jnp.broadcast_to(jnp.max(t, axis=1, keepdims=True), t.shape)


def _probs(s, m_rep):
    return jnp.concatenate([jnp.exp2(u - m_rep) for u in _lane_tiles(s)], axis=1).astype(BF16)


def _softmax_reset(m_ref, acc_ref):
    m_ref[...] = jnp.full(m_ref.shape, NEG, F32)
    acc_ref[...] = jnp.zeros(acc_ref.shape, F32)


def _softmax_update(s, v_aug, m_ref, acc_ref):
    m_old = m_ref[...]
    m = jnp.maximum(m_old, _row_max_rep(s))
    corr = jnp.exp2(m_old - m)
    m_ref[...] = m
    acc_ref[...] = jnp.concatenate([corr, corr], axis=1) * acc_ref[...] + _dot(_probs(s, m), v_aug)


def _augment_values(v_ref, vaug_ref):
    vaug_ref[:, :HEAD_DIM] = v_ref[...]
    vaug_ref[:, HEAD_DIM:] = jnp.ones((v_ref.shape[0], HEAD_DIM), BF16)


def _bias_tiles(tab, bias_ref):
    t = bias_ref.shape[1]
    i = lax.broadcasted_iota(jnp.int32, (t, t), 0)
    j = lax.broadcasted_iota(jnp.int32, (t, t), 1)
    far = tab(N_BUCKETS - 1)
    d0 = (_bias_from_dist(i - j, tab) - far) * LOG2E
    bias_ref[2] = jnp.where(i >= j, d0, NEG)
    bias_ref[1] = (_bias_from_dist(i - j + t, tab) - far) * LOG2E
    bias_ref[0] = jnp.zeros((t, t), F32)


def _causal_tiles(qi, logits_fn, update_fn, sa_ref, sb_ref):
    n = qi + 1
    bufs = (sa_ref, sb_ref)
    sa_ref[...] = logits_fn(qi)

    def group(left0, size):
        for i in range(size):
            cur, nxt = bufs[i % 2], bufs[(i + 1) % 2]
            kt = qi - (left0 + i)
            nxt[...] = logits_fn(jnp.maximum(kt - 1, 0))
            update_fn(cur[...], left0 + i, kt)

    def quad(p, carry):
        group(TILE_UNROLL * p, TILE_UNROLL)
        return carry

    quads = n // TILE_UNROLL
    lax.fori_loop(0, quads, quad, 0)
    rest = n - TILE_UNROLL * quads

    @pl.when(rest >= 2)
    def _():
        group(TILE_UNROLL * quads, 2)

    @pl.when(rest % 2 == 1)
    def _():
        update_fn(sa_ref[...], qi, 0)


def _moba_kernel(n_side, tab_ref, q_ref, k_ref, v_ref, *rest):
    side_in, o_ref, side_out = rest[:n_side], rest[n_side], rest[n_side + 1:2 * n_side + 1]
    kmean_ref, bias_ref, vaug_ref, sa_ref, sb_ref, m_ref, acc_ref = rest[2 * n_side + 1:]
    for x_ref, y_ref in zip(side_in, side_out):
        _cast_pad_cols_kernel(x_ref, y_ref)
    h = pl.program_id(0)
    bi = pl.program_id(1)
    qi = pl.program_id(2)
    t = q_ref.shape[0]
    blk = MOBA_BLOCK
    bpt = t // blk
    nb = k_ref.shape[0] // blk
    cs = HEAD_DIM ** -0.5 * LOG2E

    @pl.when((bi == 0) & (qi == 0))
    def _():
        _bias_tiles(lambda k: tab_ref[h, k], bias_ref)

    @pl.when(qi == 0)
    def _():
        for jb in range(nb):
            kb = k_ref[jb * blk:(jb + 1) * blk, :].astype(F32)
            kmean_ref[jb:jb + 1, :] = jnp.sum(kb, axis=0, keepdims=True) * (1.0 / blk)
        _augment_values(v_ref, vaug_ref)

    q = q_ref[...]
    km = kmean_ref[...]
    km_hi = km.astype(BF16)
    km_lo = (km - km_hi.astype(F32)).astype(BF16)
    gate = _dot_nt(km_hi, q) + _dot_nt(km_lo, q)
    jblk = lax.broadcasted_iota(jnp.int32, gate.shape, 0)
    qpos = lax.broadcasted_iota(jnp.int32, (1, t), 1)
    qblk = qi * bpt + qpos // blk
    g = jnp.where(jblk < qblk, gate, -jnp.inf)
    bits = jnp.left_shift(1, qblk)
    for r in range(MOBA_TOPK):
        mx = jnp.max(g, axis=0, keepdims=True)
        first = jnp.min(jnp.where(g == mx, jblk, nb), axis=0, keepdims=True)
        bits = bits | jnp.where(r < qblk, jnp.left_shift(1, first), 0)
        g = jnp.where(jblk == first, -jnp.inf, g)
    bits = jnp.broadcast_to(bits.astype(F32), (LANE, t)).T.astype(jnp.int32)

    def logits(kt):
        return _dot_nt(q, k_ref[pl.ds(pl.multiple_of(kt * t, t), t), :]) * cs

    def update(s, left, kt):
        s = s + bias_ref[jnp.maximum(2 - left, 0)]
        tiles = _lane_tiles(s)
        per_blk = blk // LANE
        out = []
        for c in range(bpt):
            vis = (jnp.right_shift(bits, kt * bpt + c) & 1) == 1
            out += [jnp.where(vis, u, NEG) for u in tiles[c * per_blk:(c + 1) * per_blk]]
        s = jnp.concatenate(out, axis=1)
        _softmax_update(s, vaug_ref[pl.ds(pl.multiple_of(kt * t, t), t), :], m_ref, acc_ref)

    _softmax_reset(m_ref, acc_ref)
    _causal_tiles(qi, logits, update, sa_ref, sb_ref)
    acc = acc_ref[...]
    o_ref[...] = (acc[:, :HEAD_DIM] / acc[:, HEAD_DIM:]).astype(o_ref.dtype)


def _row_slab_specs(w, n_steps, step_fn, n_pad, row0=0):
    rows, n = w.shape[0] - row0, w.shape[1]
    assert rows % n_steps == 0 and (rows // n_steps) % 16 == 0 and row0 % (rows // n_steps) == 0
    tr = rows // n_steps
    first = row0 // tr
    return (pl.BlockSpec((tr, n), lambda *g: (first + step_fn(*g), 0)),
            pl.BlockSpec((tr, n_pad), lambda *g: (step_fn(*g), 0)),
            jax.ShapeDtypeStruct((rows, n_pad), BF16))


def _moba(proj, tab_a, side=()):
    b, s, _ = proj.shape
    t = ATTN_TILE
    nq = s // t
    assert s % t == 0 and t % MOBA_BLOCK == 0 and t >= FAR_DIST and s // MOBA_BLOCK < 24
    qo, ko, vo = (PROJ_OFFS[i] // HEAD_DIM for i in (0, 1, 2))
    side_specs = [_row_slab_specs(w, H_A * b * nq, lambda h, bi, qi: (h * b + bi) * nq + qi, D_FF_PAD) for w in side]
    return pl.pallas_call(
        functools.partial(_moba_kernel, len(side)),
        grid=(H_A, b, nq),
        in_specs=[pl.BlockSpec(memory_space=pltpu.SMEM),
                  pl.BlockSpec((None, t, HEAD_DIM), lambda h, bi, qi: (bi, qi, qo + h)),
                  pl.BlockSpec((None, s, HEAD_DIM), lambda h, bi, qi: (bi, 0, ko + h)),
                  pl.BlockSpec((None, s, HEAD_DIM), lambda h, bi, qi: (bi, 0, vo + h))]
                 + [sp[0] for sp in side_specs],
        out_specs=[pl.BlockSpec((None, t, HEAD_DIM), lambda h, bi, qi: (bi, qi, h))] + [sp[1] for sp in side_specs],
        out_shape=[jax.ShapeDtypeStruct((b, s, D_MODEL), BF16)]
                  + [sp[2] for sp in side_specs],
        scratch_shapes=[pltpu.VMEM((s // MOBA_BLOCK, HEAD_DIM), F32),
                        pltpu.VMEM((3, t, t), F32),
                        pltpu.VMEM((s, 2 * HEAD_DIM), BF16),
                        pltpu.VMEM((t, t), F32), pltpu.VMEM((t, t), F32),
                        pltpu.VMEM((t, LANE), F32),
                        pltpu.VMEM((t, 2 * HEAD_DIM), F32)],
        compiler_params=_params("arbitrary", "arbitrary", "arbitrary"),
        name="moba_attention",
    )(tab_a, proj, proj, proj, *side)


def _swa_kernel(n_side, tab_ref, sink_ref, q_ref, k_ref, v_ref, cat_ref, *rest):
    del cat_ref
    side_in, o_ref, side_out = rest[:n_side], rest[n_side], rest[n_side + 1:2 * n_side + 1]
    bias_ref, sink_rep_ref, vaug_ref, sa_ref, sb_ref = rest[2 * n_side + 1:]
    for x_ref, y_ref in zip(side_in, side_out):
        _cast_pad_cols_kernel(x_ref, y_ref)
    g = pl.program_id(1)
    ti = pl.program_id(2)
    w = WINDOW
    nsub = q_ref.shape[0] // w
    cs = HEAD_DIM ** -0.5 * LOG2E

    @pl.when(ti == 0)
    def _():
        i = lax.broadcasted_iota(jnp.int32, (w, 2 * w), 0)
        j = lax.broadcasted_iota(jnp.int32, (w, 2 * w), 1)
        dist = i + w - j
        band = (dist >= 0) & (dist < w)
        for hh in range(G_B):
            head = g * G_B + hh
            bias = jnp.where(band, _bias_from_dist(dist, lambda k: tab_ref[head, k]) * LOG2E, NEG)
            bias_ref[0, hh * w:(hh + 1) * w, :] = bias
            bias_ref[1, hh * w:(hh + 1) * w, :] = jnp.where(j >= w, bias, NEG)
            sink_rep_ref[hh * w:(hh + 1) * w, :] = jnp.full((w, LANE), sink_ref[head] * LOG2E, F32)
        _augment_values(v_ref, vaug_ref)

    def band_rows(ref, blk):
        k0 = pl.multiple_of(jnp.maximum(blk - 1, 0) * w, w)
        k1 = pl.multiple_of(blk * w, w)
        return jnp.concatenate([ref[pl.ds(k0, w), :], ref[pl.ds(k1, w), :]], axis=0)

    def logits(n):
        qs = jnp.concatenate([q_ref[n * w:(n + 1) * w, hh * HEAD_DIM:(hh + 1) * HEAD_DIM] for hh in range(G_B)], axis=0)
        return _dot_nt(qs, band_rows(k_ref, ti * nsub + n)) * cs

    def finish(s, n):
        blk = ti * nsub + n
        s = s + bias_ref[jnp.where(blk == 0, 1, 0)]
        sink = sink_rep_ref[...]
        m = jnp.maximum(_row_max_rep(s), sink)
        acc = _dot(_probs(s, m), band_rows(vaug_ref, blk))
        o = acc[:, :HEAD_DIM] / (acc[:, HEAD_DIM:] + jnp.exp2(sink - m))
        for hh in range(G_B):
            o_ref[n * w:(n + 1) * w, hh * HEAD_DIM:(hh + 1) * HEAD_DIM] = o[hh * w:(hh + 1) * w, :].astype(o_ref.dtype)

    bufs = (sa_ref, sb_ref)
    sa_ref[...] = logits(0)
    for n in range(nsub):
        if n + 1 < nsub:
            bufs[(n + 1) % 2][...] = logits(n + 1)
        finish(bufs[n % 2][...], n)


def _swa(proj, tab_b, sinks, cat, side=()):
    b, s, _ = proj.shape
    tq = 1024
    nt = s // tq
    gw = G_B * HEAD_DIM
    qo = PROJ_OFFS[3] // gw
    oo = QA_W // gw
    ko, vo = (PROJ_OFFS[i] // HEAD_DIM for i in (4, 5))
    assert PROJ_OFFS[3] % gw == 0 and QA_W % gw == 0 and s % tq == 0
    smem = pl.BlockSpec(memory_space=pltpu.SMEM)
    side_specs = [_row_slab_specs(w, b * KV_B * nt, lambda bi, g, ti: (bi * KV_B + g) * nt + ti, w.shape[1], row0)
                  for w, row0 in side]
    side = [w for w, _ in side]
    return pl.pallas_call(
        functools.partial(_swa_kernel, len(side)),
        grid=(b, KV_B, nt),
        in_specs=[smem, smem,
                  pl.BlockSpec((None, tq, gw), lambda bi, g, ti: (bi, ti, qo + g)),
                  pl.BlockSpec((None, s, HEAD_DIM), lambda bi, g, ti: (bi, 0, ko + g)),
                  pl.BlockSpec((None, s, HEAD_DIM), lambda bi, g, ti: (bi, 0, vo + g)),
                  pl.BlockSpec(memory_space=pl.ANY)] + [sp[0] for sp in side_specs],
        out_specs=[pl.BlockSpec((None, tq, gw), lambda bi, g, ti: (bi, ti, oo + g))] + [sp[1] for sp in side_specs],
        out_shape=[jax.ShapeDtypeStruct(cat.shape, cat.dtype)] + [sp[2] for sp in side_specs],
        input_output_aliases={5: 0},
        scratch_shapes=[pltpu.VMEM((2, G_B * WINDOW, 2 * WINDOW), F32),
                        pltpu.VMEM((G_B * WINDOW, LANE), F32),
                        pltpu.VMEM((s, 2 * HEAD_DIM), BF16),
                        pltpu.VMEM((G_B * WINDOW, 2 * WINDOW), F32),
                        pltpu.VMEM((G_B * WINDOW, 2 * WINDOW), F32)],
        compiler_params=_params("parallel", "parallel", "arbitrary"),
        name="swa_attention",
    )(tab_b, sinks, proj, proj, proj, cat, *side)


def _diff_kernel(lam_init, row_tiles, tab_ref, q_ref, k_ref, v_ref, lamp_ref, subg_ref, cat_ref, *rest):
    del cat_ref
    n_side = 0 if row_tiles is None else 1
    side_in, o_ref, side_out = rest[:n_side], rest[n_side], rest[n_side + 1:2 * n_side + 1]
    bias_ref, vaug_ref, sa_ref, sb_ref, m_ref, acc_ref = rest[2 * n_side + 1:]
    h = pl.program_id(0)
    bi = pl.program_id(1)
    qi = pl.program_id(2)
    t = q_ref.shape[0]
    cs = DIFF_DIM ** -0.5 * LOG2E

    if row_tiles is not None:
        n_src, n_dst, depth = row_tiles
        step = (h * pl.num_programs(1) + bi) * pl.num_programs(2) + qi

        @pl.when((step < depth * n_dst) & (step % n_dst < n_src))
        def _():
            side_out[0][...] = side_in[0][...].astype(BF16)

        @pl.when((step < depth * n_dst) & (step % n_dst >= n_src))
        def _():
            side_out[0][...] = jnp.zeros(side_out[0].shape, BF16)

    @pl.when((bi == 0) & (qi == 0))
    def _():
        _bias_tiles(lambda k: tab_ref[h, k], bias_ref)

    @pl.when(qi == 0)
    def _():
        _augment_values(v_ref, vaug_ref)

    q = q_ref[...]
    lane = lax.broadcasted_iota(jnp.int32, q.shape, 1)
    zero = jnp.zeros_like(q)
    q2 = jnp.concatenate([jnp.where(lane < DIFF_DIM, q, zero), jnp.where(lane >= DIFF_DIM, q, zero)], axis=0)

    def logits(kt):
        return _dot_nt(q2, k_ref[pl.ds(pl.multiple_of(kt * t, t), t), :]) * cs

    def update(s, left, kt):
        d = bias_ref[jnp.maximum(2 - left, 0)]
        s = jnp.concatenate([s[:t] + d, s[t:] + d], axis=0)
        _softmax_update(s, vaug_ref[pl.ds(pl.multiple_of(kt * t, t), t), :], m_ref, acc_ref)

    _softmax_reset(m_ref, acc_ref)
    _causal_tiles(qi, logits, update, sa_ref, sb_ref)

    lp = lamp_ref[...]
    lam = (jnp.exp(jnp.sum(lp[0:1] * lp[1:2], axis=1, keepdims=True))
           - jnp.exp(jnp.sum(lp[2:3] * lp[3:4], axis=1, keepdims=True)) + lam_init)
    acc = acc_ref[...]
    a = acc[:, :HEAD_DIM] / acc[:, HEAD_DIM:]
    o = a[:t] - lam * a[t:]
    o = o * lax.rsqrt(jnp.mean(o * o, axis=-1, keepdims=True) + LN_EPS) * subg_ref[...]
    o_ref[...] = (o * (1.0 - lam_init)).astype(o_ref.dtype)


def _diff(proj, tab_c, lam_p, subln_g, lam_init, cat, pad_rows=None):
    b, s, _ = proj.shape
    t = ATTN_TILE
    nq = s // t
    assert s % t == 0 and t >= FAR_DIST
    qo, ko, vo = (PROJ_OFFS[i] // HEAD_DIM for i in (6, 7, 8))
    oo = (QA_W + QB_W) // HEAD_DIM
    side, side_in, side_out, side_shape, row_tiles = (), [], [], [], None
    if pad_rows is not None:
        w, k_pad = pad_rows
        depth, k, n = w.shape
        tr = CAST_ROW_TILE
        assert k % tr == 0 and k_pad % tr == 0
        n_src, n_dst = k // tr, k_pad // tr
        assert depth * n_dst <= H_C * b * nq
        row_tiles = (n_src, n_dst, depth)

        def dst_tile(h, bi, qi):
            return jnp.minimum((h * b + bi) * nq + qi, depth * n_dst - 1)

        def src_tile(h, bi, qi):
            d = dst_tile(h, bi, qi)
            return (d // n_dst) * n_src + jnp.minimum(d % n_dst, n_src - 1)

        side = (w.reshape(depth * k, n),)
        side_in = [pl.BlockSpec((tr, n), lambda h, bi, qi: (src_tile(h, bi, qi), 0))]
        side_out = [pl.BlockSpec((tr, n), lambda h, bi, qi: (dst_tile(h, bi, qi), 0))]
        side_shape = [jax.ShapeDtypeStruct((depth * k_pad, n), BF16)]
    return pl.pallas_call(
        functools.partial(_diff_kernel, lam_init, row_tiles),
        grid=(H_C, b, nq),
        in_specs=[pl.BlockSpec(memory_space=pltpu.SMEM),
                  pl.BlockSpec((None, t, HEAD_DIM), lambda h, bi, qi: (bi, qi, qo + h)),
                  pl.BlockSpec((None, s, HEAD_DIM), lambda h, bi, qi: (bi, 0, ko + h)),
                  pl.BlockSpec((None, s, HEAD_DIM), lambda h, bi, qi: (bi, 0, vo + h)),
                  pl.BlockSpec((4, DIFF_DIM), lambda h, bi, qi: (0, 0)),
                  pl.BlockSpec((1, HEAD_DIM), lambda h, bi, qi: (0, 0)),
                  pl.BlockSpec(memory_space=pl.ANY)] + side_in,
        out_specs=[pl.BlockSpec((None, t, HEAD_DIM), lambda h, bi, qi: (bi, qi, oo + h))] + side_out,
        out_shape=[jax.ShapeDtypeStruct(cat.shape, cat.dtype)] + side_shape,
        input_output_aliases={6: 0},
        scratch_shapes=[pltpu.VMEM((3, t, t), F32),
                        pltpu.VMEM((s, 2 * HEAD_DIM), BF16),
                        pltpu.VMEM((2 * t, t), F32), pltpu.VMEM((2 * t, t), F32),
                        pltpu.VMEM((2 * t, LANE), F32),
                        pltpu.VMEM((2 * t, 2 * HEAD_DIM), F32)],
        compiler_params=_params("arbitrary", "arbitrary", "arbitrary"),
        name="diff_attention",
    )(tab_c, proj, proj, proj, lam_p, subln_g, cat, *side)


def kernel(x, c, rel_bias, w_ada, b_ada, w_in, w_o, attn_sinks, diff_lambda, diff_subln_g,
           ln_g, ln_b, w_gate, w_up, w_down):
    b, s, d = x.shape
    m = b * s
    tab = rel_bias.T
    tab_a, tab_b, tab_c = tab[:H_A], tab[H_A:H_A + H_B], tab[H_A + H_B:]

    c8 = jnp.pad(c, ((0, 8 - b), (0, 0)))
    mod = _ada_mod(c8, w_ada, b_ada[:, None, :])[:, :b]
    mod = mod.reshape(DEPTH, b, 6, 1, d)

    w_in_b = [w_in[:1].astype(BF16), None]
    h = None
    for l in range(DEPTH):
        sh1, sc1, g1, sh2, sc2, g2 = (mod[l, :, i] for i in range(6))
        if h is None:
            proj = _modulate_matmul(x, sc1, sh1, w_in_b[l], 0, BF16, 1024, 512, "proj_in")
        else:
            proj = _matmul(h.reshape(m, d), w_in_b[l], 0, BF16, 1024, 512, "proj_in")
        proj = proj.reshape(b, s, D_PROJ)
        lam_init = 0.8 - 0.6 * math.exp(-0.3 * l)
        if l == 0:
            assert DEPTH == 2
            cat, wg_b, wu_b = _moba(proj, tab_a, (w_gate.reshape(DEPTH * d, D_FF), w_up.reshape(DEPTH * d, D_FF)))
            cat, w_o_b, w_in1_b = _swa(proj, tab_b, attn_sinks[l], cat,
                                       ((w_o.reshape(DEPTH * d, d), 0), (w_in.reshape(DEPTH * d, D_PROJ), d)))
            cat, wd_b = _diff(proj, tab_c, diff_lambda[l], diff_subln_g[l][None, :], lam_init, cat,
                              (w_down, D_FF_PAD))
            wg_b, wu_b = wg_b.reshape(DEPTH, d, D_FF_PAD), wu_b.reshape(DEPTH, d, D_FF_PAD)
            w_o_b, wd_b = w_o_b.reshape(DEPTH, d, d), wd_b.reshape(DEPTH, D_FF_PAD, d)
            w_in_b[1] = w_in1_b[None]
        else:
            (cat,) = _moba(proj, tab_a)
            (cat,) = _swa(proj, tab_b, attn_sinks[l], cat)
            (cat,) = _diff(proj, tab_c, diff_lambda[l], diff_subln_g[l][None, :], lam_init, cat)
        mix = _matmul(cat.reshape(m, d), w_o_b, l, BF16, 1024, 1024, "proj_out").reshape(b, s, d)
        x, h = _residual_norm(x, mix, g1, ln_g[l, 0][None, :], ln_b[l, 0][None, :], sc2, sh2)

        act = _gate_up(h.reshape(m, d), wg_b, wu_b, l, 2048, 256)
        ff = _matmul_ksplit(act, wd_b, l, BF16, 1024, 1024, D_FF_PAD // 2, "ffn_down").reshape(b, s, d)
        if l + 1 < DEPTH:
            nsh, nsc = mod[l + 1, :, 0], mod[l + 1, :, 1]
            x, h = _residual_norm(x, ff, g2, ln_g[l, 1][None, :], ln_b[l, 1][None, :], nsc, nsh)
        else:
            (x,) = _residual_norm(x, ff, g2, ln_g[l, 1][None, :], ln_b[l, 1][None, :])
    return x
```

```python
import functools
import math

import numpy as np
import jax
import jax.numpy as jnp
from jax import lax
from jax.experimental import pallas as pl
from jax.experimental.pallas import tpu as pltpu

D_MODEL = 4096
DEPTH = 2
HEAD_DIM = 128
N_HEADS = D_MODEL // HEAD_DIM
H_A = N_HEADS // 4
H_C = N_HEADS // 4
H_B = N_HEADS - H_A - H_C
KV_B = max(1, H_B // 8)
G_B = H_B // KV_B
DIFF_DIM = HEAD_DIM // 2
MOBA_BLOCK = 256
MOBA_TOPK = 3
WINDOW = 128
N_BUCKETS = 32
MAX_DISTANCE = 128
D_FF = -(-8 * D_MODEL // (3 * 256)) * 256
ALPHA = (2.0 * DEPTH) ** 0.25
LN_EPS = 1e-5
NEG = -1e30
LOG2E = math.log2(math.e)

QA_W = H_A * HEAD_DIM
QB_W = H_B * HEAD_DIM
KB_W = KV_B * HEAD_DIM
QC_W = H_C * 2 * DIFF_DIM
VC_W = H_C * HEAD_DIM
PROJ_SIZES = (QA_W, QA_W, QA_W, QB_W, KB_W, KB_W, QC_W, QC_W, VC_W)
PROJ_OFFS = tuple(int(sum(PROJ_SIZES[:i])) for i in range(len(PROJ_SIZES)))
D_PROJ = sum(PROJ_SIZES)

LANE = 128
ATTN_TILE = 512
TILE_UNROLL = 4
CAST_ROW_TILE = 256
VMEM_LIMIT = 56 * 1024 * 1024
TILES_PROJ_IN = (1024, 512)
TILES_PROJ_OUT = (1024, 1024)
TILES_GATE_UP = (2048, 256)
TILES_DOWN = (1024, 1024, D_FF // 2)
NORM_ROWS = 256
ADA_COLS = 512
SWA_ROWS = 1024

F32 = jnp.float32
BF16 = jnp.bfloat16


def _bucket_upper_bounds():
    n = np.arange(0, 4 * MAX_DISTANCE, dtype=np.int64)
    max_exact = N_BUCKETS // 2
    nf = np.maximum(n, 1).astype(np.float32)
    large = max_exact + (np.log(nf / np.float32(max_exact)) / np.float32(math.log(MAX_DISTANCE / max_exact))
                         * np.float32(N_BUCKETS - max_exact)).astype(np.int32)
    large = np.minimum(large, N_BUCKETS - 1)
    bucket = np.where(n < max_exact, n, large)
    assert bucket[-1] == N_BUCKETS - 1 and np.all(np.diff(bucket) >= 0)
    return tuple(int(np.argmax(bucket > k)) for k in range(N_BUCKETS - 1))


BUCKET_UPPER = _bucket_upper_bounds()
FAR_DIST = BUCKET_UPPER[-1]


def _bias_from_dist(dist, tab):
    b = jnp.full(dist.shape, tab(N_BUCKETS - 1), F32)
    for k in range(N_BUCKETS - 2, -1, -1):
        b = jnp.where(dist < BUCKET_UPPER[k], tab(k), b)
    return b


def _dot(a, b):
    return jnp.dot(a, b, preferred_element_type=F32)


def _dot_nt(a, b):
    return lax.dot_general(a, b, (((1,), (1,)), ((), ())), preferred_element_type=F32)


def _params(*sem):
    return pltpu.CompilerParams(dimension_semantics=sem, vmem_limit_bytes=VMEM_LIMIT)


def _mod_kernel(c_ref, w_ref, b_ref, o_ref):
    c = c_ref[...]
    s = (c * jax.nn.sigmoid(c)).astype(BF16)
    o_ref[...] = _dot(s, w_ref[...].astype(BF16)) + b_ref[...]


def _ada_mod(c8, w_ada, b_ada3):
    depth, d, n = w_ada.shape
    tn = ADA_COLS
    return pl.pallas_call(
        _mod_kernel,
        grid=(depth, n // tn),
        in_specs=[pl.BlockSpec((8, d), lambda l, j: (0, 0)),
                  pl.BlockSpec((None, d, tn), lambda l, j: (l, 0, j)),
                  pl.BlockSpec((None, 1, tn), lambda l, j: (l, 0, j))],
        out_specs=pl.BlockSpec((None, 8, tn), lambda l, j: (l, 0, j)),
        out_shape=jax.ShapeDtypeStruct((depth, 8, n), F32),
        compiler_params=_params("parallel", "parallel"),
        name="ada_mod",
    )(c8, w_ada, b_ada3)


def _mm_kernel(x_ref, w_ref, o_ref):
    o_ref[...] = _dot(x_ref[...], w_ref[...]).astype(o_ref.dtype)


def _matmul(x, w, l, out_dtype, tm, tn, name):
    m, k = x.shape
    _, _, n = w.shape
    return pl.pallas_call(
        _mm_kernel,
        grid=(m // tm, n // tn),
        in_specs=[pl.BlockSpec((tm, k), lambda i, j: (i, 0)),
                  pl.BlockSpec((None, k, tn), lambda i, j: (l, 0, j))],
        out_specs=pl.BlockSpec((tm, tn), lambda i, j: (i, j)),
        out_shape=jax.ShapeDtypeStruct((m, n), out_dtype),
        compiler_params=_params("parallel", "arbitrary"),
        name=name,
    )(x, w)


def _mod_mm_kernel(x_ref, sc_ref, sh_ref, w_ref, o_ref, h_ref):
    @pl.when(pl.program_id(1) == 0)
    def _():
        h_ref[...] = (x_ref[...] * (1.0 + sc_ref[...]) + sh_ref[...]).astype(BF16)

    o_ref[...] = _dot(h_ref[...], w_ref[...]).astype(o_ref.dtype)


def _modulate_matmul(x, sc, sh, w, l, out_dtype, tm, tn, name):
    b, s, k = x.shape
    _, _, n = w.shape
    assert s % tm == 0
    per_b = s // tm
    vec = pl.BlockSpec((None, 1, k), lambda i, j: (i // per_b, 0, 0))
    return pl.pallas_call(
        _mod_mm_kernel,
        grid=(b * per_b, n // tn),
        in_specs=[pl.BlockSpec((tm, k), lambda i, j: (i, 0)), vec, vec,
                  pl.BlockSpec((None, k, tn), lambda i, j: (l, 0, j))],
        out_specs=pl.BlockSpec((tm, tn), lambda i, j: (i, j)),
        out_shape=jax.ShapeDtypeStruct((b * s, n), out_dtype),
        scratch_shapes=[pltpu.VMEM((tm, k), BF16)],
        compiler_params=_params("parallel", "arbitrary"),
        name=name,
    )(x.reshape(b * s, k), sc, sh, w)


def _mm_acc_kernel(x_ref, w_ref, o_ref, acc_ref):
    kk = pl.program_id(2)
    part = _dot(x_ref[...], w_ref[...])

    @pl.when(kk == 0)
    def _():
        acc_ref[...] = part

    @pl.when((kk > 0) & (kk < pl.num_programs(2) - 1))
    def _():
        acc_ref[...] += part

    @pl.when(kk == pl.num_programs(2) - 1)
    def _():
        o_ref[...] = (acc_ref[...] + part).astype(o_ref.dtype)


def _matmul_ksplit(x, w, l, out_dtype, tm, tn, tk, name):
    m, k = x.shape
    _, _, n = w.shape
    assert k // tk >= 2
    return pl.pallas_call(
        _mm_acc_kernel,
        grid=(m // tm, n // tn, k // tk),
        in_specs=[pl.BlockSpec((tm, tk), lambda i, j, kk: (i, kk)),
                  pl.BlockSpec((None, tk, tn), lambda i, j, kk: (l, kk, j))],
        out_specs=pl.BlockSpec((tm, tn), lambda i, j, kk: (i, j)),
        out_shape=jax.ShapeDtypeStruct((m, n), out_dtype),
        scratch_shapes=[pltpu.VMEM((tm, tn), F32)],
        compiler_params=_params("parallel", "parallel", "arbitrary"),
        name=name,
    )(x, w)


def _cast_pad_cols_kernel(x_ref, o_ref):
    n = x_ref.shape[1]
    o_ref[:, :n] = x_ref[...].astype(BF16)
    if o_ref.shape[1] > n:
        o_ref[:, n:] = jnp.zeros((o_ref.shape[0], o_ref.shape[1] - n), BF16)


def _gate_up_kernel(h_ref, wg_ref, wu_ref, o_ref):
    h = h_ref[...]
    g = _dot(h, wg_ref[...])
    u = _dot(h, wu_ref[...])
    o_ref[...] = (g * jax.nn.sigmoid(g) * u).astype(BF16)


def _gate_up(h, wg, wu, l, tm, tf):
    m, k = h.shape
    _, _, f = wg.shape
    wspec = pl.BlockSpec((None, k, tf), lambda i, j: (l, 0, j))
    return pl.pallas_call(
        _gate_up_kernel,
        grid=(m // tm, f // tf),
        in_specs=[pl.BlockSpec((tm, k), lambda i, j: (i, 0)), wspec, wspec],
        out_specs=pl.BlockSpec((tm, tf), lambda i, j: (i, j)),
        out_shape=jax.ShapeDtypeStruct((m, f), BF16),
        compiler_params=_params("parallel", "arbitrary"),
        name="ffn_gate_up",
    )(h, wg, wu)


def _norm_kernel(with_h, x_ref, y_ref, gate_ref, lg_ref, lb_ref, *rest):
    z = ALPHA * x_ref[...] + gate_ref[...] * y_ref[...].astype(F32)
    mu = jnp.mean(z, axis=-1, keepdims=True)
    zc = z - mu
    var = jnp.mean(zc * zc, axis=-1, keepdims=True)
    xn = zc * lax.rsqrt(var + LN_EPS) * lg_ref[...] + lb_ref[...]
    if with_h:
        sc_ref, sh_ref, xo_ref, h_ref = rest
        xo_ref[...] = xn
        h_ref[...] = (xn * (1.0 + sc_ref[...]) + sh_ref[...]).astype(BF16)
    else:
        (xo_ref,) = rest
        xo_ref[...] = xn


def _residual_norm(x, y, gate, lg, lb, sc=None, sh=None):
    b, s, d = x.shape
    tm = NORM_ROWS
    with_h = sc is not None
    tile = pl.BlockSpec((None, tm, d), lambda i, j: (i, j, 0))
    vec = pl.BlockSpec((None, 1, d), lambda i, j: (i, 0, 0))
    par = pl.BlockSpec((1, d), lambda i, j: (0, 0))
    in_specs = [tile, tile, vec, par, par]
    args = [x, y, gate, lg, lb]
    out_specs = [tile]
    out_shape = [jax.ShapeDtypeStruct((b, s, d), F32)]
    if with_h:
        in_specs += [vec, vec]
        args += [sc, sh]
        out_specs.append(tile)
        out_shape.append(jax.ShapeDtypeStruct((b, s, d), BF16))
    return pl.pallas_call(
        functools.partial(_norm_kernel, with_h),
        grid=(b, s // tm),
        in_specs=in_specs,
        out_specs=out_specs,
        out_shape=out_shape,
        compiler_params=_params("parallel", "parallel"),
        name="residual_norm",
    )(*args)


def _lane_tiles(a):
    return [a[:, i * LANE:(i + 1) * LANE] for i in range(a.shape[1] // LANE)]


def _row_max_rep(s):
    tiles = _lane_tiles(s)
    t = tiles[0]
    for u in tiles[1:]:
        t = jnp.maximum(t, u)
    return jnp.broadcast_to(jnp.max(t, axis=1, keepdims=True), t.shape)


def _probs(s, m_rep):
    return jnp.concatenate([jnp.exp2(u - m_rep) for u in _lane_tiles(s)], axis=1).astype(BF16)


def _softmax_reset(m_ref, acc_ref):
    m_ref[...] = jnp.full(m_ref.shape, NEG, F32)
    acc_ref[...] = jnp.zeros(acc_ref.shape, F32)


def _softmax_update(s, v_aug, m_ref, acc_ref):
    m_old = m_ref[...]
    m = jnp.maximum(m_old, _row_max_rep(s))
    corr = jnp.exp2(m_old - m)
    m_ref[...] = m
    acc_ref[...] = jnp.concatenate([corr, corr], axis=1) * acc_ref[...] + _dot(_probs(s, m), v_aug)


def _augment_values(v_ref, vaug_ref):
    vaug_ref[:, :HEAD_DIM] = v_ref[...]
    vaug_ref[:, HEAD_DIM:] = jnp.ones((v_ref.shape[0], HEAD_DIM), BF16)


def _bias_tiles(tab, bias_ref):
    t = bias_ref.shape[1]
    i = lax.broadcasted_iota(jnp.int32, (t, t), 0)
    j = lax.broadcasted_iota(jnp.int32, (t, t), 1)
    far = tab(N_BUCKETS - 1)
    d0 = (_bias_from_dist(i - j, tab) - far) * LOG2E
    bias_ref[2] = jnp.where(i >= j, d0, NEG)
    bias_ref[1] = (_bias_from_dist(i - j + t, tab) - far) * LOG2E
    bias_ref[0] = jnp.zeros((t, t), F32)


def _causal_tiles(qi, logits_fn, update_fn, sa_ref, sb_ref):
    n = qi + 1
    bufs = (sa_ref, sb_ref)
    sa_ref[...] = logits_fn(qi)

    def group(left0, size):
        for i in range(size):
            cur, nxt = bufs[i % 2], bufs[(i + 1) % 2]
            kt = qi - (left0 + i)
            nxt[...] = logits_fn(jnp.maximum(kt - 1, 0))
            update_fn(cur[...], left0 + i, kt)

    def quad(p, carry):
        group(TILE_UNROLL * p, TILE_UNROLL)
        return carry

    quads = n // TILE_UNROLL
    lax.fori_loop(0, quads, quad, 0)
    rest = n - TILE_UNROLL * quads

    @pl.when(rest >= 2)
    def _():
        group(TILE_UNROLL * quads, 2)

    @pl.when(rest % 2 == 1)
    def _():
        update_fn(sa_ref[...], qi, 0)


def _moba_kernel(n_side, tab_ref, q_ref, k_ref, v_ref, *rest):
    side_in, o_ref, side_out = rest[:n_side], rest[n_side], rest[n_side + 1:2 * n_side + 1]
    kmean_ref, bias_ref, vaug_ref, sa_ref, sb_ref, m_ref, acc_ref = rest[2 * n_side + 1:]
    for x_ref, y_ref in zip(side_in, side_out):
        _cast_pad_cols_kernel(x_ref, y_ref)
    h = pl.program_id(0)
    bi = pl.program_id(1)
    qi = pl.program_id(2)
    t = q_ref.shape[0]
    blk = MOBA_BLOCK
    bpt = t // blk
    nb = k_ref.shape[0] // blk
    cs = HEAD_DIM ** -0.5 * LOG2E

    @pl.when((bi == 0) & (qi == 0))
    def _():
        _bias_tiles(lambda k: tab_ref[h, k], bias_ref)

    @pl.when(qi == 0)
    def _():
        for jb in range(nb):
            kb = k_ref[jb * blk:(jb + 1) * blk, :].astype(F32)
            kmean_ref[jb:jb + 1, :] = jnp.sum(kb, axis=0, keepdims=True) * (1.0 / blk)
        _augment_values(v_ref, vaug_ref)

    q = q_ref[...]
    km = kmean_ref[...]
    km_hi = km.astype(BF16)
    km_lo = (km - km_hi.astype(F32)).astype(BF16)
    gate = _dot_nt(km_hi, q) + _dot_nt(km_lo, q)
    jblk = lax.broadcasted_iota(jnp.int32, gate.shape, 0)
    qpos = lax.broadcasted_iota(jnp.int32, (1, t), 1)
    qblk = qi * bpt + qpos // blk
    g = jnp.where(jblk < qblk, gate, -jnp.inf)
    bits = jnp.left_shift(1, qblk)
    for r in range(MOBA_TOPK):
        mx = jnp.max(g, axis=0, keepdims=True)
        first = jnp.min(jnp.where(g == mx, jblk, nb), axis=0, keepdims=True)
        bits = bits | jnp.where(r < qblk, jnp.left_shift(1, first), 0)
        g = jnp.where(jblk == first, -jnp.inf, g)
    bits = jnp.broadcast_to(bits.astype(F32), (LANE, t)).T.astype(jnp.int32)

    def logits(kt):
        return _dot_nt(q, k_ref[pl.ds(pl.multiple_of(kt * t, t), t), :]) * cs

    def update(s, left, kt):
        s = s + bias_ref[jnp.maximum(2 - left, 0)]
        tiles = _lane_tiles(s)
        per_blk = blk // LANE
        out = []
        for c in range(bpt):
            vis = (jnp.right_shift(bits, kt * bpt + c) & 1) == 1
            out += [jnp.where(vis, u, NEG) for u in tiles[c * per_blk:(c + 1) * per_blk]]
        s = jnp.concatenate(out, axis=1)
        _softmax_update(s, vaug_ref[pl.ds(pl.multiple_of(kt * t, t), t), :], m_ref, acc_ref)

    _softmax_reset(m_ref, acc_ref)
    _causal_tiles(qi, logits, update, sa_ref, sb_ref)
    acc = acc_ref[...]
    o_ref[...] = (acc[:, :HEAD_DIM] / acc[:, HEAD_DIM:]).astype(o_ref.dtype)


def _row_slab_specs(w, n_steps, step_fn, n_pad, row0=0):
    rows, n = w.shape[0] - row0, w.shape[1]
    assert rows % n_steps == 0 and (rows // n_steps) % 16 == 0 and row0 % (rows // n_steps) == 0
    tr = rows // n_steps
    first = row0 // tr
    return (pl.BlockSpec((tr, n), lambda *g: (first + step_fn(*g), 0)),
            pl.BlockSpec((tr, n_pad), lambda *g: (step_fn(*g), 0)),
            jax.ShapeDtypeStruct((rows, n_pad), BF16))


def _moba(proj, tab_a, side=()):
    b, s, _ = proj.shape
    t = ATTN_TILE
    nq = s // t
    assert s % t == 0 and t % MOBA_BLOCK == 0 and t >= FAR_DIST and s // MOBA_BLOCK < 24
    qo, ko, vo = (PROJ_OFFS[i] // HEAD_DIM for i in (0, 1, 2))
    side_specs = [_row_slab_specs(w, H_A * b * nq, lambda h, bi, qi: (h * b + bi) * nq + qi, w.shape[1]) for w in side]
    return pl.pallas_call(
        functools.partial(_moba_kernel, len(side)),
        grid=(H_A, b, nq),
        in_specs=[pl.BlockSpec(memory_space=pltpu.SMEM),
                  pl.BlockSpec((None, t, HEAD_DIM), lambda h, bi, qi: (bi, qi, qo + h)),
                  pl.BlockSpec((None, s, HEAD_DIM), lambda h, bi, qi: (bi, 0, ko + h)),
                  pl.BlockSpec((None, s, HEAD_DIM), lambda h, bi, qi: (bi, 0, vo + h))]
                 + [sp[0] for sp in side_specs],
        out_specs=[pl.BlockSpec((None, t, HEAD_DIM), lambda h, bi, qi: (bi, qi, h))] + [sp[1] for sp in side_specs],
        out_shape=[jax.ShapeDtypeStruct((b, s, D_MODEL), BF16)]
                  + [sp[2] for sp in side_specs],
        scratch_shapes=[pltpu.VMEM((s // MOBA_BLOCK, HEAD_DIM), F32),
                        pltpu.VMEM((3, t, t), F32),
                        pltpu.VMEM((s, 2 * HEAD_DIM), BF16),
                        pltpu.VMEM((t, t), F32), pltpu.VMEM((t, t), F32),
                        pltpu.VMEM((t, LANE), F32),
                        pltpu.VMEM((t, 2 * HEAD_DIM), F32)],
        compiler_params=_params("arbitrary", "arbitrary", "arbitrary"),
        name="moba_attention",
    )(tab_a, proj, proj, proj, *side)


def _swa_kernel(n_side, tab_ref, sink_ref, q_ref, k_ref, v_ref, cat_ref, *rest):
    del cat_ref
    side_in, o_ref, side_out = rest[:n_side], rest[n_side], rest[n_side + 1:2 * n_side + 1]
    bias_ref, sink_rep_ref, vaug_ref, sa_ref, sb_ref = rest[2 * n_side + 1:]
    for x_ref, y_ref in zip(side_in, side_out):
        _cast_pad_cols_kernel(x_ref, y_ref)
    g = pl.program_id(1)
    ti = pl.program_id(2)
    w = WINDOW
    nsub = q_ref.shape[0] // w
    cs = HEAD_DIM ** -0.5 * LOG2E

    @pl.when(ti == 0)
    def _():
        i = lax.broadcasted_iota(jnp.int32, (w, 2 * w), 0)
        j = lax.broadcasted_iota(jnp.int32, (w, 2 * w), 1)
        dist = i + w - j
        band = (dist >= 0) & (dist < w)
        for hh in range(G_B):
            head = g * G_B + hh
            bias = jnp.where(band, _bias_from_dist(dist, lambda k: tab_ref[head, k]) * LOG2E, NEG)
            bias_ref[0, hh * w:(hh + 1) * w, :] = bias
            bias_ref[1, hh * w:(hh + 1) * w, :] = jnp.where(j >= w, bias, NEG)
            sink_rep_ref[hh * w:(hh + 1) * w, :] = jnp.full((w, LANE), sink_ref[head] * LOG2E, F32)
        _augment_values(v_ref, vaug_ref)

    def band_rows(ref, blk):
        k0 = pl.multiple_of(jnp.maximum(blk - 1, 0) * w, w)
        k1 = pl.multiple_of(blk * w, w)
        return jnp.concatenate([ref[pl.ds(k0, w), :], ref[pl.ds(k1, w), :]], axis=0)

    def logits(n):
        qs = jnp.concatenate([q_ref[n * w:(n + 1) * w, hh * HEAD_DIM:(hh + 1) * HEAD_DIM] for hh in range(G_B)], axis=0)
        return _dot_nt(qs, band_rows(k_ref, ti * nsub + n)) * cs

    def finish(s, n):
        blk = ti * nsub + n
        s = s + bias_ref[jnp.where(blk == 0, 1, 0)]
        sink = sink_rep_ref[...]
        m = jnp.maximum(_row_max_rep(s), sink)
        acc = _dot(_probs(s, m), band_rows(vaug_ref, blk))
        o = acc[:, :HEAD_DIM] / (acc[:, HEAD_DIM:] + jnp.exp2(sink - m))
        for hh in range(G_B):
            o_ref[n * w:(n + 1) * w, hh * HEAD_DIM:(hh + 1) * HEAD_DIM] = o[hh * w:(hh + 1) * w, :].astype(o_ref.dtype)

    bufs = (sa_ref, sb_ref)
    sa_ref[...] = logits(0)
    for n in range(nsub):
        if n + 1 < nsub:
            bufs[(n + 1) % 2][...] = logits(n + 1)
        finish(bufs[n % 2][...], n)


def _swa(proj, tab_b, sinks, cat, side=()):
    b, s, _ = proj.shape
    tq = SWA_ROWS
    nt = s // tq
    gw = G_B * HEAD_DIM
    qo = PROJ_OFFS[3] // gw
    oo = QA_W // gw
    ko, vo = (PROJ_OFFS[i] // HEAD_DIM for i in (4, 5))
    assert PROJ_OFFS[3] % gw == 0 and QA_W % gw == 0 and s % tq == 0
    smem = pl.BlockSpec(memory_space=pltpu.SMEM)
    side_specs = [_row_slab_specs(w, b * KV_B * nt, lambda bi, g, ti: (bi * KV_B + g) * nt + ti, w.shape[1], row0)
                  for w, row0 in side]
    side = [w for w, _ in side]
    return pl.pallas_call(
        functools.partial(_swa_kernel, len(side)),
        grid=(b, KV_B, nt),
        in_specs=[smem, smem,
                  pl.BlockSpec((None, tq, gw), lambda bi, g, ti: (bi, ti, qo + g)),
                  pl.BlockSpec((None, s, HEAD_DIM), lambda bi, g, ti: (bi, 0, ko + g)),
                  pl.BlockSpec((None, s, HEAD_DIM), lambda bi, g, ti: (bi, 0, vo + g)),
                  pl.BlockSpec(memory_space=pl.ANY)] + [sp[0] for sp in side_specs],
        out_specs=[pl.BlockSpec((None, tq, gw), lambda bi, g, ti: (bi, ti, oo + g))] + [sp[1] for sp in side_specs],
        out_shape=[jax.ShapeDtypeStruct(cat.shape, cat.dtype)] + [sp[2] for sp in side_specs],
        input_output_aliases={5: 0},
        scratch_shapes=[pltpu.VMEM((2, G_B * WINDOW, 2 * WINDOW), F32),
                        pltpu.VMEM((G_B * WINDOW, LANE), F32),
                        pltpu.VMEM((s, 2 * HEAD_DIM), BF16),
                        pltpu.VMEM((G_B * WINDOW, 2 * WINDOW), F32),
                        pltpu.VMEM((G_B * WINDOW, 2 * WINDOW), F32)],
        compiler_params=_params("parallel", "parallel", "arbitrary"),
        name="swa_attention",
    )(tab_b, sinks, proj, proj, proj, cat, *side)


def _diff_kernel(lam_init, n_cols, row_tiles, tab_ref, q_ref, k_ref, v_ref, lamp_ref, subg_ref, cat_ref, *rest):
    del cat_ref
    n_side = n_cols + (row_tiles is not None)
    side_in, o_ref, side_out = rest[:n_side], rest[n_side], rest[n_side + 1:2 * n_side + 1]
    bias_ref, vaug_ref, sa_ref, sb_ref, m_ref, acc_ref = rest[2 * n_side + 1:]
    for x_ref, y_ref in zip(side_in[:n_cols], side_out[:n_cols]):
        _cast_pad_cols_kernel(x_ref, y_ref)
    side_in, side_out = side_in[n_cols:], side_out[n_cols:]
    h = pl.program_id(0)
    bi = pl.program_id(1)
    qi = pl.program_id(2)
    t = q_ref.shape[0]
    cs = DIFF_DIM ** -0.5 * LOG2E

    if row_tiles is not None:
        n_src, n_dst, depth = row_tiles
        step = (h * pl.num_programs(1) + bi) * pl.num_programs(2) + qi

        @pl.when((step < depth * n_dst) & (step % n_dst < n_src))
        def _():
            side_out[0][...] = side_in[0][...].astype(BF16)

        @pl.when((step < depth * n_dst) & (step % n_dst >= n_src))
        def _():
            side_out[0][...] = jnp.zeros(side_out[0].shape, BF16)

    @pl.when((bi == 0) & (qi == 0))
    def _():
        _bias_tiles(lambda k: tab_ref[h, k], bias_ref)

    @pl.when(qi == 0)
    def _():
        _augment_values(v_ref, vaug_ref)

    q = q_ref[...]
    lane = lax.broadcasted_iota(jnp.int32, q.shape, 1)
    zero = jnp.zeros_like(q)
    q2 = jnp.concatenate([jnp.where(lane < DIFF_DIM, q, zero), jnp.where(lane >= DIFF_DIM, q, zero)], axis=0)

    def logits(kt):
        return _dot_nt(q2, k_ref[pl.ds(pl.multiple_of(kt * t, t), t), :]) * cs

    def update(s, left, kt):
        d = bias_ref[jnp.maximum(2 - left, 0)]
        s = jnp.concatenate([s[:t] + d, s[t:] + d], axis=0)
        _softmax_update(s, vaug_ref[pl.ds(pl.multiple_of(kt * t, t), t), :], m_ref, acc_ref)

    _softmax_reset(m_ref, acc_ref)
    _causal_tiles(qi, logits, update, sa_ref, sb_ref)

    lp = lamp_ref[...]
    lam = (jnp.exp(jnp.sum(lp[0:1] * lp[1:2], axis=1, keepdims=True))
           - jnp.exp(jnp.sum(lp[2:3] * lp[3:4], axis=1, keepdims=True)) + lam_init)
    acc = acc_ref[...]
    a = acc[:, :HEAD_DIM] / acc[:, HEAD_DIM:]
    o = a[:t] - lam * a[t:]
    o = o * lax.rsqrt(jnp.mean(o * o, axis=-1, keepdims=True) + LN_EPS) * subg_ref[...]
    o_ref[...] = (o * (1.0 - lam_init)).astype(o_ref.dtype)


def _diff(proj, tab_c, lam_p, subln_g, lam_init, cat, cast_cols=(), pad_rows=None):
    b, s, _ = proj.shape
    t = ATTN_TILE
    nq = s // t
    assert s % t == 0 and t >= FAR_DIST
    qo, ko, vo = (PROJ_OFFS[i] // HEAD_DIM for i in (6, 7, 8))
    oo = (QA_W + QB_W) // HEAD_DIM
    col_specs = [_row_slab_specs(w, H_C * b * nq, lambda h, bi, qi: (h * b + bi) * nq + qi, w.shape[1], row0)
                 for w, row0 in cast_cols]
    side = [w for w, _ in cast_cols]
    side_in, side_out, side_shape = ([sp[i] for sp in col_specs] for i in range(3))
    row_tiles = None
    if pad_rows is not None:
        w, k_pad = pad_rows
        depth, k, n = w.shape
        tr = CAST_ROW_TILE
        assert k % tr == 0 and k_pad % tr == 0
        n_src, n_dst = k // tr, k_pad // tr
        assert depth * n_dst <= H_C * b * nq
        row_tiles = (n_src, n_dst, depth)

        def dst_tile(h, bi, qi):
            return jnp.minimum((h * b + bi) * nq + qi, depth * n_dst - 1)

        def src_tile(h, bi, qi):
            d = dst_tile(h, bi, qi)
            return (d // n_dst) * n_src + jnp.minimum(d % n_dst, n_src - 1)

        side.append(w.reshape(depth * k, n))
        side_in.append(pl.BlockSpec((tr, n), lambda h, bi, qi: (src_tile(h, bi, qi), 0)))
        side_out.append(pl.BlockSpec((tr, n), lambda h, bi, qi: (dst_tile(h, bi, qi), 0)))
        side_shape.append(jax.ShapeDtypeStruct((depth * k_pad, n), BF16))
    return pl.pallas_call(
        functools.partial(_diff_kernel, lam_init, len(cast_cols), row_tiles),
        grid=(H_C, b, nq),
        in_specs=[pl.BlockSpec(memory_space=pltpu.SMEM),
                  pl.BlockSpec((None, t, HEAD_DIM), lambda h, bi, qi: (bi, qi, qo + h)),
                  pl.BlockSpec((None, s, HEAD_DIM), lambda h, bi, qi: (bi, 0, ko + h)),
                  pl.BlockSpec((None, s, HEAD_DIM), lambda h, bi, qi: (bi, 0, vo + h)),
                  pl.BlockSpec((4, DIFF_DIM), lambda h, bi, qi: (0, 0)),
                  pl.BlockSpec((1, HEAD_DIM), lambda h, bi, qi: (0, 0)),
                  pl.BlockSpec(memory_space=pl.ANY)] + side_in,
        out_specs=[pl.BlockSpec((None, t, HEAD_DIM), lambda h, bi, qi: (bi, qi, oo + h))] + side_out,
        out_shape=[jax.ShapeDtypeStruct(cat.shape, cat.dtype)] + side_shape,
        input_output_aliases={6: 0},
        scratch_shapes=[pltpu.VMEM((3, t, t), F32),
                        pltpu.VMEM((s, 2 * HEAD_DIM), BF16),
                        pltpu.VMEM((2 * t, t), F32), pltpu.VMEM((2 * t, t), F32),
                        pltpu.VMEM((2 * t, LANE), F32),
                        pltpu.VMEM((2 * t, 2 * HEAD_DIM), F32)],
        compiler_params=_params("arbitrary", "arbitrary", "arbitrary"),
        name="diff_attention",
    )(tab_c, proj, proj, proj, lam_p, subln_g, cat, *side)


def kernel(x, c, rel_bias, w_ada, b_ada, w_in, w_o, attn_sinks, diff_lambda, diff_subln_g,
           ln_g, ln_b, w_gate, w_up, w_down):
    b, s, d = x.shape
    m = b * s
    tab = rel_bias.T
    tab_a, tab_b, tab_c = tab[:H_A], tab[H_A:H_A + H_B], tab[H_A + H_B:]

    c8 = jnp.pad(c, ((0, 8 - b), (0, 0)))
    mod = _ada_mod(c8, w_ada, b_ada[:, None, :])[:, :b]
    mod = mod.reshape(DEPTH, b, 6, 1, d)

    w_in_b = [w_in[:1].astype(BF16), None]
    h = None
    for l in range(DEPTH):
        sh1, sc1, g1, sh2, sc2, g2 = (mod[l, :, i] for i in range(6))
        if h is None:
            proj = _modulate_matmul(x, sc1, sh1, w_in_b[l], 0, BF16, *TILES_PROJ_IN, "proj_in")
        else:
            proj = _matmul(h.reshape(m, d), w_in_b[l], 0, BF16, *TILES_PROJ_IN, "proj_in")
        proj = proj.reshape(b, s, D_PROJ)
        lam_init = 0.8 - 0.6 * math.exp(-0.3 * l)
        if l == 0:
            assert DEPTH == 2
            cat, wg_b, wu_b = _moba(proj, tab_a, (w_gate.reshape(DEPTH * d, D_FF), w_up.reshape(DEPTH * d, D_FF)))
            cat, w_o_b = _swa(proj, tab_b, attn_sinks[l], cat, ((w_o.reshape(DEPTH * d, d), 0),))
            cat, w_in1_b, wd_b = _diff(proj, tab_c, diff_lambda[l], diff_subln_g[l][None, :], lam_init, cat,
                                       ((w_in.reshape(DEPTH * d, D_PROJ), d),), (w_down, D_FF))
            wg_b, wu_b = wg_b.reshape(DEPTH, d, D_FF), wu_b.reshape(DEPTH, d, D_FF)
            w_o_b, wd_b = w_o_b.reshape(DEPTH, d, d), wd_b.reshape(DEPTH, D_FF, d)
            w_in_b[1] = w_in1_b[None]
        else:
            (cat,) = _moba(proj, tab_a)
            (cat,) = _swa(proj, tab_b, attn_sinks[l], cat)
            (cat,) = _diff(proj, tab_c, diff_lambda[l], diff_subln_g[l][None, :], lam_init, cat)
        mix = _matmul(cat.reshape(m, d), w_o_b, l, BF16, *TILES_PROJ_OUT, "proj_out").reshape(b, s, d)
        x, h = _residual_norm(x, mix, g1, ln_g[l, 0][None, :], ln_b[l, 0][None, :], sc2, sh2)

        act = _gate_up(h.reshape(m, d), wg_b, wu_b, l, *TILES_GATE_UP)
        ff = _matmul_ksplit(act, wd_b, l, BF16, *TILES_DOWN, "ffn_down").reshape(b, s, d)
        if l + 1 < DEPTH:
            nsh, nsc = mod[l + 1, :, 0], mod[l + 1, :, 1]
            x, h = _residual_norm(x, ff, g2, ln_g[l, 1][None, :], ln_b[l, 1][None, :], nsc, nsh)
        else:
            (x,) = _residual_norm(x, ff, g2, ln_g[l, 1][None, :], ln_b[l, 1][None, :])
    return x
```

```python
import functools
import math

import numpy as np
import jax
import jax.numpy as jnp
from jax import lax
from jax.experimental import pallas as pl
from jax.experimental.pallas import tpu as pltpu

D_MODEL = 4096
DEPTH = 2
HEAD_DIM = 128
N_HEADS = D_MODEL // HEAD_DIM
H_A = N_HEADS // 4
H_C = N_HEADS // 4
H_B = N_HEADS - H_A - H_C
KV_B = max(1, H_B // 8)
G_B = H_B // KV_B
DIFF_DIM = HEAD_DIM // 2
MOBA_BLOCK = 256
MOBA_TOPK = 3
WINDOW = 128
N_BUCKETS = 32
MAX_DISTANCE = 128
D_FF = -(-8 * D_MODEL // (3 * 256)) * 256
ALPHA = (2.0 * DEPTH) ** 0.25
LN_EPS = 1e-5
NEG = -1e30
LOG2E = math.log2(math.e)

QA_W = H_A * HEAD_DIM
QB_W = H_B * HEAD_DIM
KB_W = KV_B * HEAD_DIM
QC_W = H_C * 2 * DIFF_DIM
VC_W = H_C * HEAD_DIM
PROJ_SIZES = (QA_W, QA_W, QA_W, QB_W, KB_W, KB_W, QC_W, QC_W, VC_W)
PROJ_OFFS = tuple(int(sum(PROJ_SIZES[:i])) for i in range(len(PROJ_SIZES)))
D_PROJ = sum(PROJ_SIZES)

LANE = 128
ATTN_TILE = 512
TILE_UNROLL = 4
CAST_ROW_TILE = 256
VMEM_LIMIT = 56 * 1024 * 1024
TILES_PROJ_IN = (1024, 512)
TILES_PROJ_OUT = (1024, 1024)
TILES_GATE_UP = (2048, 256)
TILES_DOWN = (1024, 1024, D_FF // 2)
NORM_ROWS = 256
ADA_COLS = 512
SWA_ROWS = 1024

F32 = jnp.float32
BF16 = jnp.bfloat16


def _bucket_upper_bounds():
    n = np.arange(0, 4 * MAX_DISTANCE, dtype=np.int64)
    max_exact = N_BUCKETS // 2
    nf = np.maximum(n, 1).astype(np.float32)
    large = max_exact + (np.log(nf / np.float32(max_exact)) / np.float32(math.log(MAX_DISTANCE / max_exact))
                         * np.float32(N_BUCKETS - max_exact)).astype(np.int32)
    large = np.minimum(large, N_BUCKETS - 1)
    bucket = np.where(n < max_exact, n, large)
    assert bucket[-1] == N_BUCKETS - 1 and np.all(np.diff(bucket) >= 0)
    return tuple(int(np.argmax(bucket > k)) for k in range(N_BUCKETS - 1))


BUCKET_UPPER = _bucket_upper_bounds()
FAR_DIST = BUCKET_UPPER[-1]


def _bias_from_dist(dist, tab):
    b = jnp.full(dist.shape, tab(N_BUCKETS - 1), F32)
    for k in range(N_BUCKETS - 2, -1, -1):
        b = jnp.where(dist < BUCKET_UPPER[k], tab(k), b)
    return b


def _dot(a, b):
    return jnp.dot(a, b, preferred_element_type=F32)


def _dot_nt(a, b):
    return lax.dot_general(a, b, (((1,), (1,)), ((), ())), preferred_element_type=F32)


def _params(*sem):
    return pltpu.CompilerParams(dimension_semantics=sem, vmem_limit_bytes=VMEM_LIMIT)


def _mod_kernel(c_ref, w_ref, b_ref, o_ref):
    c = c_ref[...]
    s = (c * jax.nn.sigmoid(c)).astype(BF16)
    o_ref[...] = _dot(s, w_ref[...].astype(BF16)) + b_ref[...]


def _ada_mod(c8, w_ada, b_ada3):
    depth, d, n = w_ada.shape
    tn = ADA_COLS
    return pl.pallas_call(
        _mod_kernel,
        grid=(depth, n // tn),
        in_specs=[pl.BlockSpec((8, d), lambda l, j: (0, 0)),
                  pl.BlockSpec((None, d, tn), lambda l, j: (l, 0, j)),
                  pl.BlockSpec((None, 1, tn), lambda l, j: (l, 0, j))],
        out_specs=pl.BlockSpec((None, 8, tn), lambda l, j: (l, 0, j)),
        out_shape=jax.ShapeDtypeStruct((depth, 8, n), F32),
        compiler_params=_params("parallel", "parallel"),
        name="ada_mod",
    )(c8, w_ada, b_ada3)


def _mm_kernel(x_ref, w_ref, o_ref):
    o_ref[...] = _dot(x_ref[...], w_ref[...]).astype(o_ref.dtype)


def _matmul(x, w, l, out_dtype, tm, tn, name):
    m, k = x.shape
    _, _, n = w.shape
    return pl.pallas_call(
        _mm_kernel,
        grid=(m // tm, n // tn),
        in_specs=[pl.BlockSpec((tm, k), lambda i, j: (i, 0)),
                  pl.BlockSpec((None, k, tn), lambda i, j: (l, 0, j))],
        out_specs=pl.BlockSpec((tm, tn), lambda i, j: (i, j)),
        out_shape=jax.ShapeDtypeStruct((m, n), out_dtype),
        compiler_params=_params("parallel", "arbitrary"),
        name=name,
    )(x, w)


def _mod_mm_kernel(x_ref, sc_ref, sh_ref, w_ref, o_ref, h_ref):
    @pl.when(pl.program_id(1) == 0)
    def _():
        h_ref[...] = (x_ref[...] * (1.0 + sc_ref[...]) + sh_ref[...]).astype(BF16)

    o_ref[...] = _dot(h_ref[...], w_ref[...]).astype(o_ref.dtype)


def _modulate_matmul(x, sc, sh, w, l, out_dtype, tm, tn, name):
    b, s, k = x.shape
    _, _, n = w.shape
    assert s % tm == 0
    per_b = s // tm
    vec = pl.BlockSpec((None, 1, k), lambda i, j: (i // per_b, 0, 0))
    return pl.pallas_call(
        _mod_mm_kernel,
        grid=(b * per_b, n // tn),
        in_specs=[pl.BlockSpec((tm, k), lambda i, j: (i, 0)), vec, vec,
                  pl.BlockSpec((None, k, tn), lambda i, j: (l, 0, j))],
        out_specs=pl.BlockSpec((tm, tn), lambda i, j: (i, j)),
        out_shape=jax.ShapeDtypeStruct((b * s, n), out_dtype),
        scratch_shapes=[pltpu.VMEM((tm, k), BF16)],
        compiler_params=_params("parallel", "arbitrary"),
        name=name,
    )(x.reshape(b * s, k), sc, sh, w)


def _mm_acc_kernel(x_ref, w_ref, o_ref, acc_ref):
    kk = pl.program_id(2)
    part = _dot(x_ref[...], w_ref[...])

    @pl.when(kk == 0)
    def _():
        acc_ref[...] = part

    @pl.when((kk > 0) & (kk < pl.num_programs(2) - 1))
    def _():
        acc_ref[...] += part

    @pl.when(kk == pl.num_programs(2) - 1)
    def _():
        o_ref[...] = (acc_ref[...] + part).astype(o_ref.dtype)


def _matmul_ksplit(x, w, l, out_dtype, tm, tn, tk, name):
    m, k = x.shape
    _, _, n = w.shape
    assert k // tk >= 2
    return pl.pallas_call(
        _mm_acc_kernel,
        grid=(m // tm, n // tn, k // tk),
        in_specs=[pl.BlockSpec((tm, tk), lambda i, j, kk: (i, kk)),
                  pl.BlockSpec((None, tk, tn), lambda i, j, kk: (l, kk, j))],
        out_specs=pl.BlockSpec((tm, tn), lambda i, j, kk: (i, j)),
        out_shape=jax.ShapeDtypeStruct((m, n), out_dtype),
        scratch_shapes=[pltpu.VMEM((tm, tn), F32)],
        compiler_params=_params("parallel", "parallel", "arbitrary"),
        name=name,
    )(x, w)


def _cast_pad_cols_kernel(x_ref, o_ref):
    n = x_ref.shape[1]
    o_ref[:, :n] = x_ref[...].astype(BF16)
    if o_ref.shape[1] > n:
        o_ref[:, n:] = jnp.zeros((o_ref.shape[0], o_ref.shape[1] - n), BF16)


def _gate_up_kernel(h_ref, wg_ref, wu_ref, o_ref):
    h = h_ref[...]
    g = _dot(h, wg_ref[...])
    u = _dot(h, wu_ref[...])
    o_ref[...] = (g * jax.nn.sigmoid(g) * u).astype(BF16)


def _gate_up(h, wg, wu, l, tm, tf):
    m, k = h.shape
    _, _, f = wg.shape
    wspec = pl.BlockSpec((None, k, tf), lambda i, j: (l, 0, j))
    return pl.pallas_call(
        _gate_up_kernel,
        grid=(m // tm, f // tf),
        in_specs=[pl.BlockSpec((tm, k), lambda i, j: (i, 0)), wspec, wspec],
        out_specs=pl.BlockSpec((tm, tf), lambda i, j: (i, j)),
        out_shape=jax.ShapeDtypeStruct((m, f), BF16),
        compiler_params=_params("parallel", "arbitrary"),
        name="ffn_gate_up",
    )(h, wg, wu)


def _norm_kernel(with_h, x_ref, y_ref, gate_ref, lg_ref, lb_ref, *rest):
    z = ALPHA * x_ref[...] + gate_ref[...] * y_ref[...].astype(F32)
    mu = jnp.mean(z, axis=-1, keepdims=True)
    zc = z - mu
    var = jnp.mean(zc * zc, axis=-1, keepdims=True)
    xn = zc * lax.rsqrt(var + LN_EPS) * lg_ref[...] + lb_ref[...]
    if with_h:
        sc_ref, sh_ref, xo_ref, h_ref = rest
        xo_ref[...] = xn
        h_ref[...] = (xn * (1.0 + sc_ref[...]) + sh_ref[...]).astype(BF16)
    else:
        (xo_ref,) = rest
        xo_ref[...] = xn


def _residual_norm(x, y, gate, lg, lb, sc=None, sh=None):
    b, s, d = x.shape
    tm = NORM_ROWS
    with_h = sc is not None
    tile = pl.BlockSpec((None, tm, d), lambda i, j: (i, j, 0))
    vec = pl.BlockSpec((None, 1, d), lambda i, j: (i, 0, 0))
    par = pl.BlockSpec((1, d), lambda i, j: (0, 0))
    in_specs = [tile, tile, vec, par, par]
    args = [x, y, gate, lg, lb]
    out_specs = [tile]
    out_shape = [jax.ShapeDtypeStruct((b, s, d), F32)]
    if with_h:
        in_specs += [vec, vec]
        args += [sc, sh]
        out_specs.append(tile)
        out_shape.append(jax.ShapeDtypeStruct((b, s, d), BF16))
    return pl.pallas_call(
        functools.partial(_norm_kernel, with_h),
        grid=(b, s // tm),
        in_specs=in_specs,
        out_specs=out_specs,
        out_shape=out_shape,
        compiler_params=_params("parallel", "parallel"),
        name="residual_norm",
    )(*args)


def _lane_tiles(a):
    return [a[:, i * LANE:(i + 1) * LANE] for i in range(a.shape[1] // LANE)]


def _row_max_rep(s):
    tiles = _lane_tiles(s)
    t = tiles[0]
    for u in tiles[1:]:
        t = jnp.maximum(t, u)
    return jnp.broadcast_to(jnp.max(t, axis=1, keepdims=True), t.shape)


def _probs(s, m_rep):
    return jnp.concatenate([jnp.exp2(u - m_rep) for u in _lane_tiles(s)], axis=1).astype(BF16)


def _softmax_reset(m_ref, acc_ref):
    m_ref[...] = jnp.full(m_ref.shape, NEG, F32)
    acc_ref[...] = jnp.zeros(acc_ref.shape, F32)


def _softmax_update(s, v_aug, m_ref, acc_ref):
    m_old = m_ref[...]
    m = jnp.maximum(m_old, _row_max_rep(s))
    corr = jnp.exp2(m_old - m)
    m_ref[...] = m
    acc_ref[...] = jnp.concatenate([corr, corr], axis=1) * acc_ref[...] + _dot(_probs(s, m), v_aug)


def _augment_values(v_ref, vaug_ref):
    vaug_ref[:, :HEAD_DIM] = v_ref[...]
    vaug_ref[:, HEAD_DIM:] = jnp.ones((v_ref.shape[0], HEAD_DIM), BF16)


def _bias_tiles(tab, bias_ref):
    t = bias_ref.shape[1]
    i = lax.broadcasted_iota(jnp.int32, (t, t), 0)
    j = lax.broadcasted_iota(jnp.int32, (t, t), 1)
    far = tab(N_BUCKETS - 1)
    d0 = (_bias_from_dist(i - j, tab) - far) * LOG2E
    bias_ref[2] = jnp.where(i >= j, d0, NEG)
    bias_ref[1] = (_bias_from_dist(i - j + t, tab) - far) * LOG2E
    bias_ref[0] = jnp.zeros((t, t), F32)


def _causal_tiles(qi, logits_fn, update_fn, sa_ref, sb_ref):
    n = qi + 1
    bufs = (sa_ref, sb_ref)
    sa_ref[...] = logits_fn(qi)

    def group(left0, size):
        for i in range(size):
            cur, nxt = bufs[i % 2], bufs[(i + 1) % 2]
            kt = qi - (left0 + i)
            nxt[...] = logits_fn(jnp.maximum(kt - 1, 0))
            update_fn(cur[...], left0 + i, kt)

    def quad(p, carry):
        group(TILE_UNROLL * p, TILE_UNROLL)
        return carry

    quads = n // TILE_UNROLL
    lax.fori_loop(0, quads, quad, 0)
    rest = n - TILE_UNROLL * quads

    @pl.when(rest >= 2)
    def _():
        group(TILE_UNROLL * quads, 2)

    @pl.when(rest % 2 == 1)
    def _():
        update_fn(sa_ref[...], qi, 0)


def _moba_kernel(n_side, tab_ref, q_ref, k_ref, v_ref, *rest):
    side_in, o_ref, side_out = rest[:n_side], rest[n_side], rest[n_side + 1:2 * n_side + 1]
    kmean_ref, bias_ref, vaug_ref, sa_ref, sb_ref, m_ref, acc_ref = rest[2 * n_side + 1:]
    for x_ref, y_ref in zip(side_in, side_out):
        _cast_pad_cols_kernel(x_ref, y_ref)
    h = pl.program_id(0)
    bi = pl.program_id(1)
    pj = pl.program_id(2)
    t = ATTN_TILE
    nq = q_ref.shape[0] // t
    blk = MOBA_BLOCK
    bpt = t // blk
    nb = k_ref.shape[0] // blk
    cs = HEAD_DIM ** -0.5 * LOG2E

    @pl.when((bi == 0) & (pj == 0))
    def _():
        _bias_tiles(lambda k: tab_ref[h, k], bias_ref)

    @pl.when(pj == 0)
    def _():
        for jb in range(nb):
            kb = k_ref[jb * blk:(jb + 1) * blk, :].astype(F32)
            kmean_ref[jb:jb + 1, :] = jnp.sum(kb, axis=0, keepdims=True) * (1.0 / blk)
        _augment_values(v_ref, vaug_ref)

    def query_tile(j, carry):
        qi = jnp.where(j == 0, pj, nq - 1 - pj)
        rows = pl.ds(pl.multiple_of(qi * t, t), t)
        q = q_ref[rows, :]
        km = kmean_ref[...]
        km_hi = km.astype(BF16)
        km_lo = (km - km_hi.astype(F32)).astype(BF16)
        gate = _dot_nt(km_hi, q) + _dot_nt(km_lo, q)
        jblk = lax.broadcasted_iota(jnp.int32, gate.shape, 0)
        qpos = lax.broadcasted_iota(jnp.int32, (1, t), 1)
        qblk = qi * bpt + qpos // blk
        g = jnp.where(jblk < qblk, gate, -jnp.inf)
        bits = jnp.left_shift(1, qblk)
        for r in range(MOBA_TOPK):
            mx = jnp.max(g, axis=0, keepdims=True)
            first = jnp.min(jnp.where(g == mx, jblk, nb), axis=0, keepdims=True)
            bits = bits | jnp.where(r < qblk, jnp.left_shift(1, first), 0)
            g = jnp.where(jblk == first, -jnp.inf, g)
        bits = jnp.broadcast_to(bits.astype(F32), (LANE, t)).T.astype(jnp.int32)

        def logits(kt):
            return _dot_nt(q, k_ref[pl.ds(pl.multiple_of(kt * t, t), t), :]) * cs

        def update(s, left, kt):
            s = s + bias_ref[jnp.maximum(2 - left, 0)]
            tiles = _lane_tiles(s)
            per_blk = blk // LANE
            out = []
            for c in range(bpt):
                vis = (jnp.right_shift(bits, kt * bpt + c) & 1) == 1
                out += [jnp.where(vis, u, NEG) for u in tiles[c * per_blk:(c + 1) * per_blk]]
            s = jnp.concatenate(out, axis=1)
            _softmax_update(s, vaug_ref[pl.ds(pl.multiple_of(kt * t, t), t), :], m_ref, acc_ref)

        _softmax_reset(m_ref, acc_ref)
        _causal_tiles(qi, logits, update, sa_ref, sb_ref)
        acc = acc_ref[...]
        o_ref[rows, :] = (acc[:, :HEAD_DIM] / acc[:, HEAD_DIM:]).astype(o_ref.dtype)
        return carry

    lax.fori_loop(0, 2, query_tile, 0)


def _row_slab_specs(w, n_steps, step_fn, n_pad, row0=0):
    rows, n = w.shape[0] - row0, w.shape[1]
    assert rows % n_steps == 0 and (rows // n_steps) % 16 == 0 and row0 % (rows // n_steps) == 0
    tr = rows // n_steps
    first = row0 // tr
    return (pl.BlockSpec((tr, n), lambda *g: (first + step_fn(*g), 0)),
            pl.BlockSpec((tr, n_pad), lambda *g: (step_fn(*g), 0)),
            jax.ShapeDtypeStruct((rows, n_pad), BF16))


def _moba(proj, tab_a, side=()):
    b, s, _ = proj.shape
    t = ATTN_TILE
    assert s % (2 * t) == 0 and t % MOBA_BLOCK == 0 and t >= FAR_DIST and s // MOBA_BLOCK < 24
    nq = s // (2 * t)
    qo, ko, vo = (PROJ_OFFS[i] // HEAD_DIM for i in (0, 1, 2))
    side_specs = [_row_slab_specs(w, H_A * b * nq, lambda h, bi, qi: (h * b + bi) * nq + qi, w.shape[1]) for w in side]
    return pl.pallas_call(
        functools.partial(_moba_kernel, len(side)),
        grid=(H_A, b, nq),
        in_specs=[pl.BlockSpec(memory_space=pltpu.SMEM),
                  pl.BlockSpec((None, s, HEAD_DIM), lambda h, bi, qi: (bi, 0, qo + h)),
                  pl.BlockSpec((None, s, HEAD_DIM), lambda h, bi, qi: (bi, 0, ko + h)),
                  pl.BlockSpec((None, s, HEAD_DIM), lambda h, bi, qi: (bi, 0, vo + h))]
                 + [sp[0] for sp in side_specs],
        out_specs=[pl.BlockSpec((None, s, HEAD_DIM), lambda h, bi, qi: (bi, 0, h))] + [sp[1] for sp in side_specs],
        out_shape=[jax.ShapeDtypeStruct((b, s, D_MODEL), BF16)]
                  + [sp[2] for sp in side_specs],
        scratch_shapes=[pltpu.VMEM((s // MOBA_BLOCK, HEAD_DIM), F32),
                        pltpu.VMEM((3, t, t), F32),
                        pltpu.VMEM((s, 2 * HEAD_DIM), BF16),
                        pltpu.VMEM((t, t), F32), pltpu.VMEM((t, t), F32),
                        pltpu.VMEM((t, LANE), F32),
                        pltpu.VMEM((t, 2 * HEAD_DIM), F32)],
        compiler_params=_params("arbitrary", "arbitrary", "arbitrary"),
        name="moba_attention",
    )(tab_a, proj, proj, proj, *side)


def _swa_kernel(n_side, tab_ref, sink_ref, q_ref, k_ref, v_ref, cat_ref, *rest):
    del cat_ref
    side_in, o_ref, side_out = rest[:n_side], rest[n_side], rest[n_side + 1:2 * n_side + 1]
    bias_ref, sink_rep_ref, vaug_ref, sa_ref, sb_ref = rest[2 * n_side + 1:]
    for x_ref, y_ref in zip(side_in, side_out):
        _cast_pad_cols_kernel(x_ref, y_ref)
    g = pl.program_id(1)
    ti = pl.program_id(2)
    w = WINDOW
    nsub = q_ref.shape[0] // w
    cs = HEAD_DIM ** -0.5 * LOG2E

    @pl.when(ti == 0)
    def _():
        i = lax.broadcasted_iota(jnp.int32, (w, 2 * w), 0)
        j = lax.broadcasted_iota(jnp.int32, (w, 2 * w), 1)
        dist = i + w - j
        band = (dist >= 0) & (dist < w)
        for hh in range(G_B):
            head = g * G_B + hh
            bias = jnp.where(band, _bias_from_dist(dist, lambda k: tab_ref[head, k]) * LOG2E, NEG)
            bias_ref[0, hh * w:(hh + 1) * w, :] = bias
            bias_ref[1, hh * w:(hh + 1) * w, :] = jnp.where(j >= w, bias, NEG)
            sink_rep_ref[hh * w:(hh + 1) * w, :] = jnp.full((w, LANE), sink_ref[head] * LOG2E, F32)
        _augment_values(v_ref, vaug_ref)

    def band_rows(ref, blk):
        k0 = pl.multiple_of(jnp.maximum(blk - 1, 0) * w, w)
        k1 = pl.multiple_of(blk * w, w)
        return jnp.concatenate([ref[pl.ds(k0, w), :], ref[pl.ds(k1, w), :]], axis=0)

    def logits(n):
        qs = jnp.concatenate([q_ref[n * w:(n + 1) * w, hh * HEAD_DIM:(hh + 1) * HEAD_DIM] for hh in range(G_B)], axis=0)
        return _dot_nt(qs, band_rows(k_ref, ti * nsub + n)) * cs

    def finish(s, n):
        blk = ti * nsub + n
        s = s + bias_ref[jnp.where(blk == 0, 1, 0)]
        sink = sink_rep_ref[...]
        m = jnp.maximum(_row_max_rep(s), sink)
        acc = _dot(_probs(s, m), band_rows(vaug_ref, blk))
        o = acc[:, :HEAD_DIM] / (acc[:, HEAD_DIM:] + jnp.exp2(sink - m))
        for hh in range(G_B):
            o_ref[n * w:(n + 1) * w, hh * HEAD_DIM:(hh + 1) * HEAD_DIM] = o[hh * w:(hh + 1) * w, :].astype(o_ref.dtype)

    bufs = (sa_ref, sb_ref)
    sa_ref[...] = logits(0)
    for n in range(nsub):
        if n + 1 < nsub:
            bufs[(n + 1) % 2][...] = logits(n + 1)
        finish(bufs[n % 2][...], n)


def _swa(proj, tab_b, sinks, cat, side=()):
    b, s, _ = proj.shape
    tq = SWA_ROWS
    nt = s // tq
    gw = G_B * HEAD_DIM
    qo = PROJ_OFFS[3] // gw
    oo = QA_W // gw
    ko, vo = (PROJ_OFFS[i] // HEAD_DIM for i in (4, 5))
    assert PROJ_OFFS[3] % gw == 0 and QA_W % gw == 0 and s % tq == 0
    smem = pl.BlockSpec(memory_space=pltpu.SMEM)
    side_specs = [_row_slab_specs(w, b * KV_B * nt, lambda bi, g, ti: (bi * KV_B + g) * nt + ti, w.shape[1], row0)
                  for w, row0 in side]
    side = [w for w, _ in side]
    return pl.pallas_call(
        functools.partial(_swa_kernel, len(side)),
        grid=(b, KV_B, nt),
        in_specs=[smem, smem,
                  pl.BlockSpec((None, tq, gw), lambda bi, g, ti: (bi, ti, qo + g)),
                  pl.BlockSpec((None, s, HEAD_DIM), lambda bi, g, ti: (bi, 0, ko + g)),
                  pl.BlockSpec((None, s, HEAD_DIM), lambda bi, g, ti: (bi, 0, vo + g)),
                  pl.BlockSpec(memory_space=pl.ANY)] + [sp[0] for sp in side_specs],
        out_specs=[pl.BlockSpec((None, tq, gw), lambda bi, g, ti: (bi, ti, oo + g))] + [sp[1] for sp in side_specs],
        out_shape=[jax.ShapeDtypeStruct(cat.shape, cat.dtype)] + [sp[2] for sp in side_specs],
        input_output_aliases={5: 0},
        scratch_shapes=[pltpu.VMEM((2, G_B * WINDOW, 2 * WINDOW), F32),
                        pltpu.VMEM((G_B * WINDOW, LANE), F32),
                        pltpu.VMEM((s, 2 * HEAD_DIM), BF16),
                        pltpu.VMEM((G_B * WINDOW, 2 * WINDOW), F32),
                        pltpu.VMEM((G_B * WINDOW, 2 * WINDOW), F32)],
        compiler_params=_params("parallel", "parallel", "arbitrary"),
        name="swa_attention",
    )(tab_b, sinks, proj, proj, proj, cat, *side)


def _diff_kernel(lam_init, n_cols, row_tiles, tab_ref, q_ref, k_ref, v_ref, lamp_ref, subg_ref, cat_ref, *rest):
    del cat_ref
    n_side = n_cols + (row_tiles is not None)
    side_in, o_ref, side_out = rest[:n_side], rest[n_side], rest[n_side + 1:2 * n_side + 1]
    bias_ref, vaug_ref, sa_ref, sb_ref, m_ref, acc_ref = rest[2 * n_side + 1:]
    for x_ref, y_ref in zip(side_in[:n_cols], side_out[:n_cols]):
        _cast_pad_cols_kernel(x_ref, y_ref)
    side_in, side_out = side_in[n_cols:], side_out[n_cols:]
    h = pl.program_id(0)
    bi = pl.program_id(1)
    pj = pl.program_id(2)
    t = ATTN_TILE
    nq = q_ref.shape[0] // t
    cs = DIFF_DIM ** -0.5 * LOG2E

    if row_tiles is not None:
        n_src, n_dst, depth = row_tiles
        step = (h * pl.num_programs(1) + bi) * pl.num_programs(2) + pj

        @pl.when((step < depth * n_dst) & (step % n_dst < n_src))
        def _():
            side_out[0][...] = side_in[0][...].astype(BF16)

        @pl.when((step < depth * n_dst) & (step % n_dst >= n_src))
        def _():
            side_out[0][...] = jnp.zeros(side_out[0].shape, BF16)

    @pl.when((bi == 0) & (pj == 0))
    def _():
        _bias_tiles(lambda k: tab_ref[h, k], bias_ref)

    @pl.when(pj == 0)
    def _():
        _augment_values(v_ref, vaug_ref)

    lp = lamp_ref[...]
    lam = (jnp.exp(jnp.sum(lp[0:1] * lp[1:2], axis=1, keepdims=True))
           - jnp.exp(jnp.sum(lp[2:3] * lp[3:4], axis=1, keepdims=True)) + lam_init)

    def query_tile(j, carry):
        qi = jnp.where(j == 0, pj, nq - 1 - pj)
        rows = pl.ds(pl.multiple_of(qi * t, t), t)
        q = q_ref[rows, :]
        lane = lax.broadcasted_iota(jnp.int32, q.shape, 1)
        zero = jnp.zeros_like(q)
        q2 = jnp.concatenate([jnp.where(lane < DIFF_DIM, q, zero), jnp.where(lane >= DIFF_DIM, q, zero)], axis=0)

        def logits(kt):
            return _dot_nt(q2, k_ref[pl.ds(pl.multiple_of(kt * t, t), t), :]) * cs

        def update(s, left, kt):
            d = bias_ref[jnp.maximum(2 - left, 0)]
            s = jnp.concatenate([s[:t] + d, s[t:] + d], axis=0)
            _softmax_update(s, vaug_ref[pl.ds(pl.multiple_of(kt * t, t), t), :], m_ref, acc_ref)

        _softmax_reset(m_ref, acc_ref)
        _causal_tiles(qi, logits, update, sa_ref, sb_ref)

        acc = acc_ref[...]
        a = acc[:, :HEAD_DIM] / acc[:, HEAD_DIM:]
        o = a[:t] - lam * a[t:]
        o = o * lax.rsqrt(jnp.mean(o * o, axis=-1, keepdims=True) + LN_EPS) * subg_ref[...]
        o_ref[rows, :] = (o * (1.0 - lam_init)).astype(o_ref.dtype)
        return carry

    lax.fori_loop(0, 2, query_tile, 0)


def _diff(proj, tab_c, lam_p, subln_g, lam_init, cat, cast_cols=(), pad_rows=None):
    b, s, _ = proj.shape
    t = ATTN_TILE
    assert s % (2 * t) == 0 and t >= FAR_DIST
    nq = s // (2 * t)
    qo, ko, vo = (PROJ_OFFS[i] // HEAD_DIM for i in (6, 7, 8))
    oo = (QA_W + QB_W) // HEAD_DIM
    col_specs = [_row_slab_specs(w, H_C * b * nq, lambda h, bi, qi: (h * b + bi) * nq + qi, w.shape[1], row0)
                 for w, row0 in cast_cols]
    side = [w for w, _ in cast_cols]
    side_in, side_out, side_shape = ([sp[i] for sp in col_specs] for i in range(3))
    row_tiles = None
    if pad_rows is not None:
        w, k_pad = pad_rows
        depth, k, n = w.shape
        tr = CAST_ROW_TILE
        assert k % tr == 0 and k_pad % tr == 0
        n_src, n_dst = k // tr, k_pad // tr
        assert depth * n_dst <= H_C * b * nq
        row_tiles = (n_src, n_dst, depth)

        def dst_tile(h, bi, qi):
            return jnp.minimum((h * b + bi) * nq + qi, depth * n_dst - 1)

        def src_tile(h, bi, qi):
            d = dst_tile(h, bi, qi)
            return (d // n_dst) * n_src + jnp.minimum(d % n_dst, n_src - 1)

        side.append(w.reshape(depth * k, n))
        side_in.append(pl.BlockSpec((tr, n), lambda h, bi, qi: (src_tile(h, bi, qi), 0)))
        side_out.append(pl.BlockSpec((tr, n), lambda h, bi, qi: (dst_tile(h, bi, qi), 0)))
        side_shape.append(jax.ShapeDtypeStruct((depth * k_pad, n), BF16))
    return pl.pallas_call(
        functools.partial(_diff_kernel, lam_init, len(cast_cols), row_tiles),
        grid=(H_C, b, nq),
        in_specs=[pl.BlockSpec(memory_space=pltpu.SMEM),
                  pl.BlockSpec((None, s, HEAD_DIM), lambda h, bi, qi: (bi, 0, qo + h)),
                  pl.BlockSpec((None, s, HEAD_DIM), lambda h, bi, qi: (bi, 0, ko + h)),
                  pl.BlockSpec((None, s, HEAD_DIM), lambda h, bi, qi: (bi, 0, vo + h)),
                  pl.BlockSpec((4, DIFF_DIM), lambda h, bi, qi: (0, 0)),
                  pl.BlockSpec((1, HEAD_DIM), lambda h, bi, qi: (0, 0)),
                  pl.BlockSpec(memory_space=pl.ANY)] + side_in,
        out_specs=[pl.BlockSpec((None, s, HEAD_DIM), lambda h, bi, qi: (bi, 0, oo + h))] + side_out,
        out_shape=[jax.ShapeDtypeStruct(cat.shape, cat.dtype)] + side_shape,
        input_output_aliases={6: 0},
        scratch_shapes=[pltpu.VMEM((3, t, t), F32),
                        pltpu.VMEM((s, 2 * HEAD_DIM), BF16),
                        pltpu.VMEM((2 * t, t), F32), pltpu.VMEM((2 * t, t), F32),
                        pltpu.VMEM((2 * t, LANE), F32),
                        pltpu.VMEM((2 * t, 2 * HEAD_DIM), F32)],
        compiler_params=_params("arbitrary", "arbitrary", "arbitrary"),
        name="diff_attention",
    )(tab_c, proj, proj, proj, lam_p, subln_g, cat, *side)


def kernel(x, c, rel_bias, w_ada, b_ada, w_in, w_o, attn_sinks, diff_lambda, diff_subln_g,
           ln_g, ln_b, w_gate, w_up, w_down):
    b, s, d = x.shape
    m = b * s
    tab = rel_bias.T
    tab_a, tab_b, tab_c = tab[:H_A], tab[H_A:H_A + H_B], tab[H_A + H_B:]

    c8 = jnp.pad(c, ((0, 8 - b), (0, 0)))
    mod = _ada_mod(c8, w_ada, b_ada[:, None, :])[:, :b]
    mod = mod.reshape(DEPTH, b, 6, 1, d)

    w_in_b = [w_in[:1].astype(BF16), None]
    h = None
    for l in range(DEPTH):
        sh1, sc1, g1, sh2, sc2, g2 = (mod[l, :, i] for i in range(6))
        if h is None:
            proj = _modulate_matmul(x, sc1, sh1, w_in_b[l], 0, BF16, *TILES_PROJ_IN, "proj_in")
        else:
            proj = _matmul(h.reshape(m, d), w_in_b[l], 0, BF16, *TILES_PROJ_IN, "proj_in")
        proj = proj.reshape(b, s, D_PROJ)
        lam_init = 0.8 - 0.6 * math.exp(-0.3 * l)
        if l == 0:
            assert DEPTH == 2
            cat, wg_b, wu_b = _moba(proj, tab_a, (w_gate.reshape(DEPTH * d, D_FF), w_up.reshape(DEPTH * d, D_FF)))
            cat, w_o_b = _swa(proj, tab_b, attn_sinks[l], cat, ((w_o.reshape(DEPTH * d, d), 0),))
            cat, w_in1_b, wd_b = _diff(proj, tab_c, diff_lambda[l], diff_subln_g[l][None, :], lam_init, cat,
                                       ((w_in.reshape(DEPTH * d, D_PROJ), d),), (w_down, D_FF))
            wg_b, wu_b = wg_b.reshape(DEPTH, d, D_FF), wu_b.reshape(DEPTH, d, D_FF)
            w_o_b, wd_b = w_o_b.reshape(DEPTH, d, d), wd_b.reshape(DEPTH, D_FF, d)
            w_in_b[1] = w_in1_b[None]
        else:
            (cat,) = _moba(proj, tab_a)
            (cat,) = _swa(proj, tab_b, attn_sinks[l], cat)
            (cat,) = _diff(proj, tab_c, diff_lambda[l], diff_subln_g[l][None, :], lam_init, cat)
        mix = _matmul(cat.reshape(m, d), w_o_b, l, BF16, *TILES_PROJ_OUT, "proj_out").reshape(b, s, d)
        x, h = _residual_norm(x, mix, g1, ln_g[l, 0][None, :], ln_b[l, 0][None, :], sc2, sh2)

        act = _gate_up(h.reshape(m, d), wg_b, wu_b, l, *TILES_GATE_UP)
        ff = _matmul_ksplit(act, wd_b, l, BF16, *TILES_DOWN, "ffn_down").reshape(b, s, d)
        if l + 1 < DEPTH:
            nsh, nsc = mod[l + 1, :, 0], mod[l + 1, :, 1]
            x, h = _residual_norm(x, ff, g2, ln_g[l, 1][None, :], ln_b[l, 1][None, :], nsc, nsh)
        else:
            (x,) = _residual_norm(x, ff, g2, ln_g[l, 1][None, :], ln_b[l, 1][None, :])
    return x
```

```python
import functools
import math

import numpy as np
import jax
import jax.numpy as jnp
from jax import lax
from jax.experimental import pallas as pl
from jax.experimental.pallas import tpu as pltpu

D_MODEL = 4096
DEPTH = 2
HEAD_DIM = 128
N_HEADS = D_MODEL // HEAD_DIM
H_A = N_HEADS // 4
H_C = N_HEADS // 4
H_B = N_HEADS - H_A - H_C
KV_B = max(1, H_B // 8)
G_B = H_B // KV_B
DIFF_DIM = HEAD_DIM // 2
MOBA_BLOCK = 256
MOBA_TOPK = 3
WINDOW = 128
N_BUCKETS = 32
MAX_DISTANCE = 128
D_FF = -(-8 * D_MODEL // (3 * 256)) * 256
ALPHA = (2.0 * DEPTH) ** 0.25
LN_EPS = 1e-5
NEG = -1e30
LOG2E = math.log2(math.e)

QA_W = H_A * HEAD_DIM
QB_W = H_B * HEAD_DIM
KB_W = KV_B * HEAD_DIM
QC_W = H_C * 2 * DIFF_DIM
VC_W = H_C * HEAD_DIM
PROJ_SIZES = (QA_W, QA_W, QA_W, QB_W, KB_W, KB_W, QC_W, QC_W, VC_W)
PROJ_OFFS = tuple(int(sum(PROJ_SIZES[:i])) for i in range(len(PROJ_SIZES)))
D_PROJ = sum(PROJ_SIZES)

LANE = 128
ATTN_TILE = 512
TILE_UNROLL = 4
CAST_ROW_TILE = 256
VMEM_LIMIT = 56 * 1024 * 1024
TILES_PROJ_IN = (1024, 512)
TILES_PROJ_IN_WIDE = (1024, 1024)
TILES_PROJ_OUT = (1024, 1024)
TILES_GATE_UP = (2048, 256)
TILES_DOWN = (1024, 1024, D_FF // 2)
NORM_ROWS = 256
ADA_COLS = 512
SWA_ROWS = 1024

F32 = jnp.float32
BF16 = jnp.bfloat16


def _bucket_upper_bounds():
    n = np.arange(0, 4 * MAX_DISTANCE, dtype=np.int64)
    max_exact = N_BUCKETS // 2
    nf = np.maximum(n, 1).astype(np.float32)
    large = max_exact + (np.log(nf / np.float32(max_exact)) / np.float32(math.log(MAX_DISTANCE / max_exact))
                         * np.float32(N_BUCKETS - max_exact)).astype(np.int32)
    large = np.minimum(large, N_BUCKETS - 1)
    bucket = np.where(n < max_exact, n, large)
    assert bucket[-1] == N_BUCKETS - 1 and np.all(np.diff(bucket) >= 0)
    return tuple(int(np.argmax(bucket > k)) for k in range(N_BUCKETS - 1))


BUCKET_UPPER = _bucket_upper_bounds()
FAR_DIST = BUCKET_UPPER[-1]


def _bias_from_dist(dist, tab):
    b = jnp.full(dist.shape, tab(N_BUCKETS - 1), F32)
    for k in range(N_BUCKETS - 2, -1, -1):
        b = jnp.where(dist < BUCKET_UPPER[k], tab(k), b)
    return b


def _dot(a, b):
    return jnp.dot(a, b, preferred_element_type=F32)


def _dot_nt(a, b):
    return lax.dot_general(a, b, (((1,), (1,)), ((), ())), preferred_element_type=F32)


def _params(*sem):
    return pltpu.CompilerParams(dimension_semantics=sem, vmem_limit_bytes=VMEM_LIMIT)


def _mod_kernel(c_ref, w_ref, b_ref, o_ref):
    c = c_ref[...]
    s = (c * jax.nn.sigmoid(c)).astype(BF16)
    o_ref[...] = _dot(s, w_ref[...].astype(BF16)) + b_ref[...]


def _ada_mod(c8, w_ada, b_ada3):
    depth, d, n = w_ada.shape
    tn = ADA_COLS
    return pl.pallas_call(
        _mod_kernel,
        grid=(depth, n // tn),
        in_specs=[pl.BlockSpec((8, d), lambda l, j: (0, 0)),
                  pl.BlockSpec((None, d, tn), lambda l, j: (l, 0, j)),
                  pl.BlockSpec((None, 1, tn), lambda l, j: (l, 0, j))],
        out_specs=pl.BlockSpec((None, 8, tn), lambda l, j: (l, 0, j)),
        out_shape=jax.ShapeDtypeStruct((depth, 8, n), F32),
        compiler_params=_params("parallel", "parallel"),
        name="ada_mod",
    )(c8, w_ada, b_ada3)


def _mm_kernel(x_ref, w_ref, *rest):
    o_ref = rest[-1]
    o_ref[...] = _dot(x_ref[...], w_ref[...]).astype(o_ref.dtype)


def _matmul(x, w, l, out_dtype, tm, tn, name, cols=None, into=None):
    m, k = x.shape
    _, _, n = w.shape
    c0, nc = cols or (0, n)
    assert c0 % tn == 0 and nc % tn == 0
    j0 = c0 // tn
    in_specs = [pl.BlockSpec((tm, k), lambda i, j: (i, 0)),
                pl.BlockSpec((None, k, tn), lambda i, j: (l, 0, j0 + j))]
    args = [x, w]
    if into is not None:
        in_specs.append(pl.BlockSpec(memory_space=pl.ANY))
        args.append(into)
    return pl.pallas_call(
        _mm_kernel,
        grid=(m // tm, nc // tn),
        in_specs=in_specs,
        out_specs=pl.BlockSpec((tm, tn), lambda i, j: (i, j0 + j)),
        out_shape=jax.ShapeDtypeStruct((m, n), out_dtype),
        input_output_aliases={} if into is None else {2: 0},
        compiler_params=_params("parallel", "arbitrary"),
        name=name,
    )(*args)


def _mod_mm_kernel(x_ref, sc_ref, sh_ref, w_ref, o_ref, h_ref):
    @pl.when(pl.program_id(1) == 0)
    def _():
        h_ref[...] = (x_ref[...] * (1.0 + sc_ref[...]) + sh_ref[...]).astype(BF16)

    o_ref[...] = _dot(h_ref[...], w_ref[...]).astype(o_ref.dtype)


def _modulate_matmul(x, sc, sh, w, l, out_dtype, tm, tn, name):
    b, s, k = x.shape
    _, _, n = w.shape
    assert s % tm == 0
    per_b = s // tm
    vec = pl.BlockSpec((None, 1, k), lambda i, j: (i // per_b, 0, 0))
    return pl.pallas_call(
        _mod_mm_kernel,
        grid=(b * per_b, n // tn),
        in_specs=[pl.BlockSpec((tm, k), lambda i, j: (i, 0)), vec, vec,
                  pl.BlockSpec((None, k, tn), lambda i, j: (l, 0, j))],
        out_specs=pl.BlockSpec((tm, tn), lambda i, j: (i, j)),
        out_shape=jax.ShapeDtypeStruct((b * s, n), out_dtype),
        scratch_shapes=[pltpu.VMEM((tm, k), BF16)],
        compiler_params=_params("parallel", "arbitrary"),
        name=name,
    )(x.reshape(b * s, k), sc, sh, w)


def _mm_acc_kernel(x_ref, w_ref, o_ref, acc_ref):
    kk = pl.program_id(2)
    part = _dot(x_ref[...], w_ref[...])

    @pl.when(kk == 0)
    def _():
        acc_ref[...] = part

    @pl.when((kk > 0) & (kk < pl.num_programs(2) - 1))
    def _():
        acc_ref[...] += part

    @pl.when(kk == pl.num_programs(2) - 1)
    def _():
        o_ref[...] = (acc_ref[...] + part).astype(o_ref.dtype)


def _matmul_ksplit(x, w, l, out_dtype, tm, tn, tk, name):
    m, k = x.shape
    _, _, n = w.shape
    assert k // tk >= 2
    return pl.pallas_call(
        _mm_acc_kernel,
        grid=(m // tm, n // tn, k // tk),
        in_specs=[pl.BlockSpec((tm, tk), lambda i, j, kk: (i, kk)),
                  pl.BlockSpec((None, tk, tn), lambda i, j, kk: (l, kk, j))],
        out_specs=pl.BlockSpec((tm, tn), lambda i, j, kk: (i, j)),
        out_shape=jax.ShapeDtypeStruct((m, n), out_dtype),
        scratch_shapes=[pltpu.VMEM((tm, tn), F32)],
        compiler_params=_params("parallel", "parallel", "arbitrary"),
        name=name,
    )(x, w)


def _cast_pad_cols_kernel(x_ref, o_ref):
    n = x_ref.shape[1]
    o_ref[:, :n] = x_ref[...].astype(BF16)
    if o_ref.shape[1] > n:
        o_ref[:, n:] = jnp.zeros((o_ref.shape[0], o_ref.shape[1] - n), BF16)


def _gate_up_kernel(h_ref, wg_ref, wu_ref, o_ref):
    h = h_ref[...]
    g = _dot(h, wg_ref[...])
    u = _dot(h, wu_ref[...])
    o_ref[...] = (g * jax.nn.sigmoid(g) * u).astype(BF16)


def _gate_up(h, wg, wu, l, tm, tf):
    m, k = h.shape
    _, _, f = wg.shape
    wspec = pl.BlockSpec((None, k, tf), lambda i, j: (l, 0, j))
    return pl.pallas_call(
        _gate_up_kernel,
        grid=(m // tm, f // tf),
        in_specs=[pl.BlockSpec((tm, k), lambda i, j: (i, 0)), wspec, wspec],
        out_specs=pl.BlockSpec((tm, tf), lambda i, j: (i, j)),
        out_shape=jax.ShapeDtypeStruct((m, f), BF16),
        compiler_params=_params("parallel", "arbitrary"),
        name="ffn_gate_up",
    )(h, wg, wu)


def _norm_kernel(with_h, x_ref, y_ref, gate_ref, lg_ref, lb_ref, *rest):
    z = ALPHA * x_ref[...] + gate_ref[...] * y_ref[...].astype(F32)
    mu = jnp.mean(z, axis=-1, keepdims=True)
    zc = z - mu
    var = jnp.mean(zc * zc, axis=-1, keepdims=True)
    xn = zc * lax.rsqrt(var + LN_EPS) * lg_ref[...] + lb_ref[...]
    if with_h:
        sc_ref, sh_ref, xo_ref, h_ref = rest
        xo_ref[...] = xn
        h_ref[...] = (xn * (1.0 + sc_ref[...]) + sh_ref[...]).astype(BF16)
    else:
        (xo_ref,) = rest
        xo_ref[...] = xn


def _residual_norm(x, y, gate, lg, lb, sc=None, sh=None):
    b, s, d = x.shape
    tm = NORM_ROWS
    with_h = sc is not None
    tile = pl.BlockSpec((None, tm, d), lambda i, j: (i, j, 0))
    vec = pl.BlockSpec((None, 1, d), lambda i, j: (i, 0, 0))
    par = pl.BlockSpec((1, d), lambda i, j: (0, 0))
    in_specs = [tile, tile, vec, par, par]
    args = [x, y, gate, lg, lb]
    out_specs = [tile]
    out_shape = [jax.ShapeDtypeStruct((b, s, d), F32)]
    if with_h:
        in_specs += [vec, vec]
        args += [sc, sh]
        out_specs.append(tile)
        out_shape.append(jax.ShapeDtypeStruct((b, s, d), BF16))
    return pl.pallas_call(
        functools.partial(_norm_kernel, with_h),
        grid=(b, s // tm),
        in_specs=in_specs,
        out_specs=out_specs,
        out_shape=out_shape,
        compiler_params=_params("parallel", "parallel"),
        name="residual_norm",
    )(*args)


def _lane_tiles(a):
    return [a[:, i * LANE:(i + 1) * LANE] for i in range(a.shape[1] // LANE)]


def _row_max_rep(s):
    tiles = _lane_tiles(s)
    t = tiles[0]
    for u in tiles[1:]:
        t = jnp.maximum(t, u)
    return jnp.broadcast_to(jnp.max(t, axis=1, keepdims=True), t.shape)


def _probs(s, m_rep):
    return jnp.concatenate([jnp.exp2(u - m_rep) for u in _lane_tiles(s)], axis=1).astype(BF16)


def _softmax_reset(m_ref, acc_ref):
    m_ref[...] = jnp.full(m_ref.shape, NEG, F32)
    acc_ref[...] = jnp.zeros(acc_ref.shape, F32)


def _softmax_update(s, v_aug, m_ref, acc_ref):
    m_old = m_ref[...]
    m = jnp.maximum(m_old, _row_max_rep(s))
    corr = jnp.exp2(m_old - m)
    m_ref[...] = m
    acc_ref[...] = jnp.concatenate([corr, corr], axis=1) * acc_ref[...] + _dot(_probs(s, m), v_aug)


def _augment_values(v_ref, vaug_ref):
    vaug_ref[:, :HEAD_DIM] = v_ref[...]
    vaug_ref[:, HEAD_DIM:] = jnp.ones((v_ref.shape[0], HEAD_DIM), BF16)


def _bias_tiles(tab, bias_ref):
    t = bias_ref.shape[1]
    i = lax.broadcasted_iota(jnp.int32, (t, t), 0)
    j = lax.broadcasted_iota(jnp.int32, (t, t), 1)
    far = tab(N_BUCKETS - 1)
    d0 = (_bias_from_dist(i - j, tab) - far) * LOG2E
    bias_ref[2] = jnp.where(i >= j, d0, NEG)
    bias_ref[1] = (_bias_from_dist(i - j + t, tab) - far) * LOG2E
    bias_ref[0] = jnp.zeros((t, t), F32)


def _causal_tiles(qi, logits_fn, update_fn, sa_ref, sb_ref):
    n = qi + 1
    bufs = (sa_ref, sb_ref)
    sa_ref[...] = logits_fn(qi)

    def group(left0, size):
        for i in range(size):
            cur, nxt = bufs[i % 2], bufs[(i + 1) % 2]
            kt = qi - (left0 + i)
            nxt[...] = logits_fn(jnp.maximum(kt - 1, 0))
            update_fn(cur[...], left0 + i, kt)

    def quad(p, carry):
        group(TILE_UNROLL * p, TILE_UNROLL)
        return carry

    quads = n // TILE_UNROLL
    lax.fori_loop(0, quads, quad, 0)
    rest = n - TILE_UNROLL * quads

    @pl.when(rest >= 2)
    def _():
        group(TILE_UNROLL * quads, 2)

    @pl.when(rest % 2 == 1)
    def _():
        update_fn(sa_ref[...], qi, 0)


def _moba_kernel(n_side, tab_ref, q_ref, k_ref, v_ref, *rest):
    side_in, o_ref, side_out = rest[:n_side], rest[n_side], rest[n_side + 1:2 * n_side + 1]
    kmean_ref, bias_ref, vaug_ref, sa_ref, sb_ref, m_ref, acc_ref = rest[2 * n_side + 1:]
    for x_ref, y_ref in zip(side_in, side_out):
        _cast_pad_cols_kernel(x_ref, y_ref)
    h = pl.program_id(0)
    bi = pl.program_id(1)
    pj = pl.program_id(2)
    t = ATTN_TILE
    nq = q_ref.shape[0] // t
    blk = MOBA_BLOCK
    bpt = t // blk
    nb = k_ref.shape[0] // blk
    cs = HEAD_DIM ** -0.5 * LOG2E

    @pl.when((bi == 0) & (pj == 0))
    def _():
        _bias_tiles(lambda k: tab_ref[h, k], bias_ref)

    @pl.when(pj == 0)
    def _():
        for jb in range(nb):
            kb = k_ref[jb * blk:(jb + 1) * blk, :].astype(F32)
            kmean_ref[jb:jb + 1, :] = jnp.sum(kb, axis=0, keepdims=True) * (1.0 / blk)
        _augment_values(v_ref, vaug_ref)

    def query_tile(j, carry):
        qi = jnp.where(j == 0, pj, nq - 1 - pj)
        rows = pl.ds(pl.multiple_of(qi * t, t), t)
        q = q_ref[rows, :]
        km = kmean_ref[...]
        km_hi = km.astype(BF16)
        km_lo = (km - km_hi.astype(F32)).astype(BF16)
        gate = _dot_nt(km_hi, q) + _dot_nt(km_lo, q)
        jblk = lax.broadcasted_iota(jnp.int32, gate.shape, 0)
        qpos = lax.broadcasted_iota(jnp.int32, (1, t), 1)
        qblk = qi * bpt + qpos // blk
        g = jnp.where(jblk < qblk, gate, -jnp.inf)
        bits = jnp.left_shift(1, qblk)
        for r in range(MOBA_TOPK):
            mx = jnp.max(g, axis=0, keepdims=True)
            first = jnp.min(jnp.where(g == mx, jblk, nb), axis=0, keepdims=True)
            bits = bits | jnp.where(r < qblk, jnp.left_shift(1, first), 0)
            g = jnp.where(jblk == first, -jnp.inf, g)
        bits = jnp.broadcast_to(bits.astype(F32), (LANE, t)).T.astype(jnp.int32)

        def logits(kt):
            return _dot_nt(q, k_ref[pl.ds(pl.multiple_of(kt * t, t), t), :]) * cs

        def update(s, left, kt):
            s = s + bias_ref[jnp.maximum(2 - left, 0)]
            tiles = _lane_tiles(s)
            per_blk = blk // LANE
            out = []
            for c in range(bpt):
                vis = (jnp.right_shift(bits, kt * bpt + c) & 1) == 1
                out += [jnp.where(vis, u, NEG) for u in tiles[c * per_blk:(c + 1) * per_blk]]
            s = jnp.concatenate(out, axis=1)
            _softmax_update(s, vaug_ref[pl.ds(pl.multiple_of(kt * t, t), t), :], m_ref, acc_ref)

        _softmax_reset(m_ref, acc_ref)
        _causal_tiles(qi, logits, update, sa_ref, sb_ref)
        acc = acc_ref[...]
        o_ref[rows, :] = (acc[:, :HEAD_DIM] / acc[:, HEAD_DIM:]).astype(o_ref.dtype)
        return carry

    lax.fori_loop(0, 2, query_tile, 0)


def _row_slab_specs(w, n_steps, step_fn, n_pad, row0=0):
    rows, n = w.shape[0] - row0, w.shape[1]
    assert rows % n_steps == 0 and (rows // n_steps) % 16 == 0 and row0 % (rows // n_steps) == 0
    tr = rows // n_steps
    first = row0 // tr
    return (pl.BlockSpec((tr, n), lambda *g: (first + step_fn(*g), 0)),
            pl.BlockSpec((tr, n_pad), lambda *g: (step_fn(*g), 0)),
            jax.ShapeDtypeStruct((rows, n_pad), BF16))


def _moba(proj, tab_a, side=()):
    b, s, _ = proj.shape
    t = ATTN_TILE
    assert s % (2 * t) == 0 and t % MOBA_BLOCK == 0 and t >= FAR_DIST and s // MOBA_BLOCK < 24
    nq = s // (2 * t)
    qo, ko, vo = (PROJ_OFFS[i] // HEAD_DIM for i in (0, 1, 2))
    side_specs = [_row_slab_specs(w, H_A * b * nq, lambda h, bi, qi: (h * b + bi) * nq + qi, w.shape[1]) for w in side]
    return pl.pallas_call(
        functools.partial(_moba_kernel, len(side)),
        grid=(H_A, b, nq),
        in_specs=[pl.BlockSpec(memory_space=pltpu.SMEM),
                  pl.BlockSpec((None, s, HEAD_DIM), lambda h, bi, qi: (bi, 0, qo + h)),
                  pl.BlockSpec((None, s, HEAD_DIM), lambda h, bi, qi: (bi, 0, ko + h)),
                  pl.BlockSpec((None, s, HEAD_DIM), lambda h, bi, qi: (bi, 0, vo + h))]
                 + [sp[0] for sp in side_specs],
        out_specs=[pl.BlockSpec((None, s, HEAD_DIM), lambda h, bi, qi: (bi, 0, h))] + [sp[1] for sp in side_specs],
        out_shape=[jax.ShapeDtypeStruct((b, s, D_MODEL), BF16)]
                  + [sp[2] for sp in side_specs],
        scratch_shapes=[pltpu.VMEM((s // MOBA_BLOCK, HEAD_DIM), F32),
                        pltpu.VMEM((3, t, t), F32),
                        pltpu.VMEM((s, 2 * HEAD_DIM), BF16),
                        pltpu.VMEM((t, t), F32), pltpu.VMEM((t, t), F32),
                        pltpu.VMEM((t, LANE), F32),
                        pltpu.VMEM((t, 2 * HEAD_DIM), F32)],
        compiler_params=_params("arbitrary", "arbitrary", "arbitrary"),
        name="moba_attention",
    )(tab_a, proj, proj, proj, *side)


def _swa_kernel(n_side, tab_ref, sink_ref, q_ref, k_ref, v_ref, cat_ref, *rest):
    del cat_ref
    side_in, o_ref, side_out = rest[:n_side], rest[n_side], rest[n_side + 1:2 * n_side + 1]
    bias_ref, sink_rep_ref, vaug_ref, sa_ref, sb_ref = rest[2 * n_side + 1:]
    for x_ref, y_ref in zip(side_in, side_out):
        _cast_pad_cols_kernel(x_ref, y_ref)
    g = pl.program_id(1)
    ti = pl.program_id(2)
    w = WINDOW
    nsub = q_ref.shape[0] // w
    cs = HEAD_DIM ** -0.5 * LOG2E

    @pl.when(ti == 0)
    def _():
        i = lax.broadcasted_iota(jnp.int32, (w, 2 * w), 0)
        j = lax.broadcasted_iota(jnp.int32, (w, 2 * w), 1)
        dist = i + w - j
        band = (dist >= 0) & (dist < w)
        for hh in range(G_B):
            head = g * G_B + hh
            bias = jnp.where(band, _bias_from_dist(dist, lambda k: tab_ref[head, k]) * LOG2E, NEG)
            bias_ref[0, hh * w:(hh + 1) * w, :] = bias
            bias_ref[1, hh * w:(hh + 1) * w, :] = jnp.where(j >= w, bias, NEG)
            sink_rep_ref[hh * w:(hh + 1) * w, :] = jnp.full((w, LANE), sink_ref[head] * LOG2E, F32)
        _augment_values(v_ref, vaug_ref)

    def band_rows(ref, blk):
        k0 = pl.multiple_of(jnp.maximum(blk - 1, 0) * w, w)
        k1 = pl.multiple_of(blk * w, w)
        return jnp.concatenate([ref[pl.ds(k0, w), :], ref[pl.ds(k1, w), :]], axis=0)

    def logits(n):
        qs = jnp.concatenate([q_ref[n * w:(n + 1) * w, hh * HEAD_DIM:(hh + 1) * HEAD_DIM] for hh in range(G_B)], axis=0)
        return _dot_nt(qs, band_rows(k_ref, ti * nsub + n)) * cs

    def finish(s, n):
        blk = ti * nsub + n
        s = s + bias_ref[jnp.where(blk == 0, 1, 0)]
        sink = sink_rep_ref[...]
        m = jnp.maximum(_row_max_rep(s), sink)
        acc = _dot(_probs(s, m), band_rows(vaug_ref, blk))
        o = acc[:, :HEAD_DIM] / (acc[:, HEAD_DIM:] + jnp.exp2(sink - m))
        for hh in range(G_B):
            o_ref[n * w:(n + 1) * w, hh * HEAD_DIM:(hh + 1) * HEAD_DIM] = o[hh * w:(hh + 1) * w, :].astype(o_ref.dtype)

    bufs = (sa_ref, sb_ref)
    sa_ref[...] = logits(0)
    for n in range(nsub):
        if n + 1 < nsub:
            bufs[(n + 1) % 2][...] = logits(n + 1)
        finish(bufs[n % 2][...], n)


def _swa(proj, tab_b, sinks, cat, side=()):
    b, s, _ = proj.shape
    tq = SWA_ROWS
    nt = s // tq
    gw = G_B * HEAD_DIM
    qo = PROJ_OFFS[3] // gw
    oo = QA_W // gw
    ko, vo = (PROJ_OFFS[i] // HEAD_DIM for i in (4, 5))
    assert PROJ_OFFS[3] % gw == 0 and QA_W % gw == 0 and s % tq == 0
    smem = pl.BlockSpec(memory_space=pltpu.SMEM)
    side_specs = [_row_slab_specs(w, b * KV_B * nt, lambda bi, g, ti: (bi * KV_B + g) * nt + ti, w.shape[1], row0)
                  for w, row0 in side]
    side = [w for w, _ in side]
    return pl.pallas_call(
        functools.partial(_swa_kernel, len(side)),
        grid=(b, KV_B, nt),
        in_specs=[smem, smem,
                  pl.BlockSpec((None, tq, gw), lambda bi, g, ti: (bi, ti, qo + g)),
                  pl.BlockSpec((None, s, HEAD_DIM), lambda bi, g, ti: (bi, 0, ko + g)),
                  pl.BlockSpec((None, s, HEAD_DIM), lambda bi, g, ti: (bi, 0, vo + g)),
                  pl.BlockSpec(memory_space=pl.ANY)] + [sp[0] for sp in side_specs],
        out_specs=[pl.BlockSpec((None, tq, gw), lambda bi, g, ti: (bi, ti, oo + g))] + [sp[1] for sp in side_specs],
        out_shape=[jax.ShapeDtypeStruct(cat.shape, cat.dtype)] + [sp[2] for sp in side_specs],
        input_output_aliases={5: 0},
        scratch_shapes=[pltpu.VMEM((2, G_B * WINDOW, 2 * WINDOW), F32),
                        pltpu.VMEM((G_B * WINDOW, LANE), F32),
                        pltpu.VMEM((s, 2 * HEAD_DIM), BF16),
                        pltpu.VMEM((G_B * WINDOW, 2 * WINDOW), F32),
                        pltpu.VMEM((G_B * WINDOW, 2 * WINDOW), F32)],
        compiler_params=_params("parallel", "parallel", "arbitrary"),
        name="swa_attention",
    )(tab_b, sinks, proj, proj, proj, cat, *side)


def _diff_kernel(lam_init, n_cols, row_tiles, tab_ref, q_ref, k_ref, v_ref, lamp_ref, subg_ref, cat_ref, *rest):
    del cat_ref
    n_side = n_cols + (row_tiles is not None)
    side_in, o_ref, side_out = rest[:n_side], rest[n_side], rest[n_side + 1:2 * n_side + 1]
    bias_ref, vaug_ref, sa_ref, sb_ref, m_ref, acc_ref = rest[2 * n_side + 1:]
    for x_ref, y_ref in zip(side_in[:n_cols], side_out[:n_cols]):
        _cast_pad_cols_kernel(x_ref, y_ref)
    side_in, side_out = side_in[n_cols:], side_out[n_cols:]
    h = pl.program_id(0)
    bi = pl.program_id(1)
    pj = pl.program_id(2)
    t = ATTN_TILE
    nq = q_ref.shape[0] // t
    cs = DIFF_DIM ** -0.5 * LOG2E

    if row_tiles is not None:
        n_src, n_dst, depth = row_tiles
        step = (h * pl.num_programs(1) + bi) * pl.num_programs(2) + pj

        @pl.when((step < depth * n_dst) & (step % n_dst < n_src))
        def _():
            side_out[0][...] = side_in[0][...].astype(BF16)

        @pl.when((step < depth * n_dst) & (step % n_dst >= n_src))
        def _():
            side_out[0][...] = jnp.zeros(side_out[0].shape, BF16)

    @pl.when((bi == 0) & (pj == 0))
    def _():
        _bias_tiles(lambda k: tab_ref[h, k], bias_ref)

    @pl.when(pj == 0)
    def _():
        _augment_values(v_ref, vaug_ref)

    lp = lamp_ref[...]
    lam = (jnp.exp(jnp.sum(lp[0:1] * lp[1:2], axis=1, keepdims=True))
           - jnp.exp(jnp.sum(lp[2:3] * lp[3:4], axis=1, keepdims=True)) + lam_init)

    def query_tile(j, carry):
        qi = jnp.where(j == 0, pj, nq - 1 - pj)
        rows = pl.ds(pl.multiple_of(qi * t, t), t)
        q = q_ref[rows, :]
        lane = lax.broadcasted_iota(jnp.int32, q.shape, 1)
        zero = jnp.zeros_like(q)
        q2 = jnp.concatenate([jnp.where(lane < DIFF_DIM, q, zero), jnp.where(lane >= DIFF_DIM, q, zero)], axis=0)

        def logits(kt):
            return _dot_nt(q2, k_ref[pl.ds(pl.multiple_of(kt * t, t), t), :]) * cs

        def update(s, left, kt):
            d = bias_ref[jnp.maximum(2 - left, 0)]
            s = jnp.concatenate([s[:t] + d, s[t:] + d], axis=0)
            _softmax_update(s, vaug_ref[pl.ds(pl.multiple_of(kt * t, t), t), :], m_ref, acc_ref)

        _softmax_reset(m_ref, acc_ref)
        _causal_tiles(qi, logits, update, sa_ref, sb_ref)

        acc = acc_ref[...]
        a = acc[:, :HEAD_DIM] / acc[:, HEAD_DIM:]
        o = a[:t] - lam * a[t:]
        o = o * lax.rsqrt(jnp.mean(o * o, axis=-1, keepdims=True) + LN_EPS) * subg_ref[...]
        o_ref[rows, :] = (o * (1.0 - lam_init)).astype(o_ref.dtype)
        return carry

    lax.fori_loop(0, 2, query_tile, 0)


def _diff(proj, tab_c, lam_p, subln_g, lam_init, cat, cast_cols=(), pad_rows=None):
    b, s, _ = proj.shape
    t = ATTN_TILE
    assert s % (2 * t) == 0 and t >= FAR_DIST
    nq = s // (2 * t)
    qo, ko, vo = (PROJ_OFFS[i] // HEAD_DIM for i in (6, 7, 8))
    oo = (QA_W + QB_W) // HEAD_DIM
    col_specs = [_row_slab_specs(w, H_C * b * nq, lambda h, bi, qi: (h * b + bi) * nq + qi, w.shape[1], row0)
                 for w, row0 in cast_cols]
    side = [w for w, _ in cast_cols]
    side_in, side_out, side_shape = ([sp[i] for sp in col_specs] for i in range(3))
    row_tiles = None
    if pad_rows is not None:
        w, k_pad = pad_rows
        depth, k, n = w.shape
        tr = CAST_ROW_TILE
        assert k % tr == 0 and k_pad % tr == 0
        n_src, n_dst = k // tr, k_pad // tr
        assert depth * n_dst <= H_C * b * nq
        row_tiles = (n_src, n_dst, depth)

        def dst_tile(h, bi, qi):
            return jnp.minimum((h * b + bi) * nq + qi, depth * n_dst - 1)

        def src_tile(h, bi, qi):
            d = dst_tile(h, bi, qi)
            return (d // n_dst) * n_src + jnp.minimum(d % n_dst, n_src - 1)

        side.append(w.reshape(depth * k, n))
        side_in.append(pl.BlockSpec((tr, n), lambda h, bi, qi: (src_tile(h, bi, qi), 0)))
        side_out.append(pl.BlockSpec((tr, n), lambda h, bi, qi: (dst_tile(h, bi, qi), 0)))
        side_shape.append(jax.ShapeDtypeStruct((depth * k_pad, n), BF16))
    return pl.pallas_call(
        functools.partial(_diff_kernel, lam_init, len(cast_cols), row_tiles),
        grid=(H_C, b, nq),
        in_specs=[pl.BlockSpec(memory_space=pltpu.SMEM),
                  pl.BlockSpec((None, s, HEAD_DIM), lambda h, bi, qi: (bi, 0, qo + h)),
                  pl.BlockSpec((None, s, HEAD_DIM), lambda h, bi, qi: (bi, 0, ko + h)),
                  pl.BlockSpec((None, s, HEAD_DIM), lambda h, bi, qi: (bi, 0, vo + h)),
                  pl.BlockSpec((4, DIFF_DIM), lambda h, bi, qi: (0, 0)),
                  pl.BlockSpec((1, HEAD_DIM), lambda h, bi, qi: (0, 0)),
                  pl.BlockSpec(memory_space=pl.ANY)] + side_in,
        out_specs=[pl.BlockSpec((None, s, HEAD_DIM), lambda h, bi, qi: (bi, 0, oo + h))] + side_out,
        out_shape=[jax.ShapeDtypeStruct(cat.shape, cat.dtype)] + side_shape,
        input_output_aliases={6: 0},
        scratch_shapes=[pltpu.VMEM((3, t, t), F32),
                        pltpu.VMEM((s, 2 * HEAD_DIM), BF16),
                        pltpu.VMEM((2 * t, t), F32), pltpu.VMEM((2 * t, t), F32),
                        pltpu.VMEM((2 * t, LANE), F32),
                        pltpu.VMEM((2 * t, 2 * HEAD_DIM), F32)],
        compiler_params=_params("arbitrary", "arbitrary", "arbitrary"),
        name="diff_attention",
    )(tab_c, proj, proj, proj, lam_p, subln_g, cat, *side)


def kernel(x, c, rel_bias, w_ada, b_ada, w_in, w_o, attn_sinks, diff_lambda, diff_subln_g,
           ln_g, ln_b, w_gate, w_up, w_down):
    b, s, d = x.shape
    m = b * s
    tab = rel_bias.T
    tab_a, tab_b, tab_c = tab[:H_A], tab[H_A:H_A + H_B], tab[H_A + H_B:]

    c8 = jnp.pad(c, ((0, 8 - b), (0, 0)))
    mod = _ada_mod(c8, w_ada, b_ada[:, None, :])[:, :b]
    mod = mod.reshape(DEPTH, b, 6, 1, d)

    w_in_b = [w_in[:1].astype(BF16), None]
    h = None
    for l in range(DEPTH):
        sh1, sc1, g1, sh2, sc2, g2 = (mod[l, :, i] for i in range(6))
        if h is None:
            proj = _modulate_matmul(x, sc1, sh1, w_in_b[l], 0, BF16, *TILES_PROJ_IN, "proj_in")
        else:
            wide = TILES_PROJ_IN_WIDE[1]
            n_wide = D_PROJ // wide * wide
            proj = _matmul(h.reshape(m, d), w_in_b[l], 0, BF16, *TILES_PROJ_IN_WIDE, "proj_in", (0, n_wide))
            proj = _matmul(h.reshape(m, d), w_in_b[l], 0, BF16, *TILES_PROJ_IN, "proj_in_tail",
                           (n_wide, D_PROJ - n_wide), proj)
        proj = proj.reshape(b, s, D_PROJ)
        lam_init = 0.8 - 0.6 * math.exp(-0.3 * l)
        if l == 0:
            assert DEPTH == 2
            cat, wg_b, wu_b = _moba(proj, tab_a, (w_gate.reshape(DEPTH * d, D_FF), w_up.reshape(DEPTH * d, D_FF)))
            cat, w_o_b = _swa(proj, tab_b, attn_sinks[l], cat, ((w_o.reshape(DEPTH * d, d), 0),))
            cat, w_in1_b, wd_b = _diff(proj, tab_c, diff_lambda[l], diff_subln_g[l][None, :], lam_init, cat,
                                       ((w_in.reshape(DEPTH * d, D_PROJ), d),), (w_down, D_FF))
            wg_b, wu_b = wg_b.reshape(DEPTH, d, D_FF), wu_b.reshape(DEPTH, d, D_FF)
            w_o_b, wd_b = w_o_b.reshape(DEPTH, d, d), wd_b.reshape(DEPTH, D_FF, d)
            w_in_b[1] = w_in1_b[None]
        else:
            (cat,) = _moba(proj, tab_a)
            (cat,) = _swa(proj, tab_b, attn_sinks[l], cat)
            (cat,) = _diff(proj, tab_c, diff_lambda[l], diff_subln_g[l][None, :], lam_init, cat)
        mix = _matmul(cat.reshape(m, d), w_o_b, l, BF16, *TILES_PROJ_OUT, "proj_out").reshape(b, s, d)
        x, h = _residual_norm(x, mix, g1, ln_g[l, 0][None, :], ln_b[l, 0][None, :], sc2, sh2)

        act = _gate_up(h.reshape(m, d), wg_b, wu_b, l, *TILES_GATE_UP)
        ff = _matmul_ksplit(act, wd_b, l, BF16, *TILES_DOWN, "ffn_down").reshape(b, s, d)
        if l + 1 < DEPTH:
            nsh, nsc = mod[l + 1, :, 0], mod[l + 1, :, 1]
            x, h = _residual_norm(x, ff, g2, ln_g[l, 1][None, :], ln_b[l, 1][None, :], nsc, nsh)
        else:
            (x,) = _residual_norm(x, ff, g2, ln_g[l, 1][None, :], ln_b[l, 1][None, :])
    return x
```

```python
import functools
import math

import numpy as np
import jax
import jax.numpy as jnp
from jax import lax
from jax.experimental import pallas as pl
from jax.experimental.pallas import tpu as pltpu

D_MODEL = 4096
DEPTH = 2
HEAD_DIM = 128
N_HEADS = D_MODEL // HEAD_DIM
H_A = N_HEADS // 4
H_C = N_HEADS // 4
H_B = N_HEADS - H_A - H_C
KV_B = max(1, H_B // 8)
G_B = H_B // KV_B
DIFF_DIM = HEAD_DIM // 2
MOBA_BLOCK = 256
MOBA_TOPK = 3
WINDOW = 128
N_BUCKETS = 32
MAX_DISTANCE = 128
D_FF = -(-8 * D_MODEL // (3 * 256)) * 256
ALPHA = (2.0 * DEPTH) ** 0.25
LN_EPS = 1e-5
NEG = -1e30
LOG2E = math.log2(math.e)

QA_W = H_A * HEAD_DIM
QB_W = H_B * HEAD_DIM
KB_W = KV_B * HEAD_DIM
QC_W = H_C * 2 * DIFF_DIM
VC_W = H_C * HEAD_DIM
PROJ_SIZES = (QA_W, QA_W, QA_W, QB_W, KB_W, KB_W, QC_W, QC_W, VC_W)
PROJ_OFFS = tuple(int(sum(PROJ_SIZES[:i])) for i in range(len(PROJ_SIZES)))
D_PROJ = sum(PROJ_SIZES)

LANE = 128
ATTN_TILE = 512
TILE_UNROLL = 4
CAST_ROW_TILE = 256
VMEM_LIMIT = 56 * 1024 * 1024
TILES_PROJ_IN = (1024, 512)
TILES_PROJ_IN_WIDE = (1024, 1024)
TILES_PROJ_OUT = (1024, 1024)
TILES_GATE_UP = (2048, 256)
TILES_DOWN = (1024, 1024, D_FF // 2)
NORM_ROWS = 256
ADA_COLS = 512
SWA_ROWS = 1024

F32 = jnp.float32
BF16 = jnp.bfloat16


def _bucket_upper_bounds():
    n = np.arange(0, 4 * MAX_DISTANCE, dtype=np.int64)
    max_exact = N_BUCKETS // 2
    nf = np.maximum(n, 1).astype(np.float32)
    large = max_exact + (np.log(nf / np.float32(max_exact)) / np.float32(math.log(MAX_DISTANCE / max_exact))
                         * np.float32(N_BUCKETS - max_exact)).astype(np.int32)
    large = np.minimum(large, N_BUCKETS - 1)
    bucket = np.where(n < max_exact, n, large)
    assert bucket[-1] == N_BUCKETS - 1 and np.all(np.diff(bucket) >= 0)
    return tuple(int(np.argmax(bucket > k)) for k in range(N_BUCKETS - 1))


BUCKET_UPPER = _bucket_upper_bounds()
FAR_DIST = BUCKET_UPPER[-1]


def _bias_from_dist(dist, tab):
    b = jnp.full(dist.shape, tab(N_BUCKETS - 1), F32)
    for k in range(N_BUCKETS - 2, -1, -1):
        b = jnp.where(dist < BUCKET_UPPER[k], tab(k), b)
    return b


def _dot(a, b):
    return jnp.dot(a, b, preferred_element_type=F32)


def _dot_nt(a, b):
    return lax.dot_general(a, b, (((1,), (1,)), ((), ())), preferred_element_type=F32)


def _params(*sem):
    return pltpu.CompilerParams(dimension_semantics=sem, vmem_limit_bytes=VMEM_LIMIT)


def _mod_kernel(c_ref, w_ref, b_ref, o_ref):
    c = c_ref[...]
    s = (c * jax.nn.sigmoid(c)).astype(BF16)
    o_ref[...] = _dot(s, w_ref[...].astype(BF16)) + b_ref[...]


def _ada_mod(c8, w_ada, b_ada3):
    depth, d, n = w_ada.shape
    tn = ADA_COLS
    return pl.pallas_call(
        _mod_kernel,
        grid=(depth, n // tn),
        in_specs=[pl.BlockSpec((8, d), lambda l, j: (0, 0)),
                  pl.BlockSpec((None, d, tn), lambda l, j: (l, 0, j)),
                  pl.BlockSpec((None, 1, tn), lambda l, j: (l, 0, j))],
        out_specs=pl.BlockSpec((None, 8, tn), lambda l, j: (l, 0, j)),
        out_shape=jax.ShapeDtypeStruct((depth, 8, n), F32),
        compiler_params=_params("parallel", "parallel"),
        name="ada_mod",
    )(c8, w_ada, b_ada3)


def _mm_kernel(x_ref, w_ref, *rest):
    o_ref = rest[-1]
    o_ref[...] = _dot(x_ref[...], w_ref[...]).astype(o_ref.dtype)


def _matmul(x, w, l, out_dtype, tm, tn, name, cols=None, into=None):
    m, k = x.shape
    _, _, n = w.shape
    c0, nc = cols or (0, n)
    assert c0 % tn == 0 and nc % tn == 0
    j0 = c0 // tn
    in_specs = [pl.BlockSpec((tm, k), lambda i, j: (i, 0)),
                pl.BlockSpec((None, k, tn), lambda i, j: (l, 0, j0 + j))]
    args = [x, w]
    if into is not None:
        in_specs.append(pl.BlockSpec(memory_space=pl.ANY))
        args.append(into)
    return pl.pallas_call(
        _mm_kernel,
        grid=(m // tm, nc // tn),
        in_specs=in_specs,
        out_specs=pl.BlockSpec((tm, tn), lambda i, j: (i, j0 + j)),
        out_shape=jax.ShapeDtypeStruct((m, n), out_dtype),
        input_output_aliases={} if into is None else {2: 0},
        compiler_params=_params("parallel", "arbitrary"),
        name=name,
    )(*args)


def _mod_mm_kernel(x_ref, sc_ref, sh_ref, w_ref, o_ref, h_ref):
    @pl.when(pl.program_id(1) == 0)
    def _():
        h_ref[...] = (x_ref[...] * (1.0 + sc_ref[...]) + sh_ref[...]).astype(BF16)

    o_ref[...] = _dot(h_ref[...], w_ref[...]).astype(o_ref.dtype)


def _modulate_matmul(x, sc, sh, w, l, out_dtype, tm, tn, name):
    b, s, k = x.shape
    _, _, n = w.shape
    assert s % tm == 0
    per_b = s // tm
    vec = pl.BlockSpec((None, 1, k), lambda i, j: (i // per_b, 0, 0))
    return pl.pallas_call(
        _mod_mm_kernel,
        grid=(b * per_b, n // tn),
        in_specs=[pl.BlockSpec((tm, k), lambda i, j: (i, 0)), vec, vec,
                  pl.BlockSpec((None, k, tn), lambda i, j: (l, 0, j))],
        out_specs=pl.BlockSpec((tm, tn), lambda i, j: (i, j)),
        out_shape=jax.ShapeDtypeStruct((b * s, n), out_dtype),
        scratch_shapes=[pltpu.VMEM((tm, k), BF16)],
        compiler_params=_params("parallel", "arbitrary"),
        name=name,
    )(x.reshape(b * s, k), sc, sh, w)


def _mm_acc_kernel(x_ref, w_ref, o_ref, acc_ref):
    kk = pl.program_id(2)
    part = _dot(x_ref[...], w_ref[...])

    @pl.when(kk == 0)
    def _():
        acc_ref[...] = part

    @pl.when((kk > 0) & (kk < pl.num_programs(2) - 1))
    def _():
        acc_ref[...] += part

    @pl.when(kk == pl.num_programs(2) - 1)
    def _():
        o_ref[...] = (acc_ref[...] + part).astype(o_ref.dtype)


def _matmul_ksplit(x, w, l, out_dtype, tm, tn, tk, name):
    m, k = x.shape
    _, _, n = w.shape
    assert k // tk >= 2
    return pl.pallas_call(
        _mm_acc_kernel,
        grid=(m // tm, n // tn, k // tk),
        in_specs=[pl.BlockSpec((tm, tk), lambda i, j, kk: (i, kk)),
                  pl.BlockSpec((None, tk, tn), lambda i, j, kk: (l, kk, j))],
        out_specs=pl.BlockSpec((tm, tn), lambda i, j, kk: (i, j)),
        out_shape=jax.ShapeDtypeStruct((m, n), out_dtype),
        scratch_shapes=[pltpu.VMEM((tm, tn), F32)],
        compiler_params=_params("parallel", "parallel", "arbitrary"),
        name=name,
    )(x, w)


def _cast_pad_cols_kernel(x_ref, o_ref):
    n = x_ref.shape[1]
    o_ref[:, :n] = x_ref[...].astype(BF16)
    if o_ref.shape[1] > n:
        o_ref[:, n:] = jnp.zeros((o_ref.shape[0], o_ref.shape[1] - n), BF16)


def _gate_up_kernel(h_ref, wg_ref, wu_ref, o_ref):
    h = h_ref[...]
    g = _dot(h, wg_ref[...])
    u = _dot(h, wu_ref[...])
    o_ref[...] = (g * jax.nn.sigmoid(g) * u).astype(BF16)


def _gate_up(h, wg, wu, l, tm, tf):
    m, k = h.shape
    _, _, f = wg.shape
    wspec = pl.BlockSpec((None, k, tf), lambda i, j: (l, 0, j))
    return pl.pallas_call(
        _gate_up_kernel,
        grid=(m // tm, f // tf),
        in_specs=[pl.BlockSpec((tm, k), lambda i, j: (i, 0)), wspec, wspec],
        out_specs=pl.BlockSpec((tm, tf), lambda i, j: (i, j)),
        out_shape=jax.ShapeDtypeStruct((m, f), BF16),
        compiler_params=_params("parallel", "arbitrary"),
        name="ffn_gate_up",
    )(h, wg, wu)


def _norm_kernel(with_h, x_ref, y_ref, gate_ref, lg_ref, lb_ref, *rest):
    z = ALPHA * x_ref[...] + gate_ref[...] * y_ref[...].astype(F32)
    mu = jnp.mean(z, axis=-1, keepdims=True)
    zc = z - mu
    var = jnp.mean(zc * zc, axis=-1, keepdims=True)
    xn = zc * lax.rsqrt(var + LN_EPS) * lg_ref[...] + lb_ref[...]
    if with_h:
        sc_ref, sh_ref, xo_ref, h_ref = rest
        xo_ref[...] = xn
        h_ref[...] = (xn * (1.0 + sc_ref[...]) + sh_ref[...]).astype(BF16)
    else:
        (xo_ref,) = rest
        xo_ref[...] = xn


def _residual_norm(x, y, gate, lg, lb, sc=None, sh=None):
    b, s, d = x.shape
    tm = NORM_ROWS
    with_h = sc is not None
    tile = pl.BlockSpec((None, tm, d), lambda i, j: (i, j, 0))
    vec = pl.BlockSpec((None, 1, d), lambda i, j: (i, 0, 0))
    par = pl.BlockSpec((1, d), lambda i, j: (0, 0))
    in_specs = [tile, tile, vec, par, par]
    args = [x, y, gate, lg, lb]
    out_specs = [tile]
    out_shape = [jax.ShapeDtypeStruct((b, s, d), F32)]
    if with_h:
        in_specs += [vec, vec]
        args += [sc, sh]
        out_specs.append(tile)
        out_shape.append(jax.ShapeDtypeStruct((b, s, d), BF16))
    return pl.pallas_call(
        functools.partial(_norm_kernel, with_h),
        grid=(b, s // tm),
        in_specs=in_specs,
        out_specs=out_specs,
        out_shape=out_shape,
        compiler_params=_params("parallel", "parallel"),
        name="residual_norm",
    )(*args)


def _lane_tiles(a):
    return [a[:, i * LANE:(i + 1) * LANE] for i in range(a.shape[1] // LANE)]


def _row_max_rep(s):
    tiles = _lane_tiles(s)
    t = tiles[0]
    for u in tiles[1:]:
        t = jnp.maximum(t, u)
    return jnp.broadcast_to(jnp.max(t, axis=1, keepdims=True), t.shape)


def _probs(s, m_rep):
    return jnp.concatenate([jnp.exp2(u - m_rep) for u in _lane_tiles(s)], axis=1).astype(BF16)


def _softmax_reset(m_ref, acc_ref):
    m_ref[...] = jnp.full(m_ref.shape, NEG, F32)
    acc_ref[...] = jnp.zeros(acc_ref.shape, F32)


def _softmax_update(s, v_aug, m_ref, acc_ref):
    m_old = m_ref[...]
    m = jnp.maximum(m_old, _row_max_rep(s))
    corr = jnp.exp2(m_old - m)
    m_ref[...] = m
    acc_ref[...] = jnp.concatenate([corr, corr], axis=1) * acc_ref[...] + _dot(_probs(s, m), v_aug)


def _augment_values(v_ref, vaug_ref):
    vaug_ref[:, :HEAD_DIM] = v_ref[...]
    vaug_ref[:, HEAD_DIM:] = jnp.ones((v_ref.shape[0], HEAD_DIM), BF16)


def _bias_tiles(tab, bias_ref):
    t = bias_ref.shape[1]
    i = lax.broadcasted_iota(jnp.int32, (t, t), 0)
    j = lax.broadcasted_iota(jnp.int32, (t, t), 1)
    far = tab(N_BUCKETS - 1)
    d0 = (_bias_from_dist(i - j, tab) - far) * LOG2E
    bias_ref[2] = jnp.where(i >= j, d0, NEG)
    bias_ref[1] = (_bias_from_dist(i - j + t, tab) - far) * LOG2E
    bias_ref[0] = jnp.zeros((t, t), F32)


def _causal_tiles(qi, logits_fn, update_fn, sa_ref, sb_ref):
    n = qi + 1
    bufs = (sa_ref, sb_ref)
    sa_ref[...] = logits_fn(qi)

    def group(left0, size):
        for i in range(size):
            cur, nxt = bufs[i % 2], bufs[(i + 1) % 2]
            kt = qi - (left0 + i)
            nxt[...] = logits_fn(jnp.maximum(kt - 1, 0))
            update_fn(cur[...], left0 + i, kt)

    def quad(p, carry):
        group(TILE_UNROLL * p, TILE_UNROLL)
        return carry

    quads = n // TILE_UNROLL
    lax.fori_loop(0, quads, quad, 0)
    rest = n - TILE_UNROLL * quads

    @pl.when(rest >= 2)
    def _():
        group(TILE_UNROLL * quads, 2)

    @pl.when(rest % 2 == 1)
    def _():
        update_fn(sa_ref[...], qi, 0)


def _moba_kernel(n_side, tab_ref, q_ref, k_ref, v_ref, *rest):
    side_in, o_ref, side_out = rest[:n_side], rest[n_side], rest[n_side + 1:2 * n_side + 1]
    kmean_ref, bias_ref, vaug_ref, sa_ref, sb_ref, m_ref, acc_ref = rest[2 * n_side + 1:]
    for x_ref, y_ref in zip(side_in, side_out):
        _cast_pad_cols_kernel(x_ref, y_ref)
    h = pl.program_id(0)
    bi = pl.program_id(1)
    pj = pl.program_id(2)
    t = ATTN_TILE
    nq = q_ref.shape[0] // t
    blk = MOBA_BLOCK
    bpt = t // blk
    nb = k_ref.shape[0] // blk
    cs = HEAD_DIM ** -0.5 * LOG2E

    @pl.when((bi == 0) & (pj == 0))
    def _():
        _bias_tiles(lambda k: tab_ref[h, k], bias_ref)

    @pl.when(pj == 0)
    def _():
        for jb in range(nb):
            kb = k_ref[jb * blk:(jb + 1) * blk, :].astype(F32)
            kmean_ref[jb:jb + 1, :] = jnp.sum(kb, axis=0, keepdims=True) * (1.0 / blk)
        _augment_values(v_ref, vaug_ref)

    def query_tile(j, carry):
        qi = pj if j == 0 else nq - 1 - pj
        rows = pl.ds(pl.multiple_of(qi * t, t), t)
        q = q_ref[rows, :]
        km = kmean_ref[...]
        km_hi = km.astype(BF16)
        km_lo = (km - km_hi.astype(F32)).astype(BF16)
        gate = _dot_nt(km_hi, q) + _dot_nt(km_lo, q)
        jblk = lax.broadcasted_iota(jnp.int32, gate.shape, 0)
        qpos = lax.broadcasted_iota(jnp.int32, (1, t), 1)
        qblk = qi * bpt + qpos // blk
        g = jnp.where(jblk < qblk, gate, -jnp.inf)
        bits = jnp.left_shift(1, qblk)
        for r in range(MOBA_TOPK):
            mx = jnp.max(g, axis=0, keepdims=True)
            first = jnp.min(jnp.where(g == mx, jblk, nb), axis=0, keepdims=True)
            bits = bits | jnp.where(r < qblk, jnp.left_shift(1, first), 0)
            g = jnp.where(jblk == first, -jnp.inf, g)
        bits = jnp.broadcast_to(bits.astype(F32), (LANE, t)).T.astype(jnp.int32)

        def logits(kt):
            return _dot_nt(q, k_ref[pl.ds(pl.multiple_of(kt * t, t), t), :]) * cs

        def update(s, left, kt):
            s = s + bias_ref[jnp.maximum(2 - left, 0)]
            tiles = _lane_tiles(s)
            per_blk = blk // LANE
            out = []
            for c in range(bpt):
                vis = (jnp.right_shift(bits, kt * bpt + c) & 1) == 1
                out += [jnp.where(vis, u, NEG) for u in tiles[c * per_blk:(c + 1) * per_blk]]
            s = jnp.concatenate(out, axis=1)
            _softmax_update(s, vaug_ref[pl.ds(pl.multiple_of(kt * t, t), t), :], m_ref, acc_ref)

        _softmax_reset(m_ref, acc_ref)
        _causal_tiles(qi, logits, update, sa_ref, sb_ref)
        acc = acc_ref[...]
        o_ref[rows, :] = (acc[:, :HEAD_DIM] / acc[:, HEAD_DIM:]).astype(o_ref.dtype)
        return carry

    for j in range(2):
        query_tile(j, 0)


def _row_slab_specs(w, n_steps, step_fn, n_pad, row0=0):
    rows, n = w.shape[0] - row0, w.shape[1]
    assert rows % n_steps == 0 and (rows // n_steps) % 16 == 0 and row0 % (rows // n_steps) == 0
    tr = rows // n_steps
    first = row0 // tr
    return (pl.BlockSpec((tr, n), lambda *g: (first + step_fn(*g), 0)),
            pl.BlockSpec((tr, n_pad), lambda *g: (step_fn(*g), 0)),
            jax.ShapeDtypeStruct((rows, n_pad), BF16))


def _moba(proj, tab_a, side=()):
    b, s, _ = proj.shape
    t = ATTN_TILE
    assert s % (2 * t) == 0 and t % MOBA_BLOCK == 0 and t >= FAR_DIST and s // MOBA_BLOCK < 24
    nq = s // (2 * t)
    qo, ko, vo = (PROJ_OFFS[i] // HEAD_DIM for i in (0, 1, 2))
    side_specs = [_row_slab_specs(w, H_A * b * nq, lambda h, bi, qi: (h * b + bi) * nq + qi, w.shape[1]) for w in side]
    return pl.pallas_call(
        functools.partial(_moba_kernel, len(side)),
        grid=(H_A, b, nq),
        in_specs=[pl.BlockSpec(memory_space=pltpu.SMEM),
                  pl.BlockSpec((None, s, HEAD_DIM), lambda h, bi, qi: (bi, 0, qo + h)),
                  pl.BlockSpec((None, s, HEAD_DIM), lambda h, bi, qi: (bi, 0, ko + h)),
                  pl.BlockSpec((None, s, HEAD_DIM), lambda h, bi, qi: (bi, 0, vo + h))]
                 + [sp[0] for sp in side_specs],
        out_specs=[pl.BlockSpec((None, s, HEAD_DIM), lambda h, bi, qi: (bi, 0, h))] + [sp[1] for sp in side_specs],
        out_shape=[jax.ShapeDtypeStruct((b, s, D_MODEL), BF16)]
                  + [sp[2] for sp in side_specs],
        scratch_shapes=[pltpu.VMEM((s // MOBA_BLOCK, HEAD_DIM), F32),
                        pltpu.VMEM((3, t, t), F32),
                        pltpu.VMEM((s, 2 * HEAD_DIM), BF16),
                        pltpu.VMEM((t, t), F32), pltpu.VMEM((t, t), F32),
                        pltpu.VMEM((t, LANE), F32),
                        pltpu.VMEM((t, 2 * HEAD_DIM), F32)],
        compiler_params=_params("arbitrary", "arbitrary", "arbitrary"),
        name="moba_attention",
    )(tab_a, proj, proj, proj, *side)


def _swa_kernel(n_side, tab_ref, sink_ref, q_ref, k_ref, v_ref, cat_ref, *rest):
    del cat_ref
    side_in, o_ref, side_out = rest[:n_side], rest[n_side], rest[n_side + 1:2 * n_side + 1]
    bias_ref, sink_rep_ref, vaug_ref, sa_ref, sb_ref = rest[2 * n_side + 1:]
    for x_ref, y_ref in zip(side_in, side_out):
        _cast_pad_cols_kernel(x_ref, y_ref)
    g = pl.program_id(1)
    ti = pl.program_id(2)
    w = WINDOW
    nsub = q_ref.shape[0] // w
    cs = HEAD_DIM ** -0.5 * LOG2E

    @pl.when(ti == 0)
    def _():
        i = lax.broadcasted_iota(jnp.int32, (w, 2 * w), 0)
        j = lax.broadcasted_iota(jnp.int32, (w, 2 * w), 1)
        dist = i + w - j
        band = (dist >= 0) & (dist < w)
        for hh in range(G_B):
            head = g * G_B + hh
            bias = jnp.where(band, _bias_from_dist(dist, lambda k: tab_ref[head, k]) * LOG2E, NEG)
            bias_ref[0, hh * w:(hh + 1) * w, :] = bias
            bias_ref[1, hh * w:(hh + 1) * w, :] = jnp.where(j >= w, bias, NEG)
            sink_rep_ref[hh * w:(hh + 1) * w, :] = jnp.full((w, LANE), sink_ref[head] * LOG2E, F32)
        _augment_values(v_ref, vaug_ref)

    def band_rows(ref, blk):
        k0 = pl.multiple_of(jnp.maximum(blk - 1, 0) * w, w)
        k1 = pl.multiple_of(blk * w, w)
        return jnp.concatenate([ref[pl.ds(k0, w), :], ref[pl.ds(k1, w), :]], axis=0)

    def logits(n):
        qs = jnp.concatenate([q_ref[n * w:(n + 1) * w, hh * HEAD_DIM:(hh + 1) * HEAD_DIM] for hh in range(G_B)], axis=0)
        return _dot_nt(qs, band_rows(k_ref, ti * nsub + n)) * cs

    def finish(s, n):
        blk = ti * nsub + n
        s = s + bias_ref[jnp.where(blk == 0, 1, 0)]
        sink = sink_rep_ref[...]
        m = jnp.maximum(_row_max_rep(s), sink)
        acc = _dot(_probs(s, m), band_rows(vaug_ref, blk))
        o = acc[:, :HEAD_DIM] / (acc[:, HEAD_DIM:] + jnp.exp2(sink - m))
        for hh in range(G_B):
            o_ref[n * w:(n + 1) * w, hh * HEAD_DIM:(hh + 1) * HEAD_DIM] = o[hh * w:(hh + 1) * w, :].astype(o_ref.dtype)

    bufs = (sa_ref, sb_ref)
    sa_ref[...] = logits(0)
    for n in range(nsub):
        if n + 1 < nsub:
            bufs[(n + 1) % 2][...] = logits(n + 1)
        finish(bufs[n % 2][...], n)


def _swa(proj, tab_b, sinks, cat, side=()):
    b, s, _ = proj.shape
    tq = SWA_ROWS
    nt = s // tq
    gw = G_B * HEAD_DIM
    qo = PROJ_OFFS[3] // gw
    oo = QA_W // gw
    ko, vo = (PROJ_OFFS[i] // HEAD_DIM for i in (4, 5))
    assert PROJ_OFFS[3] % gw == 0 and QA_W % gw == 0 and s % tq == 0
    smem = pl.BlockSpec(memory_space=pltpu.SMEM)
    side_specs = [_row_slab_specs(w, b * KV_B * nt, lambda bi, g, ti: (bi * KV_B + g) * nt + ti, w.shape[1], row0)
                  for w, row0 in side]
    side = [w for w, _ in side]
    return pl.pallas_call(
        functools.partial(_swa_kernel, len(side)),
        grid=(b, KV_B, nt),
        in_specs=[smem, smem,
                  pl.BlockSpec((None, tq, gw), lambda bi, g, ti: (bi, ti, qo + g)),
                  pl.BlockSpec((None, s, HEAD_DIM), lambda bi, g, ti: (bi, 0, ko + g)),
                  pl.BlockSpec((None, s, HEAD_DIM), lambda bi, g, ti: (bi, 0, vo + g)),
                  pl.BlockSpec(memory_space=pl.ANY)] + [sp[0] for sp in side_specs],
        out_specs=[pl.BlockSpec((None, tq, gw), lambda bi, g, ti: (bi, ti, oo + g))] + [sp[1] for sp in side_specs],
        out_shape=[jax.ShapeDtypeStruct(cat.shape, cat.dtype)] + [sp[2] for sp in side_specs],
        input_output_aliases={5: 0},
        scratch_shapes=[pltpu.VMEM((2, G_B * WINDOW, 2 * WINDOW), F32),
                        pltpu.VMEM((G_B * WINDOW, LANE), F32),
                        pltpu.VMEM((s, 2 * HEAD_DIM), BF16),
                        pltpu.VMEM((G_B * WINDOW, 2 * WINDOW), F32),
                        pltpu.VMEM((G_B * WINDOW, 2 * WINDOW), F32)],
        compiler_params=_params("parallel", "parallel", "arbitrary"),
        name="swa_attention",
    )(tab_b, sinks, proj, proj, proj, cat, *side)


def _diff_kernel(lam_init, n_cols, row_tiles, tab_ref, q_ref, k_ref, v_ref, lamp_ref, subg_ref, cat_ref, *rest):
    del cat_ref
    n_side = n_cols + (row_tiles is not None)
    side_in, o_ref, side_out = rest[:n_side], rest[n_side], rest[n_side + 1:2 * n_side + 1]
    bias_ref, vaug_ref, sa_ref, sb_ref, m_ref, acc_ref = rest[2 * n_side + 1:]
    for x_ref, y_ref in zip(side_in[:n_cols], side_out[:n_cols]):
        _cast_pad_cols_kernel(x_ref, y_ref)
    side_in, side_out = side_in[n_cols:], side_out[n_cols:]
    h = pl.program_id(0)
    bi = pl.program_id(1)
    pj = pl.program_id(2)
    t = ATTN_TILE
    nq = q_ref.shape[0] // t
    cs = DIFF_DIM ** -0.5 * LOG2E

    if row_tiles is not None:
        n_src, n_dst, depth = row_tiles
        step = (h * pl.num_programs(1) + bi) * pl.num_programs(2) + pj

        @pl.when((step < depth * n_dst) & (step % n_dst < n_src))
        def _():
            side_out[0][...] = side_in[0][...].astype(BF16)

        @pl.when((step < depth * n_dst) & (step % n_dst >= n_src))
        def _():
            side_out[0][...] = jnp.zeros(side_out[0].shape, BF16)

    @pl.when((bi == 0) & (pj == 0))
    def _():
        _bias_tiles(lambda k: tab_ref[h, k], bias_ref)

    @pl.when(pj == 0)
    def _():
        _augment_values(v_ref, vaug_ref)

    lp = lamp_ref[...]
    lam = (jnp.exp(jnp.sum(lp[0:1] * lp[1:2], axis=1, keepdims=True))
           - jnp.exp(jnp.sum(lp[2:3] * lp[3:4], axis=1, keepdims=True)) + lam_init)

    def query_tile(j, carry):
        qi = pj if j == 0 else nq - 1 - pj
        rows = pl.ds(pl.multiple_of(qi * t, t), t)
        q = q_ref[rows, :]
        lane = lax.broadcasted_iota(jnp.int32, q.shape, 1)
        zero = jnp.zeros_like(q)
        q2 = jnp.concatenate([jnp.where(lane < DIFF_DIM, q, zero), jnp.where(lane >= DIFF_DIM, q, zero)], axis=0)

        def logits(kt):
            return _dot_nt(q2, k_ref[pl.ds(pl.multiple_of(kt * t, t), t), :]) * cs

        def update(s, left, kt):
            d = bias_ref[jnp.maximum(2 - left, 0)]
            s = jnp.concatenate([s[:t] + d, s[t:] + d], axis=0)
            _softmax_update(s, vaug_ref[pl.ds(pl.multiple_of(kt * t, t), t), :], m_ref, acc_ref)

        _softmax_reset(m_ref, acc_ref)
        _causal_tiles(qi, logits, update, sa_ref, sb_ref)

        acc = acc_ref[...]
        a = acc[:, :HEAD_DIM] / acc[:, HEAD_DIM:]
        o = a[:t] - lam * a[t:]
        o = o * lax.rsqrt(jnp.mean(o * o, axis=-1, keepdims=True) + LN_EPS) * subg_ref[...]
        o_ref[rows, :] = (o * (1.0 - lam_init)).astype(o_ref.dtype)
        return carry

    for j in range(2):
        query_tile(j, 0)


def _diff(proj, tab_c, lam_p, subln_g, lam_init, cat, cast_cols=(), pad_rows=None):
    b, s, _ = proj.shape
    t = ATTN_TILE
    assert s % (2 * t) == 0 and t >= FAR_DIST
    nq = s // (2 * t)
    qo, ko, vo = (PROJ_OFFS[i] // HEAD_DIM for i in (6, 7, 8))
    oo = (QA_W + QB_W) // HEAD_DIM
    col_specs = [_row_slab_specs(w, H_C * b * nq, lambda h, bi, qi: (h * b + bi) * nq + qi, w.shape[1], row0)
                 for w, row0 in cast_cols]
    side = [w for w, _ in cast_cols]
    side_in, side_out, side_shape = ([sp[i] for sp in col_specs] for i in range(3))
    row_tiles = None
    if pad_rows is not None:
        w, k_pad = pad_rows
        depth, k, n = w.shape
        tr = CAST_ROW_TILE
        assert k % tr == 0 and k_pad % tr == 0
        n_src, n_dst = k // tr, k_pad // tr
        assert depth * n_dst <= H_C * b * nq
        row_tiles = (n_src, n_dst, depth)

        def dst_tile(h, bi, qi):
            return jnp.minimum((h * b + bi) * nq + qi, depth * n_dst - 1)

        def src_tile(h, bi, qi):
            d = dst_tile(h, bi, qi)
            return (d // n_dst) * n_src + jnp.minimum(d % n_dst, n_src - 1)

        side.append(w.reshape(depth * k, n))
        side_in.append(pl.BlockSpec((tr, n), lambda h, bi, qi: (src_tile(h, bi, qi), 0)))
        side_out.append(pl.BlockSpec((tr, n), lambda h, bi, qi: (dst_tile(h, bi, qi), 0)))
        side_shape.append(jax.ShapeDtypeStruct((depth * k_pad, n), BF16))
    return pl.pallas_call(
        functools.partial(_diff_kernel, lam_init, len(cast_cols), row_tiles),
        grid=(H_C, b, nq),
        in_specs=[pl.BlockSpec(memory_space=pltpu.SMEM),
                  pl.BlockSpec((None, s, HEAD_DIM), lambda h, bi, qi: (bi, 0, qo + h)),
                  pl.BlockSpec((None, s, HEAD_DIM), lambda h, bi, qi: (bi, 0, ko + h)),
                  pl.BlockSpec((None, s, HEAD_DIM), lambda h, bi, qi: (bi, 0, vo + h)),
                  pl.BlockSpec((4, DIFF_DIM), lambda h, bi, qi: (0, 0)),
                  pl.BlockSpec((1, HEAD_DIM), lambda h, bi, qi: (0, 0)),
                  pl.BlockSpec(memory_space=pl.ANY)] + side_in,
        out_specs=[pl.BlockSpec((None, s, HEAD_DIM), lambda h, bi, qi: (bi, 0, oo + h))] + side_out,
        out_shape=[jax.ShapeDtypeStruct(cat.shape, cat.dtype)] + side_shape,
        input_output_aliases={6: 0},
        scratch_shapes=[pltpu.VMEM((3, t, t), F32),
                        pltpu.VMEM((s, 2 * HEAD_DIM), BF16),
                        pltpu.VMEM((2 * t, t), F32), pltpu.VMEM((2 * t, t), F32),
                        pltpu.VMEM((2 * t, LANE), F32),
                        pltpu.VMEM((2 * t, 2 * HEAD_DIM), F32)],
        compiler_params=_params("arbitrary", "arbitrary", "arbitrary"),
        name="diff_attention",
    )(tab_c, proj, proj, proj, lam_p, subln_g, cat, *side)


def kernel(x, c, rel_bias, w_ada, b_ada, w_in, w_o, attn_sinks, diff_lambda, diff_subln_g,
           ln_g, ln_b, w_gate, w_up, w_down):
    b, s, d = x.shape
    m = b * s
    tab = rel_bias.T
    tab_a, tab_b, tab_c = tab[:H_A], tab[H_A:H_A + H_B], tab[H_A + H_B:]

    c8 = jnp.pad(c, ((0, 8 - b), (0, 0)))
    mod = _ada_mod(c8, w_ada, b_ada[:, None, :])[:, :b]
    mod = mod.reshape(DEPTH, b, 6, 1, d)

    w_in_b = [w_in[:1].astype(BF16), None]
    h = None
    for l in range(DEPTH):
        sh1, sc1, g1, sh2, sc2, g2 = (mod[l, :, i] for i in range(6))
        if h is None:
            proj = _modulate_matmul(x, sc1, sh1, w_in_b[l], 0, BF16, *TILES_PROJ_IN, "proj_in")
        else:
            wide = TILES_PROJ_IN_WIDE[1]
            n_wide = D_PROJ // wide * wide
            proj = _matmul(h.reshape(m, d), w_in_b[l], 0, BF16, *TILES_PROJ_IN_WIDE, "proj_in", (0, n_wide))
            proj = _matmul(h.reshape(m, d), w_in_b[l], 0, BF16, *TILES_PROJ_IN, "proj_in_tail",
                           (n_wide, D_PROJ - n_wide), proj)
        proj = proj.reshape(b, s, D_PROJ)
        lam_init = 0.8 - 0.6 * math.exp(-0.3 * l)
        if l == 0:
            assert DEPTH == 2
            cat, wg_b, wu_b = _moba(proj, tab_a, (w_gate.reshape(DEPTH * d, D_FF), w_up.reshape(DEPTH * d, D_FF)))
            cat, w_o_b = _swa(proj, tab_b, attn_sinks[l], cat, ((w_o.reshape(DEPTH * d, d), 0),))
            cat, w_in1_b, wd_b = _diff(proj, tab_c, diff_lambda[l], diff_subln_g[l][None, :], lam_init, cat,
                                       ((w_in.reshape(DEPTH * d, D_PROJ), d),), (w_down, D_FF))
            wg_b, wu_b = wg_b.reshape(DEPTH, d, D_FF), wu_b.reshape(DEPTH, d, D_FF)
            w_o_b, wd_b = w_o_b.reshape(DEPTH, d, d), wd_b.reshape(DEPTH, D_FF, d)
            w_in_b[1] = w_in1_b[None]
        else:
            (cat,) = _moba(proj, tab_a)
            (cat,) = _swa(proj, tab_b, attn_sinks[l], cat)
            (cat,) = _diff(proj, tab_c, diff_lambda[l], diff_subln_g[l][None, :], lam_init, cat)
        mix = _matmul(cat.reshape(m, d), w_o_b, l, BF16, *TILES_PROJ_OUT, "proj_out").reshape(b, s, d)
        x, h = _residual_norm(x, mix, g1, ln_g[l, 0][None, :], ln_b[l, 0][None, :], sc2, sh2)

        act = _gate_up(h.reshape(m, d), wg_b, wu_b, l, *TILES_GATE_UP)
        ff = _matmul_ksplit(act, wd_b, l, BF16, *TILES_DOWN, "ffn_down").reshape(b, s, d)
        if l + 1 < DEPTH:
            nsh, nsc = mod[l + 1, :, 0], mod[l + 1, :, 1]
            x, h = _residual_norm(x, ff, g2, ln_g[l, 1][None, :], ln_b[l, 1][None, :], nsc, nsh)
        else:
            (x,) = _residual_norm(x, ff, g2, ln_g[l, 1][None, :], ln_b[l, 1][None, :])
    return x
```

```python
import functools
import math

import numpy as np
import jax
import jax.numpy as jnp
from jax import lax
from jax.experimental import pallas as pl
from jax.experimental.pallas import tpu as pltpu

D_MODEL = 4096
DEPTH = 2
HEAD_DIM = 128
N_HEADS = D_MODEL // HEAD_DIM
H_A = N_HEADS // 4
H_C = N_HEADS // 4
H_B = N_HEADS - H_A - H_C
KV_B = max(1, H_B // 8)
G_B = H_B // KV_B
DIFF_DIM = HEAD_DIM // 2
MOBA_BLOCK = 256
MOBA_TOPK = 3
WINDOW = 128
N_BUCKETS = 32
MAX_DISTANCE = 128
D_FF = -(-8 * D_MODEL // (3 * 256)) * 256
ALPHA = (2.0 * DEPTH) ** 0.25
LN_EPS = 1e-5
NEG = -1e30
LOG2E = math.log2(math.e)

QA_W = H_A * HEAD_DIM
QB_W = H_B * HEAD_DIM
KB_W = KV_B * HEAD_DIM
QC_W = H_C * 2 * DIFF_DIM
VC_W = H_C * HEAD_DIM
PROJ_SIZES = (QA_W, QA_W, QA_W, QB_W, KB_W, KB_W, QC_W, QC_W, VC_W)
PROJ_OFFS = tuple(int(sum(PROJ_SIZES[:i])) for i in range(len(PROJ_SIZES)))
D_PROJ = sum(PROJ_SIZES)

LANE = 128
ATTN_TILE = 512
TILE_UNROLL = 4
CAST_ROW_TILE = 256
VMEM_LIMIT = 56 * 1024 * 1024
TILES_PROJ_IN = (1024, 512)
TILES_PROJ_IN_WIDE = (1024, 1024)
TILES_PROJ_OUT = (1024, 1024)
TILES_GATE_UP = (2048, 256)
TILES_DOWN = (1024, 1024, D_FF // 2)
NORM_ROWS = 512
NORM_CHUNK = 64
ADA_COLS = 512
SWA_ROWS = 1024

F32 = jnp.float32
BF16 = jnp.bfloat16


def _bucket_upper_bounds():
    n = np.arange(0, 4 * MAX_DISTANCE, dtype=np.int64)
    max_exact = N_BUCKETS // 2
    nf = np.maximum(n, 1).astype(np.float32)
    large = max_exact + (np.log(nf / np.float32(max_exact)) / np.float32(math.log(MAX_DISTANCE / max_exact))
                         * np.float32(N_BUCKETS - max_exact)).astype(np.int32)
    large = np.minimum(large, N_BUCKETS - 1)
    bucket = np.where(n < max_exact, n, large)
    assert bucket[-1] == N_BUCKETS - 1 and np.all(np.diff(bucket) >= 0)
    return tuple(int(np.argmax(bucket > k)) for k in range(N_BUCKETS - 1))


BUCKET_UPPER = _bucket_upper_bounds()
FAR_DIST = BUCKET_UPPER[-1]


def _bias_from_dist(dist, tab):
    b = jnp.full(dist.shape, tab(N_BUCKETS - 1), F32)
    for k in range(N_BUCKETS - 2, -1, -1):
        b = jnp.where(dist < BUCKET_UPPER[k], tab(k), b)
    return b


def _dot(a, b):
    return jnp.dot(a, b, preferred_element_type=F32)


def _dot_nt(a, b):
    return lax.dot_general(a, b, (((1,), (1,)), ((), ())), preferred_element_type=F32)


def _params(*sem):
    return pltpu.CompilerParams(dimension_semantics=sem, vmem_limit_bytes=VMEM_LIMIT)


def _mod_kernel(c_ref, w_ref, b_ref, o_ref):
    c = c_ref[...]
    s = (c * jax.nn.sigmoid(c)).astype(BF16)
    o_ref[...] = _dot(s, w_ref[...].astype(BF16)) + b_ref[...]


def _ada_mod(c8, w_ada, b_ada3):
    depth, d, n = w_ada.shape
    tn = ADA_COLS
    return pl.pallas_call(
        _mod_kernel,
        grid=(depth, n // tn),
        in_specs=[pl.BlockSpec((8, d), lambda l, j: (0, 0)),
                  pl.BlockSpec((None, d, tn), lambda l, j: (l, 0, j)),
                  pl.BlockSpec((None, 1, tn), lambda l, j: (l, 0, j))],
        out_specs=pl.BlockSpec((None, 8, tn), lambda l, j: (l, 0, j)),
        out_shape=jax.ShapeDtypeStruct((depth, 8, n), F32),
        compiler_params=_params("parallel", "parallel"),
        name="ada_mod",
    )(c8, w_ada, b_ada3)


def _mm_kernel(x_ref, w_ref, *rest):
    o_ref = rest[-1]
    o_ref[...] = _dot(x_ref[...], w_ref[...]).astype(o_ref.dtype)


def _matmul(x, w, l, out_dtype, tm, tn, name, cols=None, into=None):
    m, k = x.shape
    _, _, n = w.shape
    c0, nc = cols or (0, n)
    assert c0 % tn == 0 and nc % tn == 0
    j0 = c0 // tn
    in_specs = [pl.BlockSpec((tm, k), lambda i, j: (i, 0)),
                pl.BlockSpec((None, k, tn), lambda i, j: (l, 0, j0 + j))]
    args = [x, w]
    if into is not None:
        in_specs.append(pl.BlockSpec(memory_space=pl.ANY))
        args.append(into)
    return pl.pallas_call(
        _mm_kernel,
        grid=(m // tm, nc // tn),
        in_specs=in_specs,
        out_specs=pl.BlockSpec((tm, tn), lambda i, j: (i, j0 + j)),
        out_shape=jax.ShapeDtypeStruct((m, n), out_dtype),
        input_output_aliases={} if into is None else {2: 0},
        compiler_params=_params("parallel", "arbitrary"),
        name=name,
    )(*args)


def _mod_mm_kernel(x_ref, sc_ref, sh_ref, w_ref, o_ref, h_ref):
    @pl.when(pl.program_id(1) == 0)
    def _():
        h_ref[...] = (x_ref[...] * (1.0 + sc_ref[...]) + sh_ref[...]).astype(BF16)

    o_ref[...] = _dot(h_ref[...], w_ref[...]).astype(o_ref.dtype)


def _modulate_matmul(x, sc, sh, w, l, out_dtype, tm, tn, name):
    b, s, k = x.shape
    _, _, n = w.shape
    assert s % tm == 0
    per_b = s // tm
    vec = pl.BlockSpec((None, 1, k), lambda i, j: (i // per_b, 0, 0))
    return pl.pallas_call(
        _mod_mm_kernel,
        grid=(b * per_b, n // tn),
        in_specs=[pl.BlockSpec((tm, k), lambda i, j: (i, 0)), vec, vec,
                  pl.BlockSpec((None, k, tn), lambda i, j: (l, 0, j))],
        out_specs=pl.BlockSpec((tm, tn), lambda i, j: (i, j)),
        out_shape=jax.ShapeDtypeStruct((b * s, n), out_dtype),
        scratch_shapes=[pltpu.VMEM((tm, k), BF16)],
        compiler_params=_params("parallel", "arbitrary"),
        name=name,
    )(x.reshape(b * s, k), sc, sh, w)


def _mm_acc_kernel(x_ref, w_ref, o_ref, acc_ref):
    kk = pl.program_id(2)
    part = _dot(x_ref[...], w_ref[...])

    @pl.when(kk == 0)
    def _():
        acc_ref[...] = part

    @pl.when((kk > 0) & (kk < pl.num_programs(2) - 1))
    def _():
        acc_ref[...] += part

    @pl.when(kk == pl.num_programs(2) - 1)
    def _():
        o_ref[...] = (acc_ref[...] + part).astype(o_ref.dtype)


def _matmul_ksplit(x, w, l, out_dtype, tm, tn, tk, name):
    m, k = x.shape
    _, _, n = w.shape
    assert k // tk >= 2
    return pl.pallas_call(
        _mm_acc_kernel,
        grid=(m // tm, n // tn, k // tk),
        in_specs=[pl.BlockSpec((tm, tk), lambda i, j, kk: (i, kk)),
                  pl.BlockSpec((None, tk, tn), lambda i, j, kk: (l, kk, j))],
        out_specs=pl.BlockSpec((tm, tn), lambda i, j, kk: (i, j)),
        out_shape=jax.ShapeDtypeStruct((m, n), out_dtype),
        scratch_shapes=[pltpu.VMEM((tm, tn), F32)],
        compiler_params=_params("parallel", "parallel", "arbitrary"),
        name=name,
    )(x, w)


def _cast_pad_cols_kernel(x_ref, o_ref):
    n = x_ref.shape[1]
    o_ref[:, :n] = x_ref[...].astype(BF16)
    if o_ref.shape[1] > n:
        o_ref[:, n:] = jnp.zeros((o_ref.shape[0], o_ref.shape[1] - n), BF16)


def _gate_up_kernel(h_ref, wg_ref, wu_ref, o_ref):
    h = h_ref[...]
    g = _dot(h, wg_ref[...])
    u = _dot(h, wu_ref[...])
    o_ref[...] = (g * jax.nn.sigmoid(g) * u).astype(BF16)


def _gate_up(h, wg, wu, l, tm, tf):
    m, k = h.shape
    _, _, f = wg.shape
    wspec = pl.BlockSpec((None, k, tf), lambda i, j: (l, 0, j))
    return pl.pallas_call(
        _gate_up_kernel,
        grid=(m // tm, f // tf),
        in_specs=[pl.BlockSpec((tm, k), lambda i, j: (i, 0)), wspec, wspec],
        out_specs=pl.BlockSpec((tm, tf), lambda i, j: (i, j)),
        out_shape=jax.ShapeDtypeStruct((m, f), BF16),
        compiler_params=_params("parallel", "arbitrary"),
        name="ffn_gate_up",
    )(h, wg, wu)


def _norm_kernel(with_h, x_ref, y_ref, gate_ref, lg_ref, lb_ref, *rest):
    def chunk(c, carry):
        r = pl.ds(pl.multiple_of(c * NORM_CHUNK, NORM_CHUNK), NORM_CHUNK)
        z = ALPHA * x_ref[r, :] + gate_ref[...] * y_ref[r, :].astype(F32)
        mu = jnp.mean(z, axis=-1, keepdims=True)
        zc = z - mu
        var = jnp.mean(zc * zc, axis=-1, keepdims=True)
        xn = zc * lax.rsqrt(var + LN_EPS) * lg_ref[...] + lb_ref[...]
        rest[-2 if with_h else -1][r, :] = xn
        if with_h:
            sc_ref, sh_ref = rest[0], rest[1]
            rest[-1][r, :] = (xn * (1.0 + sc_ref[...]) + sh_ref[...]).astype(BF16)
        return carry

    lax.fori_loop(0, x_ref.shape[0] // NORM_CHUNK, chunk, 0)


def _residual_norm(x, y, gate, lg, lb, sc=None, sh=None):
    b, s, d = x.shape
    tm = NORM_ROWS
    with_h = sc is not None
    tile = pl.BlockSpec((None, tm, d), lambda i, j: (i, j, 0))
    vec = pl.BlockSpec((None, 1, d), lambda i, j: (i, 0, 0))
    par = pl.BlockSpec((1, d), lambda i, j: (0, 0))
    in_specs = [tile, tile, vec, par, par]
    args = [x, y, gate, lg, lb]
    out_specs = [tile]
    out_shape = [jax.ShapeDtypeStruct((b, s, d), F32)]
    if with_h:
        in_specs += [vec, vec]
        args += [sc, sh]
        out_specs.append(tile)
        out_shape.append(jax.ShapeDtypeStruct((b, s, d), BF16))
    return pl.pallas_call(
        functools.partial(_norm_kernel, with_h),
        grid=(b, s // tm),
        in_specs=in_specs,
        out_specs=out_specs,
        out_shape=out_shape,
        compiler_params=_params("parallel", "parallel"),
        name="residual_norm",
    )(*args)


def _lane_tiles(a):
    return [a[:, i * LANE:(i + 1) * LANE] for i in range(a.shape[1] // LANE)]


def _row_max_rep(s):
    tiles = _lane_tiles(s)
    t = tiles[0]
    for u in tiles[1:]:
        t = jnp.maximum(t, u)
    return jnp.broadcast_to(jnp.max(t, axis=1, keepdims=True), t.shape)


def _probs(s, m_rep):
    return jnp.concatenate([jnp.exp2(u - m_rep) for u in _lane_tiles(s)], axis=1).astype(BF16)


def _softmax_reset(m_ref, acc_ref):
    m_ref[...] = jnp.full(m_ref.shape, NEG, F32)
    acc_ref[...] = jnp.zeros(acc_ref.shape, F32)


def _softmax_update(s, v_aug, m_ref, acc_ref):
    m_old = m_ref[...]
    m = jnp.maximum(m_old, _row_max_rep(s))
    corr = jnp.exp2(m_old - m)
    m_ref[...] = m
    acc_ref[...] = jnp.concatenate([corr, corr], axis=1) * acc_ref[...] + _dot(_probs(s, m), v_aug)


def _augment_values(v_ref, vaug_ref):
    vaug_ref[:, :HEAD_DIM] = v_ref[...]
    vaug_ref[:, HEAD_DIM:] = jnp.ones((v_ref.shape[0], HEAD_DIM), BF16)


def _bias_tiles(tab, bias_ref):
    t = bias_ref.shape[1]
    i = lax.broadcasted_iota(jnp.int32, (t, t), 0)
    j = lax.broadcasted_iota(jnp.int32, (t, t), 1)
    far = tab(N_BUCKETS - 1)
    d0 = (_bias_from_dist(i - j, tab) - far) * LOG2E
    bias_ref[2] = jnp.where(i >= j, d0, NEG)
    bias_ref[1] = (_bias_from_dist(i - j + t, tab) - far) * LOG2E
    bias_ref[0] = jnp.zeros((t, t), F32)


def _causal_tiles(qi, logits_fn, update_fn, sa_ref, sb_ref):
    n = qi + 1
    bufs = (sa_ref, sb_ref)
    sa_ref[...] = logits_fn(qi)

    def group(left0, size):
        for i in range(size):
            cur, nxt = bufs[i % 2], bufs[(i + 1) % 2]
            kt = qi - (left0 + i)
            nxt[...] = logits_fn(jnp.maximum(kt - 1, 0))
            update_fn(cur[...], left0 + i, kt)

    def quad(p, carry):
        group(TILE_UNROLL * p, TILE_UNROLL)
        return carry

    quads = n // TILE_UNROLL
    lax.fori_loop(0, quads, quad, 0)
    rest = n - TILE_UNROLL * quads

    @pl.when(rest >= 2)
    def _():
        group(TILE_UNROLL * quads, 2)

    @pl.when(rest % 2 == 1)
    def _():
        update_fn(sa_ref[...], qi, 0)


def _moba_kernel(n_side, tab_ref, q_ref, k_ref, v_ref, *rest):
    side_in, o_ref, side_out = rest[:n_side], rest[n_side], rest[n_side + 1:2 * n_side + 1]
    kmean_ref, bias_ref, vaug_ref, sa_ref, sb_ref, m_ref, acc_ref = rest[2 * n_side + 1:]
    for x_ref, y_ref in zip(side_in, side_out):
        _cast_pad_cols_kernel(x_ref, y_ref)
    h = pl.program_id(0)
    bi = pl.program_id(1)
    pj = pl.program_id(2)
    t = ATTN_TILE
    nq = q_ref.shape[0] // t
    blk = MOBA_BLOCK
    bpt = t // blk
    nb = k_ref.shape[0] // blk
    cs = HEAD_DIM ** -0.5 * LOG2E

    @pl.when((bi == 0) & (pj == 0))
    def _():
        _bias_tiles(lambda k: tab_ref[h, k], bias_ref)

    @pl.when(pj == 0)
    def _():
        for jb in range(nb):
            kb = k_ref[jb * blk:(jb + 1) * blk, :].astype(F32)
            kmean_ref[jb:jb + 1, :] = jnp.sum(kb, axis=0, keepdims=True) * (1.0 / blk)
        _augment_values(v_ref, vaug_ref)

    def query_tile(j, carry):
        qi = pj if j == 0 else nq - 1 - pj
        rows = pl.ds(pl.multiple_of(qi * t, t), t)
        q = q_ref[rows, :]
        km = kmean_ref[...]
        km_hi = km.astype(BF16)
        km_lo = (km - km_hi.astype(F32)).astype(BF16)
        gate = _dot_nt(km_hi, q) + _dot_nt(km_lo, q)
        jblk = lax.broadcasted_iota(jnp.int32, gate.shape, 0)
        qpos = lax.broadcasted_iota(jnp.int32, (1, t), 1)
        qblk = qi * bpt + qpos // blk
        g = jnp.where(jblk < qblk, gate, -jnp.inf)
        bits = jnp.left_shift(1, qblk)
        for r in range(MOBA_TOPK):
            mx = jnp.max(g, axis=0, keepdims=True)
            first = jnp.min(jnp.where(g == mx, jblk, nb), axis=0, keepdims=True)
            bits = bits | jnp.where(r < qblk, jnp.left_shift(1, first), 0)
            g = jnp.where(jblk == first, -jnp.inf, g)
        bits = jnp.broadcast_to(bits.astype(F32), (LANE, t)).T.astype(jnp.int32)

        def logits(kt):
            return _dot_nt(q, k_ref[pl.ds(pl.multiple_of(kt * t, t), t), :]) * cs

        def update(s, left, kt):
            s = s + bias_ref[jnp.maximum(2 - left, 0)]
            tiles = _lane_tiles(s)
            per_blk = blk // LANE
            out = []
            for c in range(bpt):
                vis = (jnp.right_shift(bits, kt * bpt + c) & 1) == 1
                out += [jnp.where(vis, u, NEG) for u in tiles[c * per_blk:(c + 1) * per_blk]]
            s = jnp.concatenate(out, axis=1)
            _softmax_update(s, vaug_ref[pl.ds(pl.multiple_of(kt * t, t), t), :], m_ref, acc_ref)

        _softmax_reset(m_ref, acc_ref)
        _causal_tiles(qi, logits, update, sa_ref, sb_ref)
        acc = acc_ref[...]
        o_ref[rows, :] = (acc[:, :HEAD_DIM] / acc[:, HEAD_DIM:]).astype(o_ref.dtype)
        return carry

    for j in range(2):
        query_tile(j, 0)


def _row_slab_specs(w, n_steps, step_fn, n_pad, row0=0):
    rows, n = w.shape[0] - row0, w.shape[1]
    assert rows % n_steps == 0 and (rows // n_steps) % 16 == 0 and row0 % (rows // n_steps) == 0
    tr = rows // n_steps
    first = row0 // tr
    return (pl.BlockSpec((tr, n), lambda *g: (first + step_fn(*g), 0)),
            pl.BlockSpec((tr, n_pad), lambda *g: (step_fn(*g), 0)),
            jax.ShapeDtypeStruct((rows, n_pad), BF16))


def _moba(proj, tab_a, side=()):
    b, s, _ = proj.shape
    t = ATTN_TILE
    assert s % (2 * t) == 0 and t % MOBA_BLOCK == 0 and t >= FAR_DIST and s // MOBA_BLOCK < 24
    nq = s // (2 * t)
    qo, ko, vo = (PROJ_OFFS[i] // HEAD_DIM for i in (0, 1, 2))
    side_specs = [_row_slab_specs(w, H_A * b * nq, lambda h, bi, qi: (h * b + bi) * nq + qi, w.shape[1]) for w in side]
    return pl.pallas_call(
        functools.partial(_moba_kernel, len(side)),
        grid=(H_A, b, nq),
        in_specs=[pl.BlockSpec(memory_space=pltpu.SMEM),
                  pl.BlockSpec((None, s, HEAD_DIM), lambda h, bi, qi: (bi, 0, qo + h)),
                  pl.BlockSpec((None, s, HEAD_DIM), lambda h, bi, qi: (bi, 0, ko + h)),
                  pl.BlockSpec((None, s, HEAD_DIM), lambda h, bi, qi: (bi, 0, vo + h))]
                 + [sp[0] for sp in side_specs],
        out_specs=[pl.BlockSpec((None, s, HEAD_DIM), lambda h, bi, qi: (bi, 0, h))] + [sp[1] for sp in side_specs],
        out_shape=[jax.ShapeDtypeStruct((b, s, D_MODEL), BF16)]
                  + [sp[2] for sp in side_specs],
        scratch_shapes=[pltpu.VMEM((s // MOBA_BLOCK, HEAD_DIM), F32),
                        pltpu.VMEM((3, t, t), F32),
                        pltpu.VMEM((s, 2 * HEAD_DIM), BF16),
                        pltpu.VMEM((t, t), F32), pltpu.VMEM((t, t), F32),
                        pltpu.VMEM((t, LANE), F32),
                        pltpu.VMEM((t, 2 * HEAD_DIM), F32)],
        compiler_params=_params("arbitrary", "arbitrary", "arbitrary"),
        name="moba_attention",
    )(tab_a, proj, proj, proj, *side)


def _swa_kernel(n_side, tab_ref, sink_ref, q_ref, k_ref, v_ref, cat_ref, *rest):
    del cat_ref
    side_in, o_ref, side_out = rest[:n_side], rest[n_side], rest[n_side + 1:2 * n_side + 1]
    bias_ref, sink_rep_ref, vaug_ref, sa_ref, sb_ref = rest[2 * n_side + 1:]
    for x_ref, y_ref in zip(side_in, side_out):
        _cast_pad_cols_kernel(x_ref, y_ref)
    g = pl.program_id(1)
    ti = pl.program_id(2)
    w = WINDOW
    nsub = q_ref.shape[0] // w
    cs = HEAD_DIM ** -0.5 * LOG2E

    @pl.when(ti == 0)
    def _():
        i = lax.broadcasted_iota(jnp.int32, (w, 2 * w), 0)
        j = lax.broadcasted_iota(jnp.int32, (w, 2 * w), 1)
        dist = i + w - j
        band = (dist >= 0) & (dist < w)
        for hh in range(G_B):
            head = g * G_B + hh
            bias = jnp.where(band, _bias_from_dist(dist, lambda k: tab_ref[head, k]) * LOG2E, NEG)
            bias_ref[0, hh * w:(hh + 1) * w, :] = bias
            bias_ref[1, hh * w:(hh + 1) * w, :] = jnp.where(j >= w, bias, NEG)
            sink_rep_ref[hh * w:(hh + 1) * w, :] = jnp.full((w, LANE), sink_ref[head] * LOG2E, F32)
        _augment_values(v_ref, vaug_ref)

    def band_rows(ref, blk):
        k0 = pl.multiple_of(jnp.maximum(blk - 1, 0) * w, w)
        k1 = pl.multiple_of(blk * w, w)
        return jnp.concatenate([ref[pl.ds(k0, w), :], ref[pl.ds(k1, w), :]], axis=0)

    def logits(n):
        qs = jnp.concatenate([q_ref[n * w:(n + 1) * w, hh * HEAD_DIM:(hh + 1) * HEAD_DIM] for hh in range(G_B)], axis=0)
        return _dot_nt(qs, band_rows(k_ref, ti * nsub + n)) * cs

    def finish(s, n):
        blk = ti * nsub + n
        s = s + bias_ref[jnp.where(blk == 0, 1, 0)]
        sink = sink_rep_ref[...]
        m = jnp.maximum(_row_max_rep(s), sink)
        acc = _dot(_probs(s, m), band_rows(vaug_ref, blk))
        o = acc[:, :HEAD_DIM] / (acc[:, HEAD_DIM:] + jnp.exp2(sink - m))
        for hh in range(G_B):
            o_ref[n * w:(n + 1) * w, hh * HEAD_DIM:(hh + 1) * HEAD_DIM] = o[hh * w:(hh + 1) * w, :].astype(o_ref.dtype)

    bufs = (sa_ref, sb_ref)
    sa_ref[...] = logits(0)
    for n in range(nsub):
        if n + 1 < nsub:
            bufs[(n + 1) % 2][...] = logits(n + 1)
        finish(bufs[n % 2][...], n)


def _swa(proj, tab_b, sinks, cat, side=()):
    b, s, _ = proj.shape
    tq = SWA_ROWS
    nt = s // tq
    gw = G_B * HEAD_DIM
    qo = PROJ_OFFS[3] // gw
    oo = QA_W // gw
    ko, vo = (PROJ_OFFS[i] // HEAD_DIM for i in (4, 5))
    assert PROJ_OFFS[3] % gw == 0 and QA_W % gw == 0 and s % tq == 0
    smem = pl.BlockSpec(memory_space=pltpu.SMEM)
    side_specs = [_row_slab_specs(w, b * KV_B * nt, lambda bi, g, ti: (bi * KV_B + g) * nt + ti, w.shape[1], row0)
                  for w, row0 in side]
    side = [w for w, _ in side]
    return pl.pallas_call(
        functools.partial(_swa_kernel, len(side)),
        grid=(b, KV_B, nt),
        in_specs=[smem, smem,
                  pl.BlockSpec((None, tq, gw), lambda bi, g, ti: (bi, ti, qo + g)),
                  pl.BlockSpec((None, s, HEAD_DIM), lambda bi, g, ti: (bi, 0, ko + g)),
                  pl.BlockSpec((None, s, HEAD_DIM), lambda bi, g, ti: (bi, 0, vo + g)),
                  pl.BlockSpec(memory_space=pl.ANY)] + [sp[0] for sp in side_specs],
        out_specs=[pl.BlockSpec((None, tq, gw), lambda bi, g, ti: (bi, ti, oo + g))] + [sp[1] for sp in side_specs],
        out_shape=[jax.ShapeDtypeStruct(cat.shape, cat.dtype)] + [sp[2] for sp in side_specs],
        input_output_aliases={5: 0},
        scratch_shapes=[pltpu.VMEM((2, G_B * WINDOW, 2 * WINDOW), F32),
                        pltpu.VMEM((G_B * WINDOW, LANE), F32),
                        pltpu.VMEM((s, 2 * HEAD_DIM), BF16),
                        pltpu.VMEM((G_B * WINDOW, 2 * WINDOW), F32),
                        pltpu.VMEM((G_B * WINDOW, 2 * WINDOW), F32)],
        compiler_params=_params("parallel", "parallel", "arbitrary"),
        name="swa_attention",
    )(tab_b, sinks, proj, proj, proj, cat, *side)


def _diff_kernel(lam_init, n_cols, row_tiles, tab_ref, q_ref, k_ref, v_ref, lamp_ref, subg_ref, cat_ref, *rest):
    del cat_ref
    n_side = n_cols + (row_tiles is not None)
    side_in, o_ref, side_out = rest[:n_side], rest[n_side], rest[n_side + 1:2 * n_side + 1]
    bias_ref, vaug_ref, sa_ref, sb_ref, m_ref, acc_ref = rest[2 * n_side + 1:]
    for x_ref, y_ref in zip(side_in[:n_cols], side_out[:n_cols]):
        _cast_pad_cols_kernel(x_ref, y_ref)
    side_in, side_out = side_in[n_cols:], side_out[n_cols:]
    h = pl.program_id(0)
    bi = pl.program_id(1)
    pj = pl.program_id(2)
    t = ATTN_TILE
    nq = q_ref.shape[0] // t
    cs = DIFF_DIM ** -0.5 * LOG2E

    if row_tiles is not None:
        n_src, n_dst, depth = row_tiles
        step = (h * pl.num_programs(1) + bi) * pl.num_programs(2) + pj

        @pl.when((step < depth * n_dst) & (step % n_dst < n_src))
        def _():
            side_out[0][...] = side_in[0][...].astype(BF16)

        @pl.when((step < depth * n_dst) & (step % n_dst >= n_src))
        def _():
            side_out[0][...] = jnp.zeros(side_out[0].shape, BF16)

    @pl.when((bi == 0) & (pj == 0))
    def _():
        _bias_tiles(lambda k: tab_ref[h, k], bias_ref)

    @pl.when(pj == 0)
    def _():
        _augment_values(v_ref, vaug_ref)

    lp = lamp_ref[...]
    lam = (jnp.exp(jnp.sum(lp[0:1] * lp[1:2], axis=1, keepdims=True))
           - jnp.exp(jnp.sum(lp[2:3] * lp[3:4], axis=1, keepdims=True)) + lam_init)

    def query_tile(j, carry):
        qi = pj if j == 0 else nq - 1 - pj
        rows = pl.ds(pl.multiple_of(qi * t, t), t)
        q = q_ref[rows, :]
        lane = lax.broadcasted_iota(jnp.int32, q.shape, 1)
        zero = jnp.zeros_like(q)
        q2 = jnp.concatenate([jnp.where(lane < DIFF_DIM, q, zero), jnp.where(lane >= DIFF_DIM, q, zero)], axis=0)

        def logits(kt):
            return _dot_nt(q2, k_ref[pl.ds(pl.multiple_of(kt * t, t), t), :]) * cs

        def update(s, left, kt):
            d = bias_ref[jnp.maximum(2 - left, 0)]
            s = jnp.concatenate([s[:t] + d, s[t:] + d], axis=0)
            _softmax_update(s, vaug_ref[pl.ds(pl.multiple_of(kt * t, t), t), :], m_ref, acc_ref)

        _softmax_reset(m_ref, acc_ref)
        _causal_tiles(qi, logits, update, sa_ref, sb_ref)

        acc = acc_ref[...]
        a = acc[:, :HEAD_DIM] / acc[:, HEAD_DIM:]
        o = a[:t] - lam * a[t:]
        o = o * lax.rsqrt(jnp.mean(o * o, axis=-1, keepdims=True) + LN_EPS) * subg_ref[...]
        o_ref[rows, :] = (o * (1.0 - lam_init)).astype(o_ref.dtype)
        return carry

    for j in range(2):
        query_tile(j, 0)


def _diff(proj, tab_c, lam_p, subln_g, lam_init, cat, cast_cols=(), pad_rows=None):
    b, s, _ = proj.shape
    t = ATTN_TILE
    assert s % (2 * t) == 0 and t >= FAR_DIST
    nq = s // (2 * t)
    qo, ko, vo = (PROJ_OFFS[i] // HEAD_DIM for i in (6, 7, 8))
    oo = (QA_W + QB_W) // HEAD_DIM
    col_specs = [_row_slab_specs(w, H_C * b * nq, lambda h, bi, qi: (h * b + bi) * nq + qi, w.shape[1], row0)
                 for w, row0 in cast_cols]
    side = [w for w, _ in cast_cols]
    side_in, side_out, side_shape = ([sp[i] for sp in col_specs] for i in range(3))
    row_tiles = None
    if pad_rows is not None:
        w, k_pad = pad_rows
        depth, k, n = w.shape
        tr = CAST_ROW_TILE
        assert k % tr == 0 and k_pad % tr == 0
        n_src, n_dst = k // tr, k_pad // tr
        assert depth * n_dst <= H_C * b * nq
        row_tiles = (n_src, n_dst, depth)

        def dst_tile(h, bi, qi):
            return jnp.minimum((h * b + bi) * nq + qi, depth * n_dst - 1)

        def src_tile(h, bi, qi):
            d = dst_tile(h, bi, qi)
            return (d // n_dst) * n_src + jnp.minimum(d % n_dst, n_src - 1)

        side.append(w.reshape(depth * k, n))
        side_in.append(pl.BlockSpec((tr, n), lambda h, bi, qi: (src_tile(h, bi, qi), 0)))
        side_out.append(pl.BlockSpec((tr, n), lambda h, bi, qi: (dst_tile(h, bi, qi), 0)))
        side_shape.append(jax.ShapeDtypeStruct((depth * k_pad, n), BF16))
    return pl.pallas_call(
        functools.partial(_diff_kernel, lam_init, len(cast_cols), row_tiles),
        grid=(H_C, b, nq),
        in_specs=[pl.BlockSpec(memory_space=pltpu.SMEM),
                  pl.BlockSpec((None, s, HEAD_DIM), lambda h, bi, qi: (bi, 0, qo + h)),
                  pl.BlockSpec((None, s, HEAD_DIM), lambda h, bi, qi: (bi, 0, ko + h)),
                  pl.BlockSpec((None, s, HEAD_DIM), lambda h, bi, qi: (bi, 0, vo + h)),
                  pl.BlockSpec((4, DIFF_DIM), lambda h, bi, qi: (0, 0)),
                  pl.BlockSpec((1, HEAD_DIM), lambda h, bi, qi: (0, 0)),
                  pl.BlockSpec(memory_space=pl.ANY)] + side_in,
        out_specs=[pl.BlockSpec((None, s, HEAD_DIM), lambda h, bi, qi: (bi, 0, oo + h))] + side_out,
        out_shape=[jax.ShapeDtypeStruct(cat.shape, cat.dtype)] + side_shape,
        input_output_aliases={6: 0},
        scratch_shapes=[pltpu.VMEM((3, t, t), F32),
                        pltpu.VMEM((s, 2 * HEAD_DIM), BF16),
                        pltpu.VMEM((2 * t, t), F32), pltpu.VMEM((2 * t, t), F32),
                        pltpu.VMEM((2 * t, LANE), F32),
                        pltpu.VMEM((2 * t, 2 * HEAD_DIM), F32)],
        compiler_params=_params("arbitrary", "arbitrary", "arbitrary"),
        name="diff_attention",
    )(tab_c, proj, proj, proj, lam_p, subln_g, cat, *side)


def kernel(x, c, rel_bias, w_ada, b_ada, w_in, w_o, attn_sinks, diff_lambda, diff_subln_g,
           ln_g, ln_b, w_gate, w_up, w_down):
    b, s, d = x.shape
    m = b * s
    tab = rel_bias.T
    tab_a, tab_b, tab_c = tab[:H_A], tab[H_A:H_A + H_B], tab[H_A + H_B:]

    c8 = jnp.pad(c, ((0, 8 - b), (0, 0)))
    mod = _ada_mod(c8, w_ada, b_ada[:, None, :])[:, :b]
    mod = mod.reshape(DEPTH, b, 6, 1, d)

    w_in_b = [w_in[:1].astype(BF16), None]
    h = None
    for l in range(DEPTH):
        sh1, sc1, g1, sh2, sc2, g2 = (mod[l, :, i] for i in range(6))
        if h is None:
            proj = _modulate_matmul(x, sc1, sh1, w_in_b[l], 0, BF16, *TILES_PROJ_IN, "proj_in")
        else:
            wide = TILES_PROJ_IN_WIDE[1]
            n_wide = D_PROJ // wide * wide
            proj = _matmul(h.reshape(m, d), w_in_b[l], 0, BF16, *TILES_PROJ_IN_WIDE, "proj_in", (0, n_wide))
            proj = _matmul(h.reshape(m, d), w_in_b[l], 0, BF16, *TILES_PROJ_IN, "proj_in_tail",
                           (n_wide, D_PROJ - n_wide), proj)
        proj = proj.reshape(b, s, D_PROJ)
        lam_init = 0.8 - 0.6 * math.exp(-0.3 * l)
        if l == 0:
            assert DEPTH == 2
            cat, wg_b, wu_b = _moba(proj, tab_a, (w_gate.reshape(DEPTH * d, D_FF), w_up.reshape(DEPTH * d, D_FF)))
            cat, w_o_b = _swa(proj, tab_b, attn_sinks[l], cat, ((w_o.reshape(DEPTH * d, d), 0),))
            cat, w_in1_b, wd_b = _diff(proj, tab_c, diff_lambda[l], diff_subln_g[l][None, :], lam_init, cat,
                                       ((w_in.reshape(DEPTH * d, D_PROJ), d),), (w_down, D_FF))
            wg_b, wu_b = wg_b.reshape(DEPTH, d, D_FF), wu_b.reshape(DEPTH, d, D_FF)
            w_o_b, wd_b = w_o_b.reshape(DEPTH, d, d), wd_b.reshape(DEPTH, D_FF, d)
            w_in_b[1] = w_in1_b[None]
        else:
            (cat,) = _moba(proj, tab_a)
            (cat,) = _swa(proj, tab_b, attn_sinks[l], cat)
            (cat,) = _diff(proj, tab_c, diff_lambda[l], diff_subln_g[l][None, :], lam_init, cat)
        mix = _matmul(cat.reshape(m, d), w_o_b, l, BF16, *TILES_PROJ_OUT, "proj_out").reshape(b, s, d)
        x, h = _residual_norm(x, mix, g1, ln_g[l, 0][None, :], ln_b[l, 0][None, :], sc2, sh2)

        act = _gate_up(h.reshape(m, d), wg_b, wu_b, l, *TILES_GATE_UP)
        ff = _matmul_ksplit(act, wd_b, l, BF16, *TILES_DOWN, "ffn_down").reshape(b, s, d)
        if l + 1 < DEPTH:
            nsh, nsc = mod[l + 1, :, 0], mod[l + 1, :, 1]
            x, h = _residual_norm(x, ff, g2, ln_g[l, 1][None, :], ln_b[l, 1][None, :], nsc, nsh)
        else:
            (x,) = _residual_norm(x, ff, g2, ln_g[l, 1][None, :], ln_b[l, 1][None, :])
    return x
```

```python
import functools
import math

import numpy as np
import jax
import jax.numpy as jnp
from jax import lax
from jax.experimental import pallas as pl
from jax.experimental.pallas import tpu as pltpu

D_MODEL = 4096
DEPTH = 2
HEAD_DIM = 128
N_HEADS = D_MODEL // HEAD_DIM
H_A = N_HEADS // 4
H_C = N_HEADS // 4
H_B = N_HEADS - H_A - H_C
KV_B = max(1, H_B // 8)
G_B = H_B // KV_B
DIFF_DIM = HEAD_DIM // 2
MOBA_BLOCK = 256
MOBA_TOPK = 3
WINDOW = 128
N_BUCKETS = 32
MAX_DISTANCE = 128
D_FF = -(-8 * D_MODEL // (3 * 256)) * 256
ALPHA = (2.0 * DEPTH) ** 0.25
LN_EPS = 1e-5
NEG = -1e30
LOG2E = math.log2(math.e)

QA_W = H_A * HEAD_DIM
QB_W = H_B * HEAD_DIM
KB_W = KV_B * HEAD_DIM
QC_W = H_C * 2 * DIFF_DIM
VC_W = H_C * HEAD_DIM
PROJ_SIZES = (QA_W, QA_W, QA_W, QB_W, KB_W, KB_W, QC_W, QC_W, VC_W)
PROJ_OFFS = tuple(int(sum(PROJ_SIZES[:i])) for i in range(len(PROJ_SIZES)))
D_PROJ = sum(PROJ_SIZES)

LANE = 128
ATTN_TILE = 512
TILE_UNROLL = 4
CAST_ROW_TILE = 256
VMEM_LIMIT = 56 * 1024 * 1024
TILES_PROJ_IN = (1024, 512)
TILES_PROJ_IN_WIDE = (1024, 1024)
TILES_PROJ_OUT = (1024, 1024)
TILES_GATE_UP = (2048, 256)
TILES_DOWN = (1024, 1024, D_FF // 2)
NORM_ROWS = 512
NORM_CHUNK = 64
ADA_COLS = 1024
SWA_ROWS = 1024

F32 = jnp.float32
BF16 = jnp.bfloat16


def _bucket_upper_bounds():
    n = np.arange(0, 4 * MAX_DISTANCE, dtype=np.int64)
    max_exact = N_BUCKETS // 2
    nf = np.maximum(n, 1).astype(np.float32)
    large = max_exact + (np.log(nf / np.float32(max_exact)) / np.float32(math.log(MAX_DISTANCE / max_exact))
                         * np.float32(N_BUCKETS - max_exact)).astype(np.int32)
    large = np.minimum(large, N_BUCKETS - 1)
    bucket = np.where(n < max_exact, n, large)
    assert bucket[-1] == N_BUCKETS - 1 and np.all(np.diff(bucket) >= 0)
    return tuple(int(np.argmax(bucket > k)) for k in range(N_BUCKETS - 1))


BUCKET_UPPER = _bucket_upper_bounds()
FAR_DIST = BUCKET_UPPER[-1]


def _bias_from_dist(dist, tab):
    b = jnp.full(dist.shape, tab(N_BUCKETS - 1), F32)
    for k in range(N_BUCKETS - 2, -1, -1):
        b = jnp.where(dist < BUCKET_UPPER[k], tab(k), b)
    return b


def _dot(a, b):
    return jnp.dot(a, b, preferred_element_type=F32)


def _dot_nt(a, b):
    return lax.dot_general(a, b, (((1,), (1,)), ((), ())), preferred_element_type=F32)


def _params(*sem):
    return pltpu.CompilerParams(dimension_semantics=sem, vmem_limit_bytes=VMEM_LIMIT)


def _mod_kernel(c_ref, w_ref, b_ref, o_ref):
    c = c_ref[...]
    s = (c * jax.nn.sigmoid(c)).astype(BF16)
    o_ref[...] = _dot(s, w_ref[...].astype(BF16)) + b_ref[...]


def _ada_mod(c8, w_ada, b_ada3):
    depth, d, n = w_ada.shape
    tn = ADA_COLS
    return pl.pallas_call(
        _mod_kernel,
        grid=(depth, n // tn),
        in_specs=[pl.BlockSpec((8, d), lambda l, j: (0, 0)),
                  pl.BlockSpec((None, d, tn), lambda l, j: (l, 0, j)),
                  pl.BlockSpec((None, 1, tn), lambda l, j: (l, 0, j))],
        out_specs=pl.BlockSpec((None, 8, tn), lambda l, j: (l, 0, j)),
        out_shape=jax.ShapeDtypeStruct((depth, 8, n), F32),
        compiler_params=_params("parallel", "parallel"),
        name="ada_mod",
    )(c8, w_ada, b_ada3)


def _mm_kernel(x_ref, w_ref, *rest):
    o_ref = rest[-1]
    o_ref[...] = _dot(x_ref[...], w_ref[...]).astype(o_ref.dtype)


def _matmul(x, w, l, out_dtype, tm, tn, name, cols=None, into=None):
    m, k = x.shape
    _, _, n = w.shape
    c0, nc = cols or (0, n)
    assert c0 % tn == 0 and nc % tn == 0
    j0 = c0 // tn
    in_specs = [pl.BlockSpec((tm, k), lambda i, j: (i, 0)),
                pl.BlockSpec((None, k, tn), lambda i, j: (l, 0, j0 + j))]
    args = [x, w]
    if into is not None:
        in_specs.append(pl.BlockSpec(memory_space=pl.ANY))
        args.append(into)
    return pl.pallas_call(
        _mm_kernel,
        grid=(m // tm, nc // tn),
        in_specs=in_specs,
        out_specs=pl.BlockSpec((tm, tn), lambda i, j: (i, j0 + j)),
        out_shape=jax.ShapeDtypeStruct((m, n), out_dtype),
        input_output_aliases={} if into is None else {2: 0},
        compiler_params=_params("parallel", "arbitrary"),
        name=name,
    )(*args)


def _mod_mm_kernel(x_ref, sc_ref, sh_ref, w_ref, o_ref, h_ref):
    @pl.when(pl.program_id(1) == 0)
    def _():
        h_ref[...] = (x_ref[...] * (1.0 + sc_ref[...]) + sh_ref[...]).astype(BF16)

    o_ref[...] = _dot(h_ref[...], w_ref[...]).astype(o_ref.dtype)


def _modulate_matmul(x, sc, sh, w, l, out_dtype, tm, tn, name):
    b, s, k = x.shape
    _, _, n = w.shape
    assert s % tm == 0
    per_b = s // tm
    vec = pl.BlockSpec((None, 1, k), lambda i, j: (i // per_b, 0, 0))
    return pl.pallas_call(
        _mod_mm_kernel,
        grid=(b * per_b, n // tn),
        in_specs=[pl.BlockSpec((tm, k), lambda i, j: (i, 0)), vec, vec,
                  pl.BlockSpec((None, k, tn), lambda i, j: (l, 0, j))],
        out_specs=pl.BlockSpec((tm, tn), lambda i, j: (i, j)),
        out_shape=jax.ShapeDtypeStruct((b * s, n), out_dtype),
        scratch_shapes=[pltpu.VMEM((tm, k), BF16)],
        compiler_params=_params("parallel", "arbitrary"),
        name=name,
    )(x.reshape(b * s, k), sc, sh, w)


def _mm_acc_kernel(x_ref, w_ref, o_ref, acc_ref):
    kk = pl.program_id(2)
    part = _dot(x_ref[...], w_ref[...])

    @pl.when(kk == 0)
    def _():
        acc_ref[...] = part

    @pl.when((kk > 0) & (kk < pl.num_programs(2) - 1))
    def _():
        acc_ref[...] += part

    @pl.when(kk == pl.num_programs(2) - 1)
    def _():
        o_ref[...] = (acc_ref[...] + part).astype(o_ref.dtype)


def _matmul_ksplit(x, w, l, out_dtype, tm, tn, tk, name):
    m, k = x.shape
    _, _, n = w.shape
    assert k // tk >= 2
    return pl.pallas_call(
        _mm_acc_kernel,
        grid=(m // tm, n // tn, k // tk),
        in_specs=[pl.BlockSpec((tm, tk), lambda i, j, kk: (i, kk)),
                  pl.BlockSpec((None, tk, tn), lambda i, j, kk: (l, kk, j))],
        out_specs=pl.BlockSpec((tm, tn), lambda i, j, kk: (i, j)),
        out_shape=jax.ShapeDtypeStruct((m, n), out_dtype),
        scratch_shapes=[pltpu.VMEM((tm, tn), F32)],
        compiler_params=_params("parallel", "parallel", "arbitrary"),
        name=name,
    )(x, w)


def _cast_pad_cols_kernel(x_ref, o_ref):
    n = x_ref.shape[1]
    o_ref[:, :n] = x_ref[...].astype(BF16)
    if o_ref.shape[1] > n:
        o_ref[:, n:] = jnp.zeros((o_ref.shape[0], o_ref.shape[1] - n), BF16)


def _gate_up_kernel(h_ref, wg_ref, wu_ref, o_ref):
    h = h_ref[...]
    g = _dot(h, wg_ref[...])
    u = _dot(h, wu_ref[...])
    o_ref[...] = (g * jax.nn.sigmoid(g) * u).astype(BF16)


def _gate_up(h, wg, wu, l, tm, tf):
    m, k = h.shape
    _, _, f = wg.shape
    wspec = pl.BlockSpec((None, k, tf), lambda i, j: (l, 0, j))
    return pl.pallas_call(
        _gate_up_kernel,
        grid=(m // tm, f // tf),
        in_specs=[pl.BlockSpec((tm, k), lambda i, j: (i, 0)), wspec, wspec],
        out_specs=pl.BlockSpec((tm, tf), lambda i, j: (i, j)),
        out_shape=jax.ShapeDtypeStruct((m, f), BF16),
        compiler_params=_params("parallel", "arbitrary"),
        name="ffn_gate_up",
    )(h, wg, wu)


def _norm_kernel(with_h, x_ref, y_ref, gate_ref, lg_ref, lb_ref, *rest):
    def chunk(c, carry):
        r = pl.ds(pl.multiple_of(c * NORM_CHUNK, NORM_CHUNK), NORM_CHUNK)
        z = ALPHA * x_ref[r, :] + gate_ref[...] * y_ref[r, :].astype(F32)
        mu = jnp.mean(z, axis=-1, keepdims=True)
        zc = z - mu
        var = jnp.mean(zc * zc, axis=-1, keepdims=True)
        xn = zc * lax.rsqrt(var + LN_EPS) * lg_ref[...] + lb_ref[...]
        rest[-2 if with_h else -1][r, :] = xn
        if with_h:
            sc_ref, sh_ref = rest[0], rest[1]
            rest[-1][r, :] = (xn * (1.0 + sc_ref[...]) + sh_ref[...]).astype(BF16)
        return carry

    lax.fori_loop(0, x_ref.shape[0] // NORM_CHUNK, chunk, 0)


def _residual_norm(x, y, gate, lg, lb, sc=None, sh=None):
    b, s, d = x.shape
    tm = NORM_ROWS
    with_h = sc is not None
    tile = pl.BlockSpec((None, tm, d), lambda i, j: (i, j, 0))
    vec = pl.BlockSpec((None, 1, d), lambda i, j: (i, 0, 0))
    par = pl.BlockSpec((1, d), lambda i, j: (0, 0))
    in_specs = [tile, tile, vec, par, par]
    args = [x, y, gate, lg, lb]
    out_specs = [tile]
    out_shape = [jax.ShapeDtypeStruct((b, s, d), F32)]
    if with_h:
        in_specs += [vec, vec]
        args += [sc, sh]
        out_specs.append(tile)
        out_shape.append(jax.ShapeDtypeStruct((b, s, d), BF16))
    return pl.pallas_call(
        functools.partial(_norm_kernel, with_h),
        grid=(b, s // tm),
        in_specs=in_specs,
        out_specs=out_specs,
        out_shape=out_shape,
        compiler_params=_params("parallel", "parallel"),
        name="residual_norm",
    )(*args)


def _lane_tiles(a):
    return [a[:, i * LANE:(i + 1) * LANE] for i in range(a.shape[1] // LANE)]


def _row_max_rep(s):
    tiles = _lane_tiles(s)
    t = tiles[0]
    for u in tiles[1:]:
        t = jnp.maximum(t, u)
    return jnp.broadcast_to(jnp.max(t, axis=1, keepdims=True), t.shape)


def _probs(s, m_rep):
    return jnp.concatenate([jnp.exp2(u - m_rep) for u in _lane_tiles(s)], axis=1).astype(BF16)


def _softmax_reset(m_ref, acc_ref):
    m_ref[...] = jnp.full(m_ref.shape, NEG, F32)
    acc_ref[...] = jnp.zeros(acc_ref.shape, F32)


def _softmax_update(s, v_aug, m_ref, acc_ref):
    m_old = m_ref[...]
    m = jnp.maximum(m_old, _row_max_rep(s))
    corr = jnp.exp2(m_old - m)
    m_ref[...] = m
    acc_ref[...] = jnp.concatenate([corr, corr], axis=1) * acc_ref[...] + _dot(_probs(s, m), v_aug)


def _augment_values(v_ref, vaug_ref):
    vaug_ref[:, :HEAD_DIM] = v_ref[...]
    vaug_ref[:, HEAD_DIM:] = jnp.ones((v_ref.shape[0], HEAD_DIM), BF16)


def _bias_tiles(tab, bias_ref):
    t = bias_ref.shape[1]
    i = lax.broadcasted_iota(jnp.int32, (t, t), 0)
    j = lax.broadcasted_iota(jnp.int32, (t, t), 1)
    far = tab(N_BUCKETS - 1)
    d0 = (_bias_from_dist(i - j, tab) - far) * LOG2E
    bias_ref[2] = jnp.where(i >= j, d0, NEG)
    bias_ref[1] = (_bias_from_dist(i - j + t, tab) - far) * LOG2E
    bias_ref[0] = jnp.zeros((t, t), F32)


def _causal_tiles(qi, logits_fn, update_fn, sa_ref, sb_ref):
    n = qi + 1
    bufs = (sa_ref, sb_ref)
    sa_ref[...] = logits_fn(qi)

    def group(left0, size):
        for i in range(size):
            cur, nxt = bufs[i % 2], bufs[(i + 1) % 2]
            kt = qi - (left0 + i)
            nxt[...] = logits_fn(jnp.maximum(kt - 1, 0))
            update_fn(cur[...], left0 + i, kt)

    def quad(p, carry):
        group(TILE_UNROLL * p, TILE_UNROLL)
        return carry

    quads = n // TILE_UNROLL
    lax.fori_loop(0, quads, quad, 0)
    rest = n - TILE_UNROLL * quads

    @pl.when(rest >= 2)
    def _():
        group(TILE_UNROLL * quads, 2)

    @pl.when(rest % 2 == 1)
    def _():
        update_fn(sa_ref[...], qi, 0)


def _moba_kernel(n_side, tab_ref, q_ref, k_ref, v_ref, *rest):
    side_in, o_ref, side_out = rest[:n_side], rest[n_side], rest[n_side + 1:2 * n_side + 1]
    kmean_ref, bias_ref, vaug_ref, sa_ref, sb_ref, m_ref, acc_ref = rest[2 * n_side + 1:]
    for x_ref, y_ref in zip(side_in, side_out):
        _cast_pad_cols_kernel(x_ref, y_ref)
    h = pl.program_id(0)
    bi = pl.program_id(1)
    pj = pl.program_id(2)
    t = ATTN_TILE
    nq = q_ref.shape[0] // t
    blk = MOBA_BLOCK
    bpt = t // blk
    nb = k_ref.shape[0] // blk
    cs = HEAD_DIM ** -0.5 * LOG2E

    @pl.when((bi == 0) & (pj == 0))
    def _():
        _bias_tiles(lambda k: tab_ref[h, k], bias_ref)

    @pl.when(pj == 0)
    def _():
        for jb in range(nb):
            kb = k_ref[jb * blk:(jb + 1) * blk, :].astype(F32)
            kmean_ref[jb:jb + 1, :] = jnp.sum(kb, axis=0, keepdims=True) * (1.0 / blk)
        _augment_values(v_ref, vaug_ref)

    def query_tile(j, carry):
        qi = pj if j == 0 else nq - 1 - pj
        rows = pl.ds(pl.multiple_of(qi * t, t), t)
        q = q_ref[rows, :]
        km = kmean_ref[...]
        km_hi = km.astype(BF16)
        km_lo = (km - km_hi.astype(F32)).astype(BF16)
        gate = _dot_nt(km_hi, q) + _dot_nt(km_lo, q)
        jblk = lax.broadcasted_iota(jnp.int32, gate.shape, 0)
        qpos = lax.broadcasted_iota(jnp.int32, (1, t), 1)
        qblk = qi * bpt + qpos // blk
        g = jnp.where(jblk < qblk, gate, -jnp.inf)
        bits = jnp.left_shift(1, qblk)
        for r in range(MOBA_TOPK):
            mx = jnp.max(g, axis=0, keepdims=True)
            first = jnp.min(jnp.where(g == mx, jblk, nb), axis=0, keepdims=True)
            bits = bits | jnp.where(r < qblk, jnp.left_shift(1, first), 0)
            g = jnp.where(jblk == first, -jnp.inf, g)
        bits = jnp.broadcast_to(bits.astype(F32), (LANE, t)).T.astype(jnp.int32)

        def logits(kt):
            return _dot_nt(q, k_ref[pl.ds(pl.multiple_of(kt * t, t), t), :]) * cs

        def update(s, left, kt):
            s = s + bias_ref[jnp.maximum(2 - left, 0)]
            tiles = _lane_tiles(s)
            per_blk = blk // LANE
            out = []
            for c in range(bpt):
                vis = (jnp.right_shift(bits, kt * bpt + c) & 1) == 1
                out += [jnp.where(vis, u, NEG) for u in tiles[c * per_blk:(c + 1) * per_blk]]
            s = jnp.concatenate(out, axis=1)
            _softmax_update(s, vaug_ref[pl.ds(pl.multiple_of(kt * t, t), t), :], m_ref, acc_ref)

        _softmax_reset(m_ref, acc_ref)
        _causal_tiles(qi, logits, update, sa_ref, sb_ref)
        acc = acc_ref[...]
        o_ref[rows, :] = (acc[:, :HEAD_DIM] / acc[:, HEAD_DIM:]).astype(o_ref.dtype)
        return carry

    for j in range(2):
        query_tile(j, 0)


def _row_slab_specs(w, n_steps, step_fn, n_pad, row0=0):
    rows, n = w.shape[0] - row0, w.shape[1]
    assert rows % n_steps == 0 and (rows // n_steps) % 16 == 0 and row0 % (rows // n_steps) == 0
    tr = rows // n_steps
    first = row0 // tr
    return (pl.BlockSpec((tr, n), lambda *g: (first + step_fn(*g), 0)),
            pl.BlockSpec((tr, n_pad), lambda *g: (step_fn(*g), 0)),
            jax.ShapeDtypeStruct((rows, n_pad), BF16))


def _moba(proj, tab_a, side=()):
    b, s, _ = proj.shape
    t = ATTN_TILE
    assert s % (2 * t) == 0 and t % MOBA_BLOCK == 0 and t >= FAR_DIST and s // MOBA_BLOCK < 24
    nq = s // (2 * t)
    qo, ko, vo = (PROJ_OFFS[i] // HEAD_DIM for i in (0, 1, 2))
    side_specs = [_row_slab_specs(w, H_A * b * nq, lambda h, bi, qi: (h * b + bi) * nq + qi, w.shape[1]) for w in side]
    return pl.pallas_call(
        functools.partial(_moba_kernel, len(side)),
        grid=(H_A, b, nq),
        in_specs=[pl.BlockSpec(memory_space=pltpu.SMEM),
                  pl.BlockSpec((None, s, HEAD_DIM), lambda h, bi, qi: (bi, 0, qo + h)),
                  pl.BlockSpec((None, s, HEAD_DIM), lambda h, bi, qi: (bi, 0, ko + h)),
                  pl.BlockSpec((None, s, HEAD_DIM), lambda h, bi, qi: (bi, 0, vo + h))]
                 + [sp[0] for sp in side_specs],
        out_specs=[pl.BlockSpec((None, s, HEAD_DIM), lambda h, bi, qi: (bi, 0, h))] + [sp[1] for sp in side_specs],
        out_shape=[jax.ShapeDtypeStruct((b, s, D_MODEL), BF16)]
                  + [sp[2] for sp in side_specs],
        scratch_shapes=[pltpu.VMEM((s // MOBA_BLOCK, HEAD_DIM), F32),
                        pltpu.VMEM((3, t, t), F32),
                        pltpu.VMEM((s, 2 * HEAD_DIM), BF16),
                        pltpu.VMEM((t, t), F32), pltpu.VMEM((t, t), F32),
                        pltpu.VMEM((t, LANE), F32),
                        pltpu.VMEM((t, 2 * HEAD_DIM), F32)],
        compiler_params=_params("arbitrary", "arbitrary", "arbitrary"),
        name="moba_attention",
    )(tab_a, proj, proj, proj, *side)


def _swa_kernel(n_side, tab_ref, sink_ref, q_ref, k_ref, v_ref, cat_ref, *rest):
    del cat_ref
    side_in, o_ref, side_out = rest[:n_side], rest[n_side], rest[n_side + 1:2 * n_side + 1]
    bias_ref, sink_rep_ref, vaug_ref, sa_ref, sb_ref = rest[2 * n_side + 1:]
    for x_ref, y_ref in zip(side_in, side_out):
        _cast_pad_cols_kernel(x_ref, y_ref)
    g = pl.program_id(1)
    ti = pl.program_id(2)
    w = WINDOW
    nsub = q_ref.shape[0] // w
    cs = HEAD_DIM ** -0.5 * LOG2E

    @pl.when(ti == 0)
    def _():
        i = lax.broadcasted_iota(jnp.int32, (w, 2 * w), 0)
        j = lax.broadcasted_iota(jnp.int32, (w, 2 * w), 1)
        dist = i + w - j
        band = (dist >= 0) & (dist < w)
        for hh in range(G_B):
            head = g * G_B + hh
            bias = jnp.where(band, _bias_from_dist(dist, lambda k: tab_ref[head, k]) * LOG2E, NEG)
            bias_ref[0, hh * w:(hh + 1) * w, :] = bias
            bias_ref[1, hh * w:(hh + 1) * w, :] = jnp.where(j >= w, bias, NEG)
            sink_rep_ref[hh * w:(hh + 1) * w, :] = jnp.full((w, LANE), sink_ref[head] * LOG2E, F32)
        _augment_values(v_ref, vaug_ref)

    def band_rows(ref, blk):
        k0 = pl.multiple_of(jnp.maximum(blk - 1, 0) * w, w)
        k1 = pl.multiple_of(blk * w, w)
        return jnp.concatenate([ref[pl.ds(k0, w), :], ref[pl.ds(k1, w), :]], axis=0)

    def logits(n):
        qs = jnp.concatenate([q_ref[n * w:(n + 1) * w, hh * HEAD_DIM:(hh + 1) * HEAD_DIM] for hh in range(G_B)], axis=0)
        return _dot_nt(qs, band_rows(k_ref, ti * nsub + n)) * cs

    def finish(s, n):
        blk = ti * nsub + n
        s = s + bias_ref[jnp.where(blk == 0, 1, 0)]
        sink = sink_rep_ref[...]
        m = jnp.maximum(_row_max_rep(s), sink)
        acc = _dot(_probs(s, m), band_rows(vaug_ref, blk))
        o = acc[:, :HEAD_DIM] / (acc[:, HEAD_DIM:] + jnp.exp2(sink - m))
        for hh in range(G_B):
            o_ref[n * w:(n + 1) * w, hh * HEAD_DIM:(hh + 1) * HEAD_DIM] = o[hh * w:(hh + 1) * w, :].astype(o_ref.dtype)

    bufs = (sa_ref, sb_ref)
    sa_ref[...] = logits(0)
    for n in range(nsub):
        if n + 1 < nsub:
            bufs[(n + 1) % 2][...] = logits(n + 1)
        finish(bufs[n % 2][...], n)


def _swa(proj, tab_b, sinks, cat, side=()):
    b, s, _ = proj.shape
    tq = SWA_ROWS
    nt = s // tq
    gw = G_B * HEAD_DIM
    qo = PROJ_OFFS[3] // gw
    oo = QA_W // gw
    ko, vo = (PROJ_OFFS[i] // HEAD_DIM for i in (4, 5))
    assert PROJ_OFFS[3] % gw == 0 and QA_W % gw == 0 and s % tq == 0
    smem = pl.BlockSpec(memory_space=pltpu.SMEM)
    side_specs = [_row_slab_specs(w, b * KV_B * nt, lambda bi, g, ti: (bi * KV_B + g) * nt + ti, w.shape[1], row0)
                  for w, row0 in side]
    side = [w for w, _ in side]
    return pl.pallas_call(
        functools.partial(_swa_kernel, len(side)),
        grid=(b, KV_B, nt),
        in_specs=[smem, smem,
                  pl.BlockSpec((None, tq, gw), lambda bi, g, ti: (bi, ti, qo + g)),
                  pl.BlockSpec((None, s, HEAD_DIM), lambda bi, g, ti: (bi, 0, ko + g)),
                  pl.BlockSpec((None, s, HEAD_DIM), lambda bi, g, ti: (bi, 0, vo + g)),
                  pl.BlockSpec(memory_space=pl.ANY)] + [sp[0] for sp in side_specs],
        out_specs=[pl.BlockSpec((None, tq, gw), lambda bi, g, ti: (bi, ti, oo + g))] + [sp[1] for sp in side_specs],
        out_shape=[jax.ShapeDtypeStruct(cat.shape, cat.dtype)] + [sp[2] for sp in side_specs],
        input_output_aliases={5: 0},
        scratch_shapes=[pltpu.VMEM((2, G_B * WINDOW, 2 * WINDOW), F32),
                        pltpu.VMEM((G_B * WINDOW, LANE), F32),
                        pltpu.VMEM((s, 2 * HEAD_DIM), BF16),
                        pltpu.VMEM((G_B * WINDOW, 2 * WINDOW), F32),
                        pltpu.VMEM((G_B * WINDOW, 2 * WINDOW), F32)],
        compiler_params=_params("parallel", "parallel", "arbitrary"),
        name="swa_attention",
    )(tab_b, sinks, proj, proj, proj, cat, *side)


def _diff_kernel(lam_init, n_cols, row_tiles, tab_ref, q_ref, k_ref, v_ref, lamp_ref, subg_ref, cat_ref, *rest):
    del cat_ref
    n_side = n_cols + (row_tiles is not None)
    side_in, o_ref, side_out = rest[:n_side], rest[n_side], rest[n_side + 1:2 * n_side + 1]
    bias_ref, vaug_ref, sa_ref, sb_ref, m_ref, acc_ref = rest[2 * n_side + 1:]
    for x_ref, y_ref in zip(side_in[:n_cols], side_out[:n_cols]):
        _cast_pad_cols_kernel(x_ref, y_ref)
    side_in, side_out = side_in[n_cols:], side_out[n_cols:]
    h = pl.program_id(0)
    bi = pl.program_id(1)
    pj = pl.program_id(2)
    t = ATTN_TILE
    nq = q_ref.shape[0] // t
    cs = DIFF_DIM ** -0.5 * LOG2E

    if row_tiles is not None:
        n_src, n_dst, depth = row_tiles
        step = (h * pl.num_programs(1) + bi) * pl.num_programs(2) + pj

        @pl.when((step < depth * n_dst) & (step % n_dst < n_src))
        def _():
            side_out[0][...] = side_in[0][...].astype(BF16)

        @pl.when((step < depth * n_dst) & (step % n_dst >= n_src))
        def _():
            side_out[0][...] = jnp.zeros(side_out[0].shape, BF16)

    @pl.when((bi == 0) & (pj == 0))
    def _():
        _bias_tiles(lambda k: tab_ref[h, k], bias_ref)

    @pl.when(pj == 0)
    def _():
        _augment_values(v_ref, vaug_ref)

    lp = lamp_ref[...]
    lam = (jnp.exp(jnp.sum(lp[0:1] * lp[1:2], axis=1, keepdims=True))
           - jnp.exp(jnp.sum(lp[2:3] * lp[3:4], axis=1, keepdims=True)) + lam_init)

    def query_tile(j, carry):
        qi = pj if j == 0 else nq - 1 - pj
        rows = pl.ds(pl.multiple_of(qi * t, t), t)
        q = q_ref[rows, :]
        lane = lax.broadcasted_iota(jnp.int32, q.shape, 1)
        zero = jnp.zeros_like(q)
        q2 = jnp.concatenate([jnp.where(lane < DIFF_DIM, q, zero), jnp.where(lane >= DIFF_DIM, q, zero)], axis=0)

        def logits(kt):
            return _dot_nt(q2, k_ref[pl.ds(pl.multiple_of(kt * t, t), t), :]) * cs

        def update(s, left, kt):
            d = bias_ref[jnp.maximum(2 - left, 0)]
            s = jnp.concatenate([s[:t] + d, s[t:] + d], axis=0)
            _softmax_update(s, vaug_ref[pl.ds(pl.multiple_of(kt * t, t), t), :], m_ref, acc_ref)

        _softmax_reset(m_ref, acc_ref)
        _causal_tiles(qi, logits, update, sa_ref, sb_ref)

        acc = acc_ref[...]
        a = acc[:, :HEAD_DIM] / acc[:, HEAD_DIM:]
        o = a[:t] - lam * a[t:]
        o = o * lax.rsqrt(jnp.mean(o * o, axis=-1, keepdims=True) + LN_EPS) * subg_ref[...]
        o_ref[rows, :] = (o * (1.0 - lam_init)).astype(o_ref.dtype)
        return carry

    for j in range(2):
        query_tile(j, 0)


def _diff(proj, tab_c, lam_p, subln_g, lam_init, cat, cast_cols=(), pad_rows=None):
    b, s, _ = proj.shape
    t = ATTN_TILE
    assert s % (2 * t) == 0 and t >= FAR_DIST
    nq = s // (2 * t)
    qo, ko, vo = (PROJ_OFFS[i] // HEAD_DIM for i in (6, 7, 8))
    oo = (QA_W + QB_W) // HEAD_DIM
    col_specs = [_row_slab_specs(w, H_C * b * nq, lambda h, bi, qi: (h * b + bi) * nq + qi, w.shape[1], row0)
                 for w, row0 in cast_cols]
    side = [w for w, _ in cast_cols]
    side_in, side_out, side_shape = ([sp[i] for sp in col_specs] for i in range(3))
    row_tiles = None
    if pad_rows is not None:
        w, k_pad = pad_rows
        depth, k, n = w.shape
        tr = CAST_ROW_TILE
        assert k % tr == 0 and k_pad % tr == 0
        n_src, n_dst = k // tr, k_pad // tr
        assert depth * n_dst <= H_C * b * nq
        row_tiles = (n_src, n_dst, depth)

        def dst_tile(h, bi, qi):
            return jnp.minimum((h * b + bi) * nq + qi, depth * n_dst - 1)

        def src_tile(h, bi, qi):
            d = dst_tile(h, bi, qi)
            return (d // n_dst) * n_src + jnp.minimum(d % n_dst, n_src - 1)

        side.append(w.reshape(depth * k, n))
        side_in.append(pl.BlockSpec((tr, n), lambda h, bi, qi: (src_tile(h, bi, qi), 0)))
        side_out.append(pl.BlockSpec((tr, n), lambda h, bi, qi: (dst_tile(h, bi, qi), 0)))
        side_shape.append(jax.ShapeDtypeStruct((depth * k_pad, n), BF16))
    return pl.pallas_call(
        functools.partial(_diff_kernel, lam_init, len(cast_cols), row_tiles),
        grid=(H_C, b, nq),
        in_specs=[pl.BlockSpec(memory_space=pltpu.SMEM),
                  pl.BlockSpec((None, s, HEAD_DIM), lambda h, bi, qi: (bi, 0, qo + h)),
                  pl.BlockSpec((None, s, HEAD_DIM), lambda h, bi, qi: (bi, 0, ko + h)),
                  pl.BlockSpec((None, s, HEAD_DIM), lambda h, bi, qi: (bi, 0, vo + h)),
                  pl.BlockSpec((4, DIFF_DIM), lambda h, bi, qi: (0, 0)),
                  pl.BlockSpec((1, HEAD_DIM), lambda h, bi, qi: (0, 0)),
                  pl.BlockSpec(memory_space=pl.ANY)] + side_in,
        out_specs=[pl.BlockSpec((None, s, HEAD_DIM), lambda h, bi, qi: (bi, 0, oo + h))] + side_out,
        out_shape=[jax.ShapeDtypeStruct(cat.shape, cat.dtype)] + side_shape,
        input_output_aliases={6: 0},
        scratch_shapes=[pltpu.VMEM((3, t, t), F32),
                        pltpu.VMEM((s, 2 * HEAD_DIM), BF16),
                        pltpu.VMEM((2 * t, t), F32), pltpu.VMEM((2 * t, t), F32),
                        pltpu.VMEM((2 * t, LANE), F32),
                        pltpu.VMEM((2 * t, 2 * HEAD_DIM), F32)],
        compiler_params=_params("arbitrary", "arbitrary", "arbitrary"),
        name="diff_attention",
    )(tab_c, proj, proj, proj, lam_p, subln_g, cat, *side)


def kernel(x, c, rel_bias, w_ada, b_ada, w_in, w_o, attn_sinks, diff_lambda, diff_subln_g,
           ln_g, ln_b, w_gate, w_up, w_down):
    b, s, d = x.shape
    m = b * s
    tab = rel_bias.T
    tab_a, tab_b, tab_c = tab[:H_A], tab[H_A:H_A + H_B], tab[H_A + H_B:]

    c8 = jnp.pad(c, ((0, 8 - b), (0, 0)))
    mod = _ada_mod(c8, w_ada, b_ada[:, None, :])[:, :b]
    mod = mod.reshape(DEPTH, b, 6, 1, d)

    w_in_b = [w_in[:1].astype(BF16), None]
    h = None
    for l in range(DEPTH):
        sh1, sc1, g1, sh2, sc2, g2 = (mod[l, :, i] for i in range(6))
        if h is None:
            proj = _modulate_matmul(x, sc1, sh1, w_in_b[l], 0, BF16, *TILES_PROJ_IN, "proj_in")
        else:
            wide = TILES_PROJ_IN_WIDE[1]
            n_wide = D_PROJ // wide * wide
            proj = _matmul(h.reshape(m, d), w_in_b[l], 0, BF16, *TILES_PROJ_IN_WIDE, "proj_in", (0, n_wide))
            proj = _matmul(h.reshape(m, d), w_in_b[l], 0, BF16, *TILES_PROJ_IN, "proj_in_tail",
                           (n_wide, D_PROJ - n_wide), proj)
        proj = proj.reshape(b, s, D_PROJ)
        lam_init = 0.8 - 0.6 * math.exp(-0.3 * l)
        if l == 0:
            assert DEPTH == 2
            cat, wg_b, wu_b = _moba(proj, tab_a, (w_gate.reshape(DEPTH * d, D_FF), w_up.reshape(DEPTH * d, D_FF)))
            cat, w_o_b = _swa(proj, tab_b, attn_sinks[l], cat, ((w_o.reshape(DEPTH * d, d), 0),))
            cat, w_in1_b, wd_b = _diff(proj, tab_c, diff_lambda[l], diff_subln_g[l][None, :], lam_init, cat,
                                       ((w_in.reshape(DEPTH * d, D_PROJ), d),), (w_down, D_FF))
            wg_b, wu_b = wg_b.reshape(DEPTH, d, D_FF), wu_b.reshape(DEPTH, d, D_FF)
            w_o_b, wd_b = w_o_b.reshape(DEPTH, d, d), wd_b.reshape(DEPTH, D_FF, d)
            w_in_b[1] = w_in1_b[None]
        else:
            (cat,) = _moba(proj, tab_a)
            (cat,) = _swa(proj, tab_b, attn_sinks[l], cat)
            (cat,) = _diff(proj, tab_c, diff_lambda[l], diff_subln_g[l][None, :], lam_init, cat)
        mix = _matmul(cat.reshape(m, d), w_o_b, l, BF16, *TILES_PROJ_OUT, "proj_out").reshape(b, s, d)
        x, h = _residual_norm(x, mix, g1, ln_g[l, 0][None, :], ln_b[l, 0][None, :], sc2, sh2)

        act = _gate_up(h.reshape(m, d), wg_b, wu_b, l, *TILES_GATE_UP)
        ff = _matmul_ksplit(act, wd_b, l, BF16, *TILES_DOWN, "ffn_down").reshape(b, s, d)
        if l + 1 < DEPTH:
            nsh, nsc = mod[l + 1, :, 0], mod[l + 1, :, 1]
            x, h = _residual_norm(x, ff, g2, ln_g[l, 1][None, :], ln_b[l, 1][None, :], nsc, nsh)
        else:
            (x,) = _residual_norm(x, ff, g2, ln_g[l, 1][None, :], ln_b[l, 1][None, :])
    return x
```
